```python
import jax
import jax.numpy as jnp
from jax import lax
import numpy as np

D_MODEL = 4096
BATCH = 2
SEQ = 4096
DEPTH = 2

CTX_LEN = 256
GRID_W = 64
N_BRANCHES = 4
BRANCH_W = D_MODEL // 4
FOURIER_GROUPS = 4
FOURIER_GW = BRANCH_W // FOURIER_GROUPS
MLSTM_HEADS = 4
MLSTM_DV = BRANCH_W // MLSTM_HEADS
MLSTM_DQK = MLSTM_DV // 2
MLSTM_CHUNK = 128
RG_BLOCKS = 8
RG_BW = BRANCH_W // RG_BLOCKS
RG_C = 8.0
RG_CONV = 4
RG_CONV_LEFT = 2
SC_CONV = 3
SC_CONV_LEFT = 1
PEER_HEADS = 8
PEER_NKEYS = 128
PEER_N = PEER_NKEYS * PEER_NKEYS
PEER_DK = 256
PEER_DKH = PEER_DK // 2
PEER_TOPK = 16
PEER_BLOCK = 64
EPS = 1e-6

COL_SIZES = (
    BRANCH_W,
    MLSTM_HEADS * MLSTM_DQK,
    MLSTM_HEADS * MLSTM_DQK,
    BRANCH_W,
    BRANCH_W,
    2 * 2 * MLSTM_HEADS,
    BRANCH_W,
    BRANCH_W,
    BRANCH_W,
    BRANCH_W,
    BRANCH_W,
    N_BRANCHES * D_MODEL,
)
IN_COLS = sum(COL_SIZES)

kernel_name = 'hybrid_gated_branch_diffusion_block'


def rmsnorm(x, g):
    xf = x.astype(jnp.float32)
    y = xf * lax.rsqrt(jnp.mean(xf * xf, axis=-1, keepdims=True) + EPS)
    return (y * g.astype(jnp.float32)).astype(x.dtype)


def modulate(h, shift, scale):
    return h * (1 + scale) + shift


def split_cols(p):
    out, start = [], 0
    for size in COL_SIZES:
        out.append(p[..., start:start + size])
        start += size
    return out


def dwconv(u, w, pad_l):
    k_w, length = w.shape[0], u.shape[-2]
    pad = [(0, 0)] * (u.ndim - 2) + [(pad_l, k_w - 1 - pad_l), (0, 0)]
    up = jnp.pad(u, pad)
    y = w[0] * up[..., 0:length, :]
    for j in range(1, k_w):
        y = y + w[j] * up[..., j:j + length, :]
    return y


def conv_latent(u, w, pad_l):
    bsz, t, ch = u.shape
    rows = t // GRID_W
    y = dwconv(u.reshape(bsz, rows, GRID_W, ch), w, pad_l)
    return y.reshape(bsz, t, ch)


def fourier_mix(u):
    bsz, t, _ = u.shape
    g = u.astype(jnp.float32).reshape(bsz, t, FOURIER_GROUPS, FOURIER_GW)
    z = jnp.fft.fft2(g, axes=(1, 3), norm='ortho').real
    return z.reshape(bsz, t, BRANCH_W).astype(u.dtype)


def mlstm_scan(q, k, v, li, lf, state):
    bsz, nh, t, _ = q.shape
    nc = t // MLSTM_CHUNK

    def chunks(a):
        a = a.reshape(a.shape[:2] + (nc, MLSTM_CHUNK) + a.shape[3:])
        return jnp.moveaxis(a, 2, 0)

    tril = jnp.tril(jnp.ones((MLSTM_CHUNK, MLSTM_CHUNK), dtype=bool))

    def step(carry, inp):
        cmat, nvec, m = carry
        qc, kc, vc, ic, fc = inp
        b = jnp.cumsum(fc, axis=-1)
        logw = jnp.where(tril, b[..., :, None] - b[..., None, :] + ic[..., None, :], -jnp.inf)
        g = b + m[..., None]
        mt = jnp.maximum(g, jnp.max(logw, axis=-1))
        s = jnp.einsum('bhtd,bhsd->bhts', qc, kc) * jnp.exp(logw - mt[..., None])
        w_inter = jnp.exp(g - mt)
        num = jnp.einsum('bhts,bhsv->bhtv', s, vc) + w_inter[..., None] * jnp.einsum('bhvd,bhtd->bhtv', cmat, qc)
        den = jnp.sum(s, axis=-1) + w_inter * jnp.einsum('bhd,bhtd->bht', nvec, qc)
        h = num / jnp.maximum(jnp.abs(den), jnp.exp(-mt))[..., None]
        bl = b[..., -1]
        logu = bl[..., None] - b + ic
        m_new = jnp.maximum(bl + m, jnp.max(logu, axis=-1))
        ws = jnp.exp(logu - m_new[..., None])
        wc = jnp.exp(bl + m - m_new)
        c_new = wc[..., None, None] * cmat + jnp.einsum('bhs,bhsv,bhsd->bhvd', ws, vc, kc)
        n_new = wc[..., None] * nvec + jnp.einsum('bhs,bhsd->bhd', ws, kc)
        return (c_new, n_new, m_new), h

    state, hs = lax.scan(step, state, (chunks(q), chunks(k), chunks(v), chunks(li), chunks(lf)))
    hs = jnp.moveaxis(hs, 0, 2).reshape(bsz, nh, t, MLSTM_DV)
    return hs, state


def mlstm_heads(p_q, p_k, p_v, p_g, gate_b):
    bsz, t, _ = p_q.shape
    f32 = jnp.float32
    q = p_q.astype(f32).reshape(bsz, t, MLSTM_HEADS, MLSTM_DQK).transpose(0, 2, 1, 3) * (MLSTM_DQK ** -0.5)
    k = p_k.astype(f32).reshape(bsz, t, MLSTM_HEADS, MLSTM_DQK).transpose(0, 2, 1, 3)
    v = p_v.astype(f32).reshape(bsz, t, MLSTM_HEADS, MLSTM_DV).transpose(0, 2, 1, 3)
    g = p_g.astype(f32).reshape(bsz, t, 2, 2, MLSTM_HEADS) + gate_b.astype(f32)
    g = g.transpose(2, 3, 0, 4, 1)
    return q, k, v, g


def mlstm_bidir(lat, ctx, gate_b):
    ql, kl, vl, gl = mlstm_heads(*lat, gate_b)
    qc, kc, vc, gc = mlstm_heads(*ctx, gate_b)
    bsz = ql.shape[0]
    outs_l, outs_c = [], []
    for d in range(2):
        flip = (lambda a: jnp.flip(a, axis=2)) if d == 1 else (lambda a: a)
        init = (jnp.zeros((bsz, MLSTM_HEADS, MLSTM_DV, MLSTM_DQK), jnp.float32),
                jnp.zeros((bsz, MLSTM_HEADS, MLSTM_DQK), jnp.float32),
                jnp.zeros((bsz, MLSTM_HEADS), jnp.float32))
        hc, st = mlstm_scan(flip(qc), flip(kc), flip(vc), flip(gc[d, 0]), flip(jax.nn.log_sigmoid(gc[d, 1])), init)
        hl, _ = mlstm_scan(flip(ql), flip(kl), flip(vl), flip(gl[d, 0]), flip(jax.nn.log_sigmoid(gl[d, 1])), st)
        outs_l.append(flip(hl))
        outs_c.append(flip(hc))
    return outs_l[0] + outs_l[1], outs_c[0] + outs_c[1]


def mlstm_out(h, p_o, norm_g):
    h = h * lax.rsqrt(jnp.mean(h * h, axis=-1, keepdims=True) + EPS)
    bsz, nh, t, dv = h.shape
    h = h.transpose(0, 2, 1, 3).reshape(bsz, t, nh * dv) * norm_g.astype(jnp.float32)
    return (h * jax.nn.sigmoid(p_o.astype(jnp.float32))).astype(p_o.dtype)


def _lin_combine(e1, e2):
    a1, b1 = e1
    a2, b2 = e2
    return a1 * a2, a2 * b1 + b2


def rglru_scan(u, h0, wa, ba, wx, bx, lam):
    bsz, t, w = u.shape
    f32 = jnp.float32
    ub = u.reshape(bsz, t, RG_BLOCKS, RG_BW)
    r = jax.nn.sigmoid(jnp.einsum('btgi,gij->btgj', ub, wa.astype(f32)).reshape(bsz, t, w) + ba.astype(f32))
    i = jax.nn.sigmoid(jnp.einsum('btgi,gij->btgj', ub, wx.astype(f32)).reshape(bsz, t, w) + bx.astype(f32))
    log_a = -RG_C * jax.nn.softplus(-lam.astype(f32)) * r
    a = jnp.exp(log_a)
    b = jnp.sqrt(-jnp.expm1(2.0 * log_a)) * (i * u)
    a_cum, h_part = lax.associative_scan(_lin_combine, (a, b), axis=1)
    h = h_part + a_cum * h0[:, None, :]
    return h, h[:, -1]


def rglru_bidir(ul, uc, wa, ba, wx, bx, lam):
    bsz = ul.shape[0]
    outs_l, outs_c = [], []
    for d in range(2):
        flip = (lambda a: jnp.flip(a, axis=1)) if d == 1 else (lambda a: a)
        h0 = jnp.zeros((bsz, BRANCH_W), jnp.float32)
        hc, st = rglru_scan(flip(uc), h0, wa[d], ba[d], wx[d], bx[d], lam[d])
        hl, _ = rglru_scan(flip(ul), st, wa[d], ba[d], wx[d], bx[d], lam[d])
        outs_l.append(flip(hl))
        outs_c.append(flip(hc))
    return outs_l[0] + outs_l[1], outs_c[0] + outs_c[1]


def merge_branches(gate_cols, ys, w_branch, w_out):
    bsz, t, _ = gate_cols.shape
    gates = jax.nn.sigmoid(gate_cols.reshape(bsz, t, N_BRANCHES, D_MODEL))
    m = gates[:, :, 0] * (ys[0] @ w_branch[0])
    for b in range(1, N_BRANCHES):
        m = m + gates[:, :, b] * (ys[b] @ w_branch[b])
    return m @ w_out


def token_mixing(h_lat, h_ctx, w_in, w_branch, w_out, mlstm_gate_b, mlstm_norm_g, rg_conv_w, rg_conv_b,
                 rg_wa, rg_ba, rg_wx, rg_bx, rg_lam, sc_conv_w, need_ctx):
    f32 = jnp.float32
    pl = split_cols(h_lat @ w_in)
    pc = split_cols(h_ctx @ w_in)
    ml_l, ml_c = mlstm_bidir((pl[1], pl[2], pl[3], pl[5]), (pc[1], pc[2], pc[3], pc[5]), mlstm_gate_b)
    ul = conv_latent(pl[6].astype(f32), rg_conv_w.astype(f32), RG_CONV_LEFT) + rg_conv_b.astype(f32)
    uc = dwconv(pc[6].astype(f32), rg_conv_w.astype(f32), RG_CONV_LEFT) + rg_conv_b.astype(f32)
    rg_l, rg_c = rglru_bidir(ul, uc, rg_wa, rg_ba, rg_wx, rg_bx, rg_lam)

    def stream_out(p, conv, mh, rh):
        y_four = fourier_mix(p[0])
        y_ml = mlstm_out(mh, p[4], mlstm_norm_g)
        y_rg = (jax.nn.gelu(p[7].astype(f32)) * rh).astype(p[7].dtype)
        y_sc = p[8] * conv(p[9] * p[10], sc_conv_w, SC_CONV_LEFT)
        return merge_branches(p[11], (y_four, y_ml, y_rg, y_sc), w_branch, w_out)

    y_lat = stream_out(pl, conv_latent, ml_l, rg_l)
    y_ctx = stream_out(pc, dwconv, ml_c, rg_c) if need_ctx else None
    return y_lat, y_ctx


def peer_ffn(h, w_q, keys, u_tab, v_tab):
    bsz, t, d = h.shape
    flat = h.reshape(bsz * t, d)
    n_tok = flat.shape[0]
    q = (flat @ w_q).astype(jnp.float32).reshape(n_tok, PEER_HEADS, 2, PEER_DKH)
    s = jnp.einsum('nhpd,hpkd->nhpk', q, keys.astype(jnp.float32))
    sv, si = lax.top_k(s, PEER_TOPK)
    n_cand = PEER_TOPK * PEER_TOPK
    cand = (sv[:, :, 0, :, None] + sv[:, :, 1, None, :]).reshape(n_tok, PEER_HEADS, n_cand)
    cand_id = (si[:, :, 0, :, None] * PEER_NKEYS + si[:, :, 1, None, :]).reshape(n_tok, PEER_HEADS, n_cand)
    top_s, top_i = lax.top_k(cand, PEER_TOPK)
    ids = jnp.take_along_axis(cand_id, top_i, axis=-1)
    wts = jax.nn.softmax(top_s, axis=-1)
    nb = n_tok // PEER_BLOCK

    def block(args):
        hb, ib, wb = args
        ub = jnp.take(u_tab, ib, axis=0)
        act = jax.nn.gelu(jnp.einsum('pd,phkd->phk', hb, ub).astype(jnp.float32))
        vb = jnp.take(v_tab, ib, axis=0)
        return jnp.einsum('phk,phkd->pd', (wb * act).astype(vb.dtype), vb)

    out = lax.map(block, (flat.reshape(nb, PEER_BLOCK, d),
                          ids.reshape(nb, PEER_BLOCK, PEER_HEADS, PEER_TOPK),
                          wts.reshape(nb, PEER_BLOCK, PEER_HEADS, PEER_TOPK)))
    return out.reshape(bsz, t, d).astype(h.dtype)


def setup_inputs(seed: int = 0) -> dict:
    key = jax.random.key(seed)
    ks = jax.random.split(key, 28)
    f32 = jnp.float32

    def nrm(k, shape, scale):
        return jax.random.normal(k, shape, f32) * scale

    x = nrm(ks[0], (BATCH, SEQ, D_MODEL), 1.0)
    c = nrm(ks[1], (BATCH, D_MODEL), 1.0)
    ctx = nrm(ks[2], (BATCH, CTX_LEN, D_MODEL), 1.0)
    c_ctx = nrm(ks[3], (D_MODEL,), 1.0)
    w_mod = nrm(ks[4], (DEPTH, D_MODEL, 6 * D_MODEL), 0.5 * D_MODEL ** -0.5)
    b_mod = nrm(ks[5], (DEPTH, 6 * D_MODEL), 0.02)
    g_norm1 = 1.0 + nrm(ks[6], (DEPTH, D_MODEL), 0.02)
    g_norm2 = 1.0 + nrm(ks[7], (DEPTH, D_MODEL), 0.02)
    w_in = nrm(ks[8], (DEPTH, D_MODEL, IN_COLS), D_MODEL ** -0.5)
    w_branch = nrm(ks[9], (DEPTH, N_BRANCHES, BRANCH_W, D_MODEL), BRANCH_W ** -0.5)
    w_out = nrm(ks[10], (DEPTH, D_MODEL, D_MODEL), D_MODEL ** -0.5)
    i_bias = nrm(ks[11], (DEPTH, 2, MLSTM_HEADS), 0.1)
    f_bias = jnp.linspace(3.0, 6.0, MLSTM_HEADS, dtype=f32) + nrm(ks[12], (DEPTH, 2, MLSTM_HEADS), 0.1)
    mlstm_gate_b = jnp.stack([i_bias, f_bias], axis=2)
    mlstm_norm_g = 1.0 + nrm(ks[13], (DEPTH, BRANCH_W), 0.02)
    rg_conv_w = nrm(ks[14], (DEPTH, RG_CONV, BRANCH_W), RG_CONV ** -0.5)
    rg_conv_b = nrm(ks[15], (DEPTH, BRANCH_W), 0.02)
    rg_wa = nrm(ks[16], (DEPTH, 2, RG_BLOCKS, RG_BW, RG_BW), RG_BW ** -0.5)
    rg_ba = nrm(ks[17], (DEPTH, 2, BRANCH_W), 0.02)
    rg_wx = nrm(ks[18], (DEPTH, 2, RG_BLOCKS, RG_BW, RG_BW), RG_BW ** -0.5)
    rg_bx = nrm(ks[19], (DEPTH, 2, BRANCH_W), 0.02)
    a0 = jax.random.uniform(ks[20], (DEPTH, 2, BRANCH_W), f32, 0.9, 0.999)
    p = a0 ** (1.0 / RG_C)
    rg_lam = jnp.log(p) - jnp.log1p(-p)
    sc_conv_w = nrm(ks[21], (DEPTH, SC_CONV, BRANCH_W), SC_CONV ** -0.5)
    peer_wq = nrm(ks[22], (DEPTH, D_MODEL, PEER_HEADS * PEER_DK), D_MODEL ** -0.5)
    peer_keys = nrm(ks[23], (DEPTH, PEER_HEADS, 2, PEER_NKEYS, PEER_DKH), PEER_DKH ** -0.5)
    peer_u = nrm(ks[24], (DEPTH, PEER_N, D_MODEL), D_MODEL ** -0.5)
    peer_v = nrm(ks[25], (DEPTH, PEER_N, D_MODEL), 1.0)
    g_final = 1.0 + nrm(ks[26], (D_MODEL,), 0.02)
    return {'x': x, 'c': c, 'ctx': ctx, 'c_ctx': c_ctx, 'w_mod': w_mod, 'b_mod': b_mod,
            'g_norm1': g_norm1, 'g_norm2': g_norm2, 'w_in': w_in, 'w_branch': w_branch, 'w_out': w_out,
            'mlstm_gate_b': mlstm_gate_b, 'mlstm_norm_g': mlstm_norm_g, 'rg_conv_w': rg_conv_w,
            'rg_conv_b': rg_conv_b, 'rg_wa': rg_wa, 'rg_ba': rg_ba, 'rg_wx': rg_wx, 'rg_bx': rg_bx,
            'rg_lam': rg_lam, 'sc_conv_w': sc_conv_w, 'peer_wq': peer_wq, 'peer_keys': peer_keys,
            'peer_u': peer_u, 'peer_v': peer_v, 'g_final': g_final}


def reference(x, c, ctx, c_ctx, w_mod, b_mod, g_norm1, g_norm2, w_in, w_branch, w_out, mlstm_gate_b,
              mlstm_norm_g, rg_conv_w, rg_conv_b, rg_wa, rg_ba, rg_wx, rg_bx, rg_lam, sc_conv_w,
              peer_wq, peer_keys, peer_u, peer_v, g_final):
    xc = ctx
    for l in range(DEPTH):
        need_ctx = l < DEPTH - 1
        mod = jax.nn.silu(c) @ w_mod[l] + b_mod[l]
        mod_c = jax.nn.silu(c_ctx) @ w_mod[l] + b_mod[l]
        sh1, sc1, gt1, sh2, sc2, gt2 = jnp.split(mod[:, None, :], 6, axis=-1)
        sh1c, sc1c, gt1c, sh2c, sc2c, gt2c = jnp.split(mod_c, 6)
        h = modulate(rmsnorm(x, g_norm1[l]), sh1, sc1)
        hc = modulate(rmsnorm(xc, g_norm1[l]), sh1c, sc1c)
        y, yc = token_mixing(h, hc, w_in[l], w_branch[l], w_out[l], mlstm_gate_b[l], mlstm_norm_g[l],
                             rg_conv_w[l], rg_conv_b[l], rg_wa[l], rg_ba[l], rg_wx[l], rg_bx[l], rg_lam[l],
                             sc_conv_w[l], need_ctx)
        x = x + gt1 * y
        h = modulate(rmsnorm(x, g_norm2[l]), sh2, sc2)
        x = x + gt2 * peer_ffn(h, peer_wq[l], peer_keys[l], peer_u[l], peer_v[l])
        if need_ctx:
            xc = xc + gt1c * yc
            hc = modulate(rmsnorm(xc, g_norm2[l]), sh2c, sc2c)
            xc = xc + gt2c * peer_ffn(hc, peer_wq[l], peer_keys[l], peer_u[l], peer_v[l])
    return rmsnorm(x, g_final)
```

```python
import functools

import jax
import jax.numpy as jnp
from jax import lax
from jax.experimental import pallas as pl
from jax.experimental.pallas import tpu as pltpu

D_MODEL = 4096
BATCH = 2
SEQ = 4096
DEPTH = 2
CTX_LEN = 256
GRID_W = 64
N_BRANCHES = 4
BRANCH_W = D_MODEL // 4
FOURIER_GROUPS = 4
FOURIER_GW = BRANCH_W // FOURIER_GROUPS
MLSTM_HEADS = 4
MLSTM_DV = BRANCH_W // MLSTM_HEADS
MLSTM_DQK = MLSTM_DV // 2
MLSTM_CHUNK = 128
RG_BLOCKS = 8
RG_BW = BRANCH_W // RG_BLOCKS
RG_C = 8.0
RG_CONV_LEFT = 2
SC_CONV_LEFT = 1
PEER_HEADS = 8
PEER_NKEYS = 128
PEER_DK = 256
PEER_DKH = PEER_DK // 2
PEER_TOPK = 16
PEER_BLOCK = 64
EPS = 1e-6

COL_SIZES = (BRANCH_W, MLSTM_HEADS * MLSTM_DQK, MLSTM_HEADS * MLSTM_DQK, BRANCH_W, BRANCH_W,
             2 * 2 * MLSTM_HEADS, BRANCH_W, BRANCH_W, BRANCH_W, BRANCH_W, BRANCH_W, N_BRANCHES * D_MODEL)

VMEM_LIMIT_BYTES = 48 * 1024 * 1024


def _round_up(x, m):
    return (x + m - 1) // m * m


def _mm_kernel(a_ref, b_ref, o_ref, acc_ref):
    @pl.when(pl.program_id(2) == 0)
    def _():
        acc_ref[...] = jnp.zeros_like(acc_ref)

    acc_ref[...] += jnp.dot(a_ref[...], b_ref[...], preferred_element_type=jnp.float32)

    @pl.when(pl.program_id(2) == pl.num_programs(2) - 1)
    def _():
        o_ref[...] = acc_ref[...].astype(o_ref.dtype)


def mm(a, b, out_dtype=jnp.float32, tm=512, tn=1024, tk=1024):
    m, k = a.shape
    _, n = b.shape
    a = a.astype(jnp.bfloat16)
    b = b.astype(jnp.bfloat16)
    tm = min(tm, _round_up(m, 16))
    tn = min(tn, _round_up(n, 128))
    tk = min(tk, k)
    mp, np_ = _round_up(m, tm), _round_up(n, tn)
    if mp != m:
        a = jnp.pad(a, ((0, mp - m), (0, 0)))
    if np_ != n:
        b = jnp.pad(b, ((0, 0), (0, np_ - n)))
    out = pl.pallas_call(
        _mm_kernel,
        grid=(mp // tm, np_ // tn, k // tk),
        in_specs=[pl.BlockSpec((tm, tk), lambda i, j, kk: (i, kk)),
                  pl.BlockSpec((tk, tn), lambda i, j, kk: (kk, j))],
        out_specs=pl.BlockSpec((tm, tn), lambda i, j, kk: (i, j)),
        out_shape=jax.ShapeDtypeStruct((mp, np_), out_dtype),
        scratch_shapes=[pltpu.VMEM((tm, tn), jnp.float32)],
        compiler_params=pltpu.CompilerParams(
            dimension_semantics=("parallel", "parallel", "arbitrary"),
            vmem_limit_bytes=VMEM_LIMIT_BYTES),
        name="mm",
    )(a, b)
    return out[:m, :n]


def mm3(a, b, **kw):
    lead = a.shape[:-1]
    return mm(a.reshape(-1, a.shape[-1]), b, **kw).reshape(lead + (b.shape[-1],))


def rmsnorm(x, g):
    xf = x.astype(jnp.float32)
    y = xf * lax.rsqrt(jnp.mean(xf * xf, axis=-1, keepdims=True) + EPS)
    return (y * g.astype(jnp.float32)).astype(x.dtype)


def modulate(h, shift, scale):
    return h * (1 + scale) + shift


def split_cols(p):
    out, start = [], 0
    for size in COL_SIZES:
        out.append(p[..., start:start + size])
        start += size
    return out


def dwconv(u, w, pad_l):
    k_w, length = w.shape[0], u.shape[-2]
    pad = [(0, 0)] * (u.ndim - 2) + [(pad_l, k_w - 1 - pad_l), (0, 0)]
    up = jnp.pad(u, pad)
    y = w[0] * up[..., 0:length, :]
    for j in range(1, k_w):
        y = y + w[j] * up[..., j:j + length, :]
    return y


def conv_latent(u, w, pad_l):
    bsz, t, ch = u.shape
    rows = t // GRID_W
    y = dwconv(u.reshape(bsz, rows, GRID_W, ch), w, pad_l)
    return y.reshape(bsz, t, ch)


def fourier_mix(u):
    bsz, t, _ = u.shape
    g = u.astype(jnp.float32).reshape(bsz, t, FOURIER_GROUPS, FOURIER_GW)
    z = jnp.fft.fft2(g, axes=(1, 3), norm='ortho').real
    return z.reshape(bsz, t, BRANCH_W).astype(u.dtype)


def mlstm_scan(q, k, v, li, lf, state):
    bsz, nh, t, _ = q.shape
    nc = t // MLSTM_CHUNK

    def chunks(a):
        a = a.reshape(a.shape[:2] + (nc, MLSTM_CHUNK) + a.shape[3:])
        return jnp.moveaxis(a, 2, 0)

    tril = jnp.tril(jnp.ones((MLSTM_CHUNK, MLSTM_CHUNK), dtype=bool))

    def step(carry, inp):
        cmat, nvec, m = carry
        qc, kc, vc, ic, fc = inp
        b = jnp.cumsum(fc, axis=-1)
        logw = jnp.where(tril, b[..., :, None] - b[..., None, :] + ic[..., None, :], -jnp.inf)
        g = b + m[..., None]
        mt = jnp.maximum(g, jnp.max(logw, axis=-1))
        s = jnp.einsum('bhtd,bhsd->bhts', qc, kc) * jnp.exp(logw - mt[..., None])
        w_inter = jnp.exp(g - mt)
        num = jnp.einsum('bhts,bhsv->bhtv', s, vc) + w_inter[..., None] * jnp.einsum('bhvd,bhtd->bhtv', cmat, qc)
        den = jnp.sum(s, axis=-1) + w_inter * jnp.einsum('bhd,bhtd->bht', nvec, qc)
        h = num / jnp.maximum(jnp.abs(den), jnp.exp(-mt))[..., None]
        bl = b[..., -1]
        logu = bl[..., None] - b + ic
        m_new = jnp.maximum(bl + m, jnp.max(logu, axis=-1))
        ws = jnp.exp(logu - m_new[..., None])
        wc = jnp.exp(bl + m - m_new)
        c_new = wc[..., None, None] * cmat + jnp.einsum('bhs,bhsv,bhsd->bhvd', ws, vc, kc)
        n_new = wc[..., None] * nvec + jnp.einsum('bhs,bhsd->bhd', ws, kc)
        return (c_new, n_new, m_new), h

    state, hs = lax.scan(step, state, (chunks(q), chunks(k), chunks(v), chunks(li), chunks(lf)))
    hs = jnp.moveaxis(hs, 0, 2).reshape(bsz, nh, t, MLSTM_DV)
    return hs, state


def mlstm_heads(p_q, p_k, p_v, p_g, gate_b):
    bsz, t, _ = p_q.shape
    f32 = jnp.float32
    q = p_q.astype(f32).reshape(bsz, t, MLSTM_HEADS, MLSTM_DQK).transpose(0, 2, 1, 3) * (MLSTM_DQK ** -0.5)
    k = p_k.astype(f32).reshape(bsz, t, MLSTM_HEADS, MLSTM_DQK).transpose(0, 2, 1, 3)
    v = p_v.astype(f32).reshape(bsz, t, MLSTM_HEADS, MLSTM_DV).transpose(0, 2, 1, 3)
    g = p_g.astype(f32).reshape(bsz, t, 2, 2, MLSTM_HEADS) + gate_b.astype(f32)
    g = g.transpose(2, 3, 0, 4, 1)
    return q, k, v, g


def mlstm_bidir(lat, ctx, gate_b):
    ql, kl, vl, gl = mlstm_heads(*lat, gate_b)
    qc, kc, vc, gc = mlstm_heads(*ctx, gate_b)
    bsz = ql.shape[0]
    outs_l, outs_c = [], []
    for d in range(2):
        flip = (lambda a: jnp.flip(a, axis=2)) if d == 1 else (lambda a: a)
        init = (jnp.zeros((bsz, MLSTM_HEADS, MLSTM_DV, MLSTM_DQK), jnp.float32),
                jnp.zeros((bsz, MLSTM_HEADS, MLSTM_DQK), jnp.float32),
                jnp.zeros((bsz, MLSTM_HEADS), jnp.float32))
        hc, st = mlstm_scan(flip(qc), flip(kc), flip(vc), flip(gc[d, 0]), flip(jax.nn.log_sigmoid(gc[d, 1])), init)
        hl, _ = mlstm_scan(flip(ql), flip(kl), flip(vl), flip(gl[d, 0]), flip(jax.nn.log_sigmoid(gl[d, 1])), st)
        outs_l.append(flip(hl))
        outs_c.append(flip(hc))
    return outs_l[0] + outs_l[1], outs_c[0] + outs_c[1]


def mlstm_out(h, p_o, norm_g):
    h = h * lax.rsqrt(jnp.mean(h * h, axis=-1, keepdims=True) + EPS)
    bsz, nh, t, dv = h.shape
    h = h.transpose(0, 2, 1, 3).reshape(bsz, t, nh * dv) * norm_g.astype(jnp.float32)
    return (h * jax.nn.sigmoid(p_o.astype(jnp.float32))).astype(p_o.dtype)


def _lin_combine(e1, e2):
    a1, b1 = e1
    a2, b2 = e2
    return a1 * a2, a2 * b1 + b2


def rglru_scan(u, h0, wa, ba, wx, bx, lam):
    bsz, t, w = u.shape
    f32 = jnp.float32
    ub = u.reshape(bsz, t, RG_BLOCKS, RG_BW)
    r = jax.nn.sigmoid(jnp.einsum('btgi,gij->btgj', ub, wa.astype(f32)).reshape(bsz, t, w) + ba.astype(f32))
    i = jax.nn.sigmoid(jnp.einsum('btgi,gij->btgj', ub, wx.astype(f32)).reshape(bsz, t, w) + bx.astype(f32))
    log_a = -RG_C * jax.nn.softplus(-lam.astype(f32)) * r
    a = jnp.exp(log_a)
    b = jnp.sqrt(-jnp.expm1(2.0 * log_a)) * (i * u)
    a_cum, h_part = lax.associative_scan(_lin_combine, (a, b), axis=1)
    h = h_part + a_cum * h0[:, None, :]
    return h, h[:, -1]


def rglru_bidir(ul, uc, wa, ba, wx, bx, lam):
    bsz = ul.shape[0]
    outs_l, outs_c = [], []
    for d in range(2):
        flip = (lambda a: jnp.flip(a, axis=1)) if d == 1 else (lambda a: a)
        h0 = jnp.zeros((bsz, BRANCH_W), jnp.float32)
        hc, st = rglru_scan(flip(uc), h0, wa[d], ba[d], wx[d], bx[d], lam[d])
        hl, _ = rglru_scan(flip(ul), st, wa[d], ba[d], wx[d], bx[d], lam[d])
        outs_l.append(flip(hl))
        outs_c.append(flip(hc))
    return outs_l[0] + outs_l[1], outs_c[0] + outs_c[1]


def merge_branches(gate_cols, ys, w_branch, w_out):
    bsz, t, _ = gate_cols.shape
    gates = jax.nn.sigmoid(gate_cols.reshape(bsz, t, N_BRANCHES, D_MODEL))
    m = gates[:, :, 0] * mm3(ys[0], w_branch[0])
    for b in range(1, N_BRANCHES):
        m = m + gates[:, :, b] * mm3(ys[b], w_branch[b])
    return mm3(m, w_out)


def token_mixing(h_lat, h_ctx, w_in, w_branch, w_out, mlstm_gate_b, mlstm_norm_g, rg_conv_w, rg_conv_b,
                 rg_wa, rg_ba, rg_wx, rg_bx, rg_lam, sc_conv_w, need_ctx):
    f32 = jnp.float32
    w_in_b = w_in.astype(jnp.bfloat16)
    pl_ = split_cols(mm3(h_lat, w_in_b))
    pc = split_cols(mm3(h_ctx, w_in_b))
    ml_l, ml_c = mlstm_bidir((pl_[1], pl_[2], pl_[3], pl_[5]), (pc[1], pc[2], pc[3], pc[5]), mlstm_gate_b)
    ul = conv_latent(pl_[6].astype(f32), rg_conv_w.astype(f32), RG_CONV_LEFT) + rg_conv_b.astype(f32)
    uc = dwconv(pc[6].astype(f32), rg_conv_w.astype(f32), RG_CONV_LEFT) + rg_conv_b.astype(f32)
    rg_l, rg_c = rglru_bidir(ul, uc, rg_wa, rg_ba, rg_wx, rg_bx, rg_lam)
    wb_b = w_branch.astype(jnp.bfloat16)
    wo_b = w_out.astype(jnp.bfloat16)

    def stream_out(p, conv, mh, rh):
        y_four = fourier_mix(p[0])
        y_ml = mlstm_out(mh, p[4], mlstm_norm_g)
        y_rg = (jax.nn.gelu(p[7].astype(f32)) * rh).astype(p[7].dtype)
        y_sc = p[8] * conv(p[9] * p[10], sc_conv_w, SC_CONV_LEFT)
        return merge_branches(p[11], (y_four, y_ml, y_rg, y_sc), wb_b, wo_b)

    y_lat = stream_out(pl_, conv_latent, ml_l, rg_l)
    y_ctx = stream_out(pc, dwconv, ml_c, rg_c) if need_ctx else None
    return y_lat, y_ctx


def peer_ffn(h, w_q, keys, u_tab, v_tab):
    bsz, t, d = h.shape
    flat = h.reshape(bsz * t, d)
    n_tok = flat.shape[0]
    q = mm(flat, w_q).astype(jnp.float32).reshape(n_tok, PEER_HEADS, 2, PEER_DKH)
    s = jnp.einsum('nhpd,hpkd->nhpk', q, keys.astype(jnp.float32))
    sv, si = lax.top_k(s, PEER_TOPK)
    n_cand = PEER_TOPK * PEER_TOPK
    cand = (sv[:, :, 0, :, None] + sv[:, :, 1, None, :]).reshape(n_tok, PEER_HEADS, n_cand)
    cand_id = (si[:, :, 0, :, None] * PEER_NKEYS + si[:, :, 1, None, :]).reshape(n_tok, PEER_HEADS, n_cand)
    top_s, top_i = lax.top_k(cand, PEER_TOPK)
    ids = jnp.take_along_axis(cand_id, top_i, axis=-1)
    wts = jax.nn.softmax(top_s, axis=-1)
    nb = n_tok // PEER_BLOCK

    def block(args):
        hb, ib, wb = args
        ub = jnp.take(u_tab, ib, axis=0)
        act = jax.nn.gelu(jnp.einsum('pd,phkd->phk', hb, ub).astype(jnp.float32))
        vb = jnp.take(v_tab, ib, axis=0)
        return jnp.einsum('phk,phkd->pd', (wb * act).astype(vb.dtype), vb)

    out = lax.map(block, (flat.reshape(nb, PEER_BLOCK, d),
                          ids.reshape(nb, PEER_BLOCK, PEER_HEADS, PEER_TOPK),
                          wts.reshape(nb, PEER_BLOCK, PEER_HEADS, PEER_TOPK)))
    return out.reshape(bsz, t, d).astype(h.dtype)


def kernel(x, c, ctx, c_ctx, w_mod, b_mod, g_norm1, g_norm2, w_in, w_branch, w_out, mlstm_gate_b, mlstm_norm_g,
           rg_conv_w, rg_conv_b, rg_wa, rg_ba, rg_wx, rg_bx, rg_lam, sc_conv_w, peer_wq, peer_keys, peer_u,
           peer_v, g_final):
    xc = ctx
    for l in range(DEPTH):
        need_ctx = l < DEPTH - 1
        cc = jnp.concatenate([c, c_ctx[None, :]], axis=0)
        mod_all = mm(jax.nn.silu(cc), w_mod[l], tm=16, tn=2048, tk=2048) + b_mod[l]
        mod, mod_c = mod_all[:BATCH], mod_all[BATCH]
        sh1, sc1, gt1, sh2, sc2, gt2 = jnp.split(mod[:, None, :], 6, axis=-1)
        sh1c, sc1c, gt1c, sh2c, sc2c, gt2c = jnp.split(mod_c, 6)
        h = modulate(rmsnorm(x, g_norm1[l]), sh1, sc1)
        hc = modulate(rmsnorm(xc, g_norm1[l]), sh1c, sc1c)
        y, yc = token_mixing(h, hc, w_in[l], w_branch[l], w_out[l], mlstm_gate_b[l], mlstm_norm_g[l],
                             rg_conv_w[l], rg_conv_b[l], rg_wa[l], rg_ba[l], rg_wx[l], rg_bx[l], rg_lam[l],
                             sc_conv_w[l], need_ctx)
        x = x + gt1 * y
        h = modulate(rmsnorm(x, g_norm2[l]), sh2, sc2)
        x = x + gt2 * peer_ffn(h, peer_wq[l], peer_keys[l], peer_u[l], peer_v[l])
        if need_ctx:
            xc = xc + gt1c * yc
            hc = modulate(rmsnorm(xc, g_norm2[l]), sh2c, sc2c)
            xc = xc + gt2c * peer_ffn(hc, peer_wq[l], peer_keys[l], peer_u[l], peer_v[l])
    return rmsnorm(x, g_final)
```

```python
import functools

import jax
import jax.numpy as jnp
from jax import lax
from jax.experimental import pallas as pl
from jax.experimental.pallas import tpu as pltpu

D_MODEL = 4096
BATCH = 2
SEQ = 4096
DEPTH = 2
CTX_LEN = 256
GRID_W = 64
N_BRANCHES = 4
BRANCH_W = D_MODEL // 4
FOURIER_GROUPS = 4
FOURIER_GW = BRANCH_W // FOURIER_GROUPS
MLSTM_HEADS = 4
MLSTM_DV = BRANCH_W // MLSTM_HEADS
MLSTM_DQK = MLSTM_DV // 2
MLSTM_CHUNK = 128
RG_BLOCKS = 8
RG_BW = BRANCH_W // RG_BLOCKS
RG_C = 8.0
RG_CONV_LEFT = 2
SC_CONV_LEFT = 1
PEER_HEADS = 8
PEER_NKEYS = 128
PEER_DK = 256
PEER_DKH = PEER_DK // 2
PEER_TOPK = 16
PEER_BLOCK = 64
EPS = 1e-6

COL_SIZES = (BRANCH_W, MLSTM_HEADS * MLSTM_DQK, MLSTM_HEADS * MLSTM_DQK, BRANCH_W, BRANCH_W,
             2 * 2 * MLSTM_HEADS, BRANCH_W, BRANCH_W, BRANCH_W, BRANCH_W, BRANCH_W, N_BRANCHES * D_MODEL)

VMEM_LIMIT_BYTES = 48 * 1024 * 1024


def _round_up(x, m):
    return (x + m - 1) // m * m


def _mm_kernel(a_ref, b_ref, o_ref, acc_ref):
    @pl.when(pl.program_id(2) == 0)
    def _():
        acc_ref[...] = jnp.zeros_like(acc_ref)

    acc_ref[...] += jnp.dot(a_ref[...].astype(jnp.bfloat16), b_ref[...].astype(jnp.bfloat16),
                            preferred_element_type=jnp.float32)

    @pl.when(pl.program_id(2) == pl.num_programs(2) - 1)
    def _():
        o_ref[...] = acc_ref[...].astype(o_ref.dtype)


def _mm_resid_kernel(a_ref, b_ref, x_ref, g_ref, o_ref, acc_ref):
    @pl.when(pl.program_id(2) == 0)
    def _():
        acc_ref[...] = jnp.zeros_like(acc_ref)

    acc_ref[...] += jnp.dot(a_ref[...], b_ref[...], preferred_element_type=jnp.float32)

    @pl.when(pl.program_id(2) == pl.num_programs(2) - 1)
    def _():
        o_ref[...] = x_ref[...] + g_ref[0] * acc_ref[...]


def mm(a, b, out_dtype=jnp.float32, tm=512, tn=1024, tk=2048):
    m, k = a.shape
    _, n = b.shape
    tm = min(tm, _round_up(m, 16))
    tn = min(tn, _round_up(n, 128))
    tk = min(tk, k)
    mp, np_ = _round_up(m, tm), _round_up(n, tn)
    if mp != m:
        a = jnp.pad(a, ((0, mp - m), (0, 0)))
    if np_ != n:
        b = jnp.pad(b, ((0, 0), (0, np_ - n)))
    out = pl.pallas_call(
        _mm_kernel,
        grid=(mp // tm, np_ // tn, k // tk),
        in_specs=[pl.BlockSpec((tm, tk), lambda i, j, kk: (i, kk)),
                  pl.BlockSpec((tk, tn), lambda i, j, kk: (kk, j))],
        out_specs=pl.BlockSpec((tm, tn), lambda i, j, kk: (i, j)),
        out_shape=jax.ShapeDtypeStruct((mp, np_), out_dtype),
        scratch_shapes=[pltpu.VMEM((tm, tn), jnp.float32)],
        compiler_params=pltpu.CompilerParams(
            dimension_semantics=("parallel", "parallel", "arbitrary"),
            vmem_limit_bytes=VMEM_LIMIT_BYTES),
        name="mm",
    )(a, b)
    if mp != m or np_ != n:
        out = out[:m, :n]
    return out


def mm_resid(a, b, x, gates, rows_per_gate, tm=512, tn=1024, tk=2048):
    m, k = a.shape
    _, n = b.shape
    tm, tn, tk = min(tm, m), min(tn, n), min(tk, k)
    assert m % tm == 0 and n % tn == 0 and k % tk == 0 and rows_per_gate % tm == 0
    blocks_per_gate = rows_per_gate // tm
    return pl.pallas_call(
        _mm_resid_kernel,
        grid=(m // tm, n // tn, k // tk),
        in_specs=[pl.BlockSpec((tm, tk), lambda i, j, kk: (i, kk)),
                  pl.BlockSpec((tk, tn), lambda i, j, kk: (kk, j)),
                  pl.BlockSpec((tm, tn), lambda i, j, kk: (i, j)),
                  pl.BlockSpec((1, 1, tn), lambda i, j, kk: (i // blocks_per_gate, 0, j))],
        out_specs=pl.BlockSpec((tm, tn), lambda i, j, kk: (i, j)),
        out_shape=jax.ShapeDtypeStruct((m, n), jnp.float32),
        scratch_shapes=[pltpu.VMEM((tm, tn), jnp.float32)],
        compiler_params=pltpu.CompilerParams(
            dimension_semantics=("parallel", "parallel", "arbitrary"),
            vmem_limit_bytes=VMEM_LIMIT_BYTES),
        name="mm_resid",
    )(a, b, x, gates)


def mm3(a, b, **kw):
    lead = a.shape[:-1]
    return mm(a.reshape(-1, a.shape[-1]), b, **kw).reshape(lead + (b.shape[-1],))


def rmsnorm(x, g):
    xf = x.astype(jnp.float32)
    y = xf * lax.rsqrt(jnp.mean(xf * xf, axis=-1, keepdims=True) + EPS)
    return (y * g.astype(jnp.float32)).astype(x.dtype)


def modulate(h, shift, scale):
    return h * (1 + scale) + shift


def split_cols(p):
    out, start = [], 0
    for size in COL_SIZES:
        out.append(p[..., start:start + size])
        start += size
    return out


def dwconv(u, w, pad_l):
    k_w, length = w.shape[0], u.shape[-2]
    pad = [(0, 0)] * (u.ndim - 2) + [(pad_l, k_w - 1 - pad_l), (0, 0)]
    up = jnp.pad(u, pad)
    y = w[0] * up[..., 0:length, :]
    for j in range(1, k_w):
        y = y + w[j] * up[..., j:j + length, :]
    return y


def conv_latent(u, w, pad_l):
    bsz, t, ch = u.shape
    rows = t // GRID_W
    y = dwconv(u.reshape(bsz, rows, GRID_W, ch), w, pad_l)
    return y.reshape(bsz, t, ch)


def fourier_mix(u):
    bsz, t, _ = u.shape
    g = u.astype(jnp.float32).reshape(bsz, t, FOURIER_GROUPS, FOURIER_GW)
    z = jnp.fft.fft2(g, axes=(1, 3), norm='ortho').real
    return z.reshape(bsz, t, BRANCH_W).astype(u.dtype)


def mlstm_scan(q, k, v, li, lf, state):
    bsz, nh, t, _ = q.shape
    nc = t // MLSTM_CHUNK

    def chunks(a):
        a = a.reshape(a.shape[:2] + (nc, MLSTM_CHUNK) + a.shape[3:])
        return jnp.moveaxis(a, 2, 0)

    tril = jnp.tril(jnp.ones((MLSTM_CHUNK, MLSTM_CHUNK), dtype=bool))

    def step(carry, inp):
        cmat, nvec, m = carry
        qc, kc, vc, ic, fc = inp
        b = jnp.cumsum(fc, axis=-1)
        logw = jnp.where(tril, b[..., :, None] - b[..., None, :] + ic[..., None, :], -jnp.inf)
        g = b + m[..., None]
        mt = jnp.maximum(g, jnp.max(logw, axis=-1))
        s = jnp.einsum('bhtd,bhsd->bhts', qc, kc) * jnp.exp(logw - mt[..., None])
        w_inter = jnp.exp(g - mt)
        num = jnp.einsum('bhts,bhsv->bhtv', s, vc) + w_inter[..., None] * jnp.einsum('bhvd,bhtd->bhtv', cmat, qc)
        den = jnp.sum(s, axis=-1) + w_inter * jnp.einsum('bhd,bhtd->bht', nvec, qc)
        h = num / jnp.maximum(jnp.abs(den), jnp.exp(-mt))[..., None]
        bl = b[..., -1]
        logu = bl[..., None] - b + ic
        m_new = jnp.maximum(bl + m, jnp.max(logu, axis=-1))
        ws = jnp.exp(logu - m_new[..., None])
        wc = jnp.exp(bl + m - m_new)
        c_new = wc[..., None, None] * cmat + jnp.einsum('bhs,bhsv,bhsd->bhvd', ws, vc, kc)
        n_new = wc[..., None] * nvec + jnp.einsum('bhs,bhsd->bhd', ws, kc)
        return (c_new, n_new, m_new), h

    state, hs = lax.scan(step, state, (chunks(q), chunks(k), chunks(v), chunks(li), chunks(lf)))
    hs = jnp.moveaxis(hs, 0, 2).reshape(bsz, nh, t, MLSTM_DV)
    return hs, state


def mlstm_heads(p_q, p_k, p_v, p_g, gate_b):
    bsz, t, _ = p_q.shape
    f32 = jnp.float32
    q = p_q.astype(f32).reshape(bsz, t, MLSTM_HEADS, MLSTM_DQK).transpose(0, 2, 1, 3) * (MLSTM_DQK ** -0.5)
    k = p_k.astype(f32).reshape(bsz, t, MLSTM_HEADS, MLSTM_DQK).transpose(0, 2, 1, 3)
    v = p_v.astype(f32).reshape(bsz, t, MLSTM_HEADS, MLSTM_DV).transpose(0, 2, 1, 3)
    g = p_g.astype(f32).reshape(bsz, t, 2, 2, MLSTM_HEADS) + gate_b.astype(f32)
    g = g.transpose(2, 3, 0, 4, 1)
    return q, k, v, g


def mlstm_bidir(lat, ctx, gate_b):
    ql, kl, vl, gl = mlstm_heads(*lat, gate_b)
    qc, kc, vc, gc = mlstm_heads(*ctx, gate_b)
    bsz = ql.shape[0]
    outs_l, outs_c = [], []
    for d in range(2):
        flip = (lambda a: jnp.flip(a, axis=2)) if d == 1 else (lambda a: a)
        init = (jnp.zeros((bsz, MLSTM_HEADS, MLSTM_DV, MLSTM_DQK), jnp.float32),
                jnp.zeros((bsz, MLSTM_HEADS, MLSTM_DQK), jnp.float32),
                jnp.zeros((bsz, MLSTM_HEADS), jnp.float32))
        hc, st = mlstm_scan(flip(qc), flip(kc), flip(vc), flip(gc[d, 0]), flip(jax.nn.log_sigmoid(gc[d, 1])), init)
        hl, _ = mlstm_scan(flip(ql), flip(kl), flip(vl), flip(gl[d, 0]), flip(jax.nn.log_sigmoid(gl[d, 1])), st)
        outs_l.append(flip(hl))
        outs_c.append(flip(hc))
    return outs_l[0] + outs_l[1], outs_c[0] + outs_c[1]


def mlstm_out(h, p_o, norm_g):
    h = h * lax.rsqrt(jnp.mean(h * h, axis=-1, keepdims=True) + EPS)
    bsz, nh, t, dv = h.shape
    h = h.transpose(0, 2, 1, 3).reshape(bsz, t, nh * dv) * norm_g.astype(jnp.float32)
    return (h * jax.nn.sigmoid(p_o.astype(jnp.float32))).astype(p_o.dtype)


SEQ_BLOCK = CTX_LEN
LAT_BLOCKS = SEQ // SEQ_BLOCK
SUBLANES = 8


def _masked_conv(u, w_ref, pad_l, is_ctx):
    rows = u.shape[0]
    t = lax.broadcasted_iota(jnp.int32, (rows, 1), 0)
    seg = jnp.where(is_ctx, rows, GRID_W)
    pos = t & (seg - 1)
    y = None
    for j in range(w_ref.shape[0]):
        k = j - pad_l
        if k == 0:
            sh = u
        else:
            sh = pltpu.roll(u, (-k) % rows, axis=0)
            sh = jnp.where((pos + k >= 0) & (pos + k < seg), sh, 0.0)
        term = w_ref[j:j + 1, :] * sh
        y = term if y is None else y + term
    return y


def _rglru_kernel(*refs, reverse):
    if reverse:
        (p6_ref, cw_ref, cb_ref, wa_ref, ba_ref, wx_ref, bx_ref, lam_ref, hf_ref, p7_ref,
         o_ref, a_sc, b_sc, h_sc) = refs
    else:
        p6_ref, cw_ref, cb_ref, wa_ref, ba_ref, wx_ref, bx_ref, lam_ref, o_ref, a_sc, b_sc, h_sc = refs
    s = pl.program_id(1)

    @pl.when(s == 0)
    def _():
        h_sc[...] = jnp.zeros_like(h_sc)

    u = _masked_conv(p6_ref[...], cw_ref, RG_CONV_LEFT, s == 0) + cb_ref[...]
    ub = u.astype(jnp.bfloat16)
    for g in range(RG_BLOCKS):
        cols = slice(g * RG_BW, (g + 1) * RG_BW)
        ug = ub[:, cols]
        r = jax.nn.sigmoid(jnp.dot(ug, wa_ref[0, g].astype(jnp.bfloat16), preferred_element_type=jnp.float32)
                           + ba_ref[:, cols])
        i = jax.nn.sigmoid(jnp.dot(ug, wx_ref[0, g].astype(jnp.bfloat16), preferred_element_type=jnp.float32)
                           + bx_ref[:, cols])
        neg_lam = -lam_ref[:, cols]
        softplus = jnp.log1p(jnp.exp(-jnp.abs(neg_lam))) + jnp.maximum(neg_lam, 0.0)
        log_a = (-RG_C * softplus) * r
        a_sc[:, cols] = jnp.exp(log_a)
        b_sc[:, cols] = jnp.sqrt(1.0 - jnp.exp(2.0 * log_a)) * (i * u[:, cols])

    n_groups = a_sc.shape[0] // SUBLANES
    row = lax.broadcasted_iota(jnp.int32, (SUBLANES, a_sc.shape[1]), 0)

    def body(it, h_prev):
        grp = (n_groups - 1 - it) if reverse else it
        off = pl.multiple_of(grp * SUBLANES, SUBLANES)
        a = a_sc[pl.ds(off, SUBLANES), :]
        b = b_sc[pl.ds(off, SUBLANES), :]
        for k in (1, 2, 4):
            shift = (SUBLANES - k) if reverse else k
            inside = (row < SUBLANES - k) if reverse else (row >= k)
            a_s = jnp.where(inside, pltpu.roll(a, shift, axis=0), 1.0)
            b_s = jnp.where(inside, pltpu.roll(b, shift, axis=0), 0.0)
            b = a * b_s + b
            a = a * a_s
        h = b + a * h_prev
        b_sc[pl.ds(off, SUBLANES), :] = h
        return h[0:1] if reverse else h[SUBLANES - 1:SUBLANES]

    h_sc[...] = lax.fori_loop(0, n_groups, body, h_sc[...])
    if reverse:
        o_ref[...] = (jax.nn.gelu(p7_ref[...]) * (hf_ref[...] + b_sc[...])).astype(o_ref.dtype)
    else:
        o_ref[...] = b_sc[...]


def _seq_block(b, s, reverse, n_batch):
    lat = (LAT_BLOCKS - s) if reverse else (s - 1)
    return jnp.where(s == 0, n_batch * LAT_BLOCKS + b, b * LAT_BLOCKS + lat)


def rglru_mixer(p6, p7, conv_w, conv_b, wa, ba, wx, bx, lam, n_batch):
    rows, w = p6.shape
    row2 = lambda a: a.reshape(1, w)
    outs = None
    for reverse in (False, True):
        d = int(reverse)
        blk = lambda b, s: (_seq_block(b, s, reverse, n_batch), 0)
        const2 = lambda b, s: (0, 0)
        in_specs = [pl.BlockSpec((SEQ_BLOCK, w), blk),
                    pl.BlockSpec(conv_w.shape, const2),
                    pl.BlockSpec((1, w), const2),
                    pl.BlockSpec((1,) + wa.shape[1:], lambda b, s: (d, 0, 0, 0)),
                    pl.BlockSpec((1, w), const2),
                    pl.BlockSpec((1,) + wx.shape[1:], lambda b, s: (d, 0, 0, 0)),
                    pl.BlockSpec((1, w), const2),
                    pl.BlockSpec((1, w), const2)]
        args = [p6, conv_w, row2(conv_b), wa, row2(ba[d]), wx, row2(bx[d]), row2(lam[d])]
        if reverse:
            in_specs += [pl.BlockSpec((SEQ_BLOCK, w), blk), pl.BlockSpec((SEQ_BLOCK, w), blk)]
            args += [outs, p7]
        outs = pl.pallas_call(
            functools.partial(_rglru_kernel, reverse=reverse),
            grid=(n_batch, LAT_BLOCKS + 1),
            in_specs=in_specs,
            out_specs=pl.BlockSpec((SEQ_BLOCK, w), blk),
            out_shape=jax.ShapeDtypeStruct((rows, w), jnp.bfloat16 if reverse else jnp.float32),
            scratch_shapes=[pltpu.VMEM((SEQ_BLOCK, w), jnp.float32), pltpu.VMEM((SEQ_BLOCK, w), jnp.float32),
                            pltpu.VMEM((1, w), jnp.float32)],
            compiler_params=pltpu.CompilerParams(dimension_semantics=("parallel", "arbitrary"),
                                                 vmem_limit_bytes=VMEM_LIMIT_BYTES),
            name="rglru_bwd" if reverse else "rglru_fwd",
        )(*args)
    return outs


def merge_branches(gate_cols, ys, w_branch, w_out):
    bsz, t, _ = gate_cols.shape
    gates = jax.nn.sigmoid(gate_cols.reshape(bsz, t, N_BRANCHES, D_MODEL))
    m = gates[:, :, 0] * mm3(ys[0], w_branch[0])
    for b in range(1, N_BRANCHES):
        m = m + gates[:, :, b] * mm3(ys[b], w_branch[b])
    return mm3(m, w_out)


def token_mixing(h_lat, h_ctx, w_in, w_branch, w_out, mlstm_gate_b, mlstm_norm_g, rg_conv_w, rg_conv_b,
                 rg_wa, rg_ba, rg_wx, rg_bx, rg_lam, sc_conv_w, need_ctx):
    f32 = jnp.float32
    w_in_b = w_in.astype(jnp.bfloat16)
    pl_ = split_cols(mm3(h_lat, w_in_b))
    pc = split_cols(mm3(h_ctx, w_in_b))
    ml_l, ml_c = mlstm_bidir((pl_[1], pl_[2], pl_[3], pl_[5]), (pc[1], pc[2], pc[3], pc[5]), mlstm_gate_b)
    bsz = h_lat.shape[0]
    seq_rows = lambda a_l, a_c: jnp.concatenate([a_l.reshape(-1, BRANCH_W), a_c.reshape(-1, BRANCH_W)], axis=0)
    y_rg_all = rglru_mixer(seq_rows(pl_[6], pc[6]), seq_rows(pl_[7], pc[7]), rg_conv_w, rg_conv_b,
                           rg_wa, rg_ba, rg_wx, rg_bx, rg_lam, bsz)
    y_rg_l = y_rg_all[:bsz * SEQ].reshape(bsz, SEQ, BRANCH_W)
    y_rg_c = y_rg_all[bsz * SEQ:].reshape(bsz, CTX_LEN, BRANCH_W)
    wb_b = w_branch.astype(jnp.bfloat16)
    wo_b = w_out.astype(jnp.bfloat16)

    def stream_out(p, conv, mh, y_rg):
        y_four = fourier_mix(p[0])
        y_ml = mlstm_out(mh, p[4], mlstm_norm_g)
        y_sc = p[8] * conv(p[9] * p[10], sc_conv_w, SC_CONV_LEFT)
        return merge_branches(p[11], (y_four, y_ml, y_rg, y_sc), wb_b, wo_b)

    y_lat = stream_out(pl_, conv_latent, ml_l, y_rg_l)
    y_ctx = stream_out(pc, dwconv, ml_c, y_rg_c) if need_ctx else None
    return y_lat, y_ctx


def peer_ffn(h, w_q, keys, ut_tab, v_tab, x_resid, gates, rows_per_gate):
    bsz, t, d = h.shape
    flat = h.reshape(bsz * t, d)
    n_tok = flat.shape[0]
    q = mm(flat, w_q).astype(jnp.float32).reshape(n_tok, PEER_HEADS, 2, PEER_DKH)
    s = jnp.einsum('nhpd,hpkd->nhpk', q, keys.astype(jnp.float32))
    sv, si = lax.top_k(s, PEER_TOPK)
    n_cand = PEER_TOPK * PEER_TOPK
    cand = (sv[:, :, 0, :, None] + sv[:, :, 1, None, :]).reshape(n_tok, PEER_HEADS, n_cand)
    cand_id = (si[:, :, 0, :, None] * PEER_NKEYS + si[:, :, 1, None, :]).reshape(n_tok, PEER_HEADS, n_cand)
    top_s, top_i = lax.top_k(cand, PEER_TOPK)
    ids = jnp.take_along_axis(cand_id, top_i, axis=-1).reshape(n_tok, PEER_HEADS * PEER_TOPK)
    wts = jax.nn.softmax(top_s, axis=-1).reshape(n_tok, PEER_HEADS * PEER_TOPK)
    i1, i2 = ids // PEER_NKEYS, ids % PEER_NKEYS
    sc = peer_scores(flat.astype(jnp.bfloat16), ut_tab, i1, i2)
    coef = peer_coef(sc, wts, i1, i2).reshape(n_tok, PEER_NKEYS * PEER_NKEYS)
    return mm_resid(coef, v_tab, x_resid.reshape(n_tok, d), gates, rows_per_gate).reshape(bsz, t, d)


def _peer_score_kernel(h_ref, ut_ref, i1_ref, i2_ref, o_ref):
    j = pl.program_id(1)

    @pl.when(j == 0)
    def _():
        o_ref[...] = jnp.zeros_like(o_ref)

    s = jnp.dot(h_ref[...], ut_ref[...], preferred_element_type=jnp.float32)
    i1, i2 = i1_ref[...], i2_ref[...]
    acc = o_ref[...]
    n_chunks = s.shape[1] // PEER_NKEYS
    for c in range(n_chunks):
        picked = jnp.take_along_axis(s[:, c * PEER_NKEYS:(c + 1) * PEER_NKEYS], i2, axis=1)
        acc = jnp.where(i1 == j * n_chunks + c, picked, acc)
    o_ref[...] = acc


def peer_scores(h, ut_tab, i1, i2, tm=512, tn=1024):
    n_tok, d = h.shape
    n_exp = ut_tab.shape[1]
    n_ent = i1.shape[1]
    tm = min(tm, n_tok)
    assert n_tok % tm == 0 and n_exp % tn == 0 and n_ent == PEER_NKEYS
    ent_spec = pl.BlockSpec((tm, n_ent), lambda i, j: (i, 0))
    return pl.pallas_call(
        _peer_score_kernel,
        grid=(n_tok // tm, n_exp // tn),
        in_specs=[pl.BlockSpec((tm, d), lambda i, j: (i, 0)),
                  pl.BlockSpec((d, tn), lambda i, j: (0, j)),
                  ent_spec, ent_spec],
        out_specs=ent_spec,
        out_shape=jax.ShapeDtypeStruct((n_tok, n_ent), jnp.float32),
        compiler_params=pltpu.CompilerParams(dimension_semantics=("parallel", "arbitrary"),
                                             vmem_limit_bytes=VMEM_LIMIT_BYTES),
        name="peer_scores",
    )(h, ut_tab, i1, i2)


def _peer_coef_kernel(sc_ref, w_ref, i1_ref, i2_ref, o_ref, wa_sc):
    wa_sc[...] = w_ref[...] * jax.nn.gelu(sc_ref[...])
    n_keys = o_ref.shape[1]
    n_ent = sc_ref.shape[1]
    key = lax.broadcasted_iota(jnp.int32, (n_keys, n_ent), 0)

    def body(t, carry):
        row = lambda ref: jnp.broadcast_to(ref[pl.ds(t, 1), :], (n_keys, n_ent))
        at = jnp.where(key == row(i1_ref), row(wa_sc), 0.0).astype(jnp.bfloat16)
        bt = jnp.where(key == row(i2_ref), 1.0, 0.0).astype(jnp.bfloat16)
        ct = lax.dot_general(at, bt, (((1,), (1,)), ((), ())), preferred_element_type=jnp.float32)
        o_ref[t] = ct.astype(o_ref.dtype)
        return carry

    lax.fori_loop(0, sc_ref.shape[0], body, 0, unroll=4)


def peer_coef(sc, wts, i1, i2, tb=128):
    n_tok, n_ent = sc.shape
    tb = min(tb, n_tok)
    assert n_tok % tb == 0
    ent_spec = pl.BlockSpec((tb, n_ent), lambda i: (i, 0))
    return pl.pallas_call(
        _peer_coef_kernel,
        grid=(n_tok // tb,),
        in_specs=[ent_spec] * 4,
        out_specs=pl.BlockSpec((tb, PEER_NKEYS, PEER_NKEYS), lambda i: (i, 0, 0)),
        out_shape=jax.ShapeDtypeStruct((n_tok, PEER_NKEYS, PEER_NKEYS), jnp.bfloat16),
        scratch_shapes=[pltpu.VMEM((tb, n_ent), jnp.float32)],
        compiler_params=pltpu.CompilerParams(dimension_semantics=("parallel",),
                                             vmem_limit_bytes=VMEM_LIMIT_BYTES),
        name="peer_coef",
    )(sc, wts, i1, i2)


def kernel(x, c, ctx, c_ctx, w_mod, b_mod, g_norm1, g_norm2, w_in, w_branch, w_out, mlstm_gate_b, mlstm_norm_g,
           rg_conv_w, rg_conv_b, rg_wa, rg_ba, rg_wx, rg_bx, rg_lam, sc_conv_w, peer_wq, peer_keys, peer_u,
           peer_v, g_final):
    xc = ctx
    for l in range(DEPTH):
        need_ctx = l < DEPTH - 1
        cc = jnp.concatenate([c, c_ctx[None, :]], axis=0)
        mod_all = mm(jax.nn.silu(cc), w_mod[l], tm=16, tn=2048, tk=2048) + b_mod[l]
        mod, mod_c = mod_all[:BATCH], mod_all[BATCH]
        sh1, sc1, gt1, sh2, sc2, gt2 = jnp.split(mod[:, None, :], 6, axis=-1)
        sh1c, sc1c, gt1c, sh2c, sc2c, gt2c = jnp.split(mod_c, 6)
        h = modulate(rmsnorm(x, g_norm1[l]), sh1, sc1)
        hc = modulate(rmsnorm(xc, g_norm1[l]), sh1c, sc1c)
        y, yc = token_mixing(h, hc, w_in[l], w_branch[l], w_out[l], mlstm_gate_b[l], mlstm_norm_g[l],
                             rg_conv_w[l], rg_conv_b[l], rg_wa[l], rg_ba[l], rg_wx[l], rg_bx[l], rg_lam[l],
                             sc_conv_w[l], need_ctx)
        x = x + gt1 * y
        h = modulate(rmsnorm(x, g_norm2[l]), sh2, sc2)
        wq_b = peer_wq[l].astype(jnp.bfloat16)
        ut_tab = peer_u[l].T.astype(jnp.bfloat16)
        v_tab = peer_v[l].astype(jnp.bfloat16)
        x = peer_ffn(h, wq_b, peer_keys[l], ut_tab, v_tab, x, gt2.reshape(BATCH, 1, D_MODEL), SEQ)
        if need_ctx:
            xc = xc + gt1c * yc
            hc = modulate(rmsnorm(xc, g_norm2[l]), sh2c, sc2c)
            xc = peer_ffn(hc, wq_b, peer_keys[l], ut_tab, v_tab, xc, gt2c.reshape(1, 1, D_MODEL), BATCH * CTX_LEN)
    return rmsnorm(x, g_final)
```

```python
import functools

import jax
import jax.numpy as jnp
from jax import lax
from jax.experimental import pallas as pl
from jax.experimental.pallas import tpu as pltpu

D_MODEL = 4096
BATCH = 2
SEQ = 4096
DEPTH = 2
CTX_LEN = 256
GRID_W = 64
N_BRANCHES = 4
BRANCH_W = D_MODEL // 4
FOURIER_GROUPS = 4
FOURIER_GW = BRANCH_W // FOURIER_GROUPS
MLSTM_HEADS = 4
MLSTM_DV = BRANCH_W // MLSTM_HEADS
MLSTM_DQK = MLSTM_DV // 2
MLSTM_CHUNK = 128
RG_BLOCKS = 8
RG_BW = BRANCH_W // RG_BLOCKS
RG_C = 8.0
RG_CONV_LEFT = 2
SC_CONV_LEFT = 1
PEER_HEADS = 8
PEER_NKEYS = 128
PEER_DK = 256
PEER_DKH = PEER_DK // 2
PEER_TOPK = 16
TOPK_SHIFT = PEER_TOPK.bit_length() - 1
EPS = 1e-6

N_MLSTM_GATES = 2 * 2 * MLSTM_HEADS
COL_MAIN = 4 * BRANCH_W
COL_GATES = COL_MAIN + N_MLSTM_GATES
COL_MERGE = COL_GATES + 5 * BRANCH_W

LAT_ROWS = BATCH * SEQ
ROWS = LAT_ROWS + BATCH * CTX_LEN
SEQ_BLOCK = CTX_LEN
LAT_BLOCKS = SEQ // SEQ_BLOCK
ROW_BLOCK = 512
SUBLANES = 8
LANES = 128
VMEM_LIMIT_BYTES = 48 * 1024 * 1024

MOD_SHIFT1, MOD_SCALE1, MOD_GATE1, MOD_SHIFT2, MOD_SCALE2, MOD_GATE2 = range(6)


def _round_up(x, m):
    return (x + m - 1) // m * m


def _params(*semantics):
    return pltpu.CompilerParams(dimension_semantics=semantics, vmem_limit_bytes=VMEM_LIMIT_BYTES)


def _segment(row_block, rows_per_block):
    return jnp.minimum(row_block // (SEQ // rows_per_block), BATCH)


def _mm_kernel(a_ref, b_ref, o_ref, acc_ref):
    @pl.when(pl.program_id(2) == 0)
    def _():
        acc_ref[...] = jnp.zeros_like(acc_ref)

    acc_ref[...] += jnp.dot(a_ref[...].astype(jnp.bfloat16), b_ref[...].astype(jnp.bfloat16),
                            preferred_element_type=jnp.float32)

    @pl.when(pl.program_id(2) == pl.num_programs(2) - 1)
    def _():
        o_ref[...] = acc_ref[...].astype(o_ref.dtype)


def _mm_fullk_kernel(a_ref, b_ref, o_ref):
    o_ref[...] = jnp.dot(a_ref[...].astype(jnp.bfloat16), b_ref[...].astype(jnp.bfloat16),
                         preferred_element_type=jnp.float32).astype(o_ref.dtype)


def mm(a, b, out_dtype=jnp.float32, tm=ROW_BLOCK, tn=1024, tk=2048):
    m, k = a.shape
    _, n = b.shape
    tm = min(tm, _round_up(m, 16))
    tn = min(tn, _round_up(n, LANES))
    tk = min(tk, k)
    mp, np_ = _round_up(m, tm), _round_up(n, tn)
    if mp != m:
        a = jnp.pad(a, ((0, mp - m), (0, 0)))
    if np_ != n:
        b = jnp.pad(b, ((0, 0), (0, np_ - n)))
    if tk == k:
        out = pl.pallas_call(
            _mm_fullk_kernel,
            grid=(mp // tm, np_ // tn),
            in_specs=[pl.BlockSpec((tm, k), lambda i, j: (i, 0)),
                      pl.BlockSpec((k, tn), lambda i, j: (0, j))],
            out_specs=pl.BlockSpec((tm, tn), lambda i, j: (i, j)),
            out_shape=jax.ShapeDtypeStruct((mp, np_), out_dtype),
            compiler_params=_params("parallel", "parallel"),
            name="mm_fullk",
        )(a, b)
    else:
        out = pl.pallas_call(
            _mm_kernel,
            grid=(mp // tm, np_ // tn, k // tk),
            in_specs=[pl.BlockSpec((tm, tk), lambda i, j, kk: (i, kk)),
                      pl.BlockSpec((tk, tn), lambda i, j, kk: (kk, j))],
            out_specs=pl.BlockSpec((tm, tn), lambda i, j, kk: (i, j)),
            out_shape=jax.ShapeDtypeStruct((mp, np_), out_dtype),
            scratch_shapes=[pltpu.VMEM((tm, tn), jnp.float32)],
            compiler_params=_params("parallel", "parallel", "arbitrary"),
            name="mm",
        )(a, b)
    if mp != m or np_ != n:
        out = out[:m, :n]
    return out


def _mm_resid_kernel(a_ref, b_ref, x_ref, g_ref, o_ref, acc_ref):
    @pl.when(pl.program_id(2) == 0)
    def _():
        acc_ref[...] = jnp.zeros_like(acc_ref)

    acc_ref[...] += jnp.dot(a_ref[...], b_ref[...], preferred_element_type=jnp.float32)

    @pl.when(pl.program_id(2) == pl.num_programs(2) - 1)
    def _():
        o_ref[...] = x_ref[...] + g_ref[0, 0] * acc_ref[...]


def mm_resid(a, b, x, modtab, which, tm=ROW_BLOCK, tn=1024, tk=2048):
    m, k = a.shape
    _, n = b.shape
    tn, tk = min(tn, n), min(tk, k)
    assert m % tm == 0 and n % tn == 0 and k % tk == 0
    return pl.pallas_call(
        _mm_resid_kernel,
        grid=(m // tm, n // tn, k // tk),
        in_specs=[pl.BlockSpec((tm, tk), lambda i, j, kk: (i, kk)),
                  pl.BlockSpec((tk, tn), lambda i, j, kk: (kk, j)),
                  pl.BlockSpec((tm, tn), lambda i, j, kk: (i, j)),
                  pl.BlockSpec((1, 1, 1, tn), lambda i, j, kk: (_segment(i, tm), which, 0, j))],
        out_specs=pl.BlockSpec((tm, tn), lambda i, j, kk: (i, j)),
        out_shape=jax.ShapeDtypeStruct((m, n), jnp.float32),
        scratch_shapes=[pltpu.VMEM((tm, tn), jnp.float32)],
        compiler_params=_params("parallel", "parallel", "arbitrary"),
        name="mm_resid",
    )(a, b, x, modtab)


def _norm_mod_kernel(x_ref, g_ref, sh_ref, sc_ref, o_ref):
    x = x_ref[...]
    y = x * lax.rsqrt(jnp.mean(x * x, axis=-1, keepdims=True) + EPS)
    o_ref[...] = ((y * g_ref[...]) * (1.0 + sc_ref[0, 0]) + sh_ref[0, 0]).astype(o_ref.dtype)


def norm_mod(x, g, modtab, which_shift, which_scale, tm=SEQ_BLOCK):
    m, d = x.shape
    mod_spec = lambda which: pl.BlockSpec((1, 1, 1, d), lambda i: (_segment(i, tm), which, 0, 0))
    return pl.pallas_call(
        _norm_mod_kernel,
        grid=(m // tm,),
        in_specs=[pl.BlockSpec((tm, d), lambda i: (i, 0)),
                  pl.BlockSpec((1, d), lambda i: (0, 0)),
                  mod_spec(which_shift), mod_spec(which_scale)],
        out_specs=pl.BlockSpec((tm, d), lambda i: (i, 0)),
        out_shape=jax.ShapeDtypeStruct((m, d), jnp.bfloat16),
        compiler_params=_params("parallel"),
        name="norm_mod",
    )(x, g.reshape(1, d), modtab, modtab)


def _rmsnorm_kernel(x_ref, g_ref, o_ref):
    x = x_ref[...]
    o_ref[...] = x * lax.rsqrt(jnp.mean(x * x, axis=-1, keepdims=True) + EPS) * g_ref[...]


def rmsnorm_rows(x, g, tm=SEQ_BLOCK):
    m, d = x.shape
    return pl.pallas_call(
        _rmsnorm_kernel,
        grid=(m // tm,),
        in_specs=[pl.BlockSpec((tm, d), lambda i: (i, 0)), pl.BlockSpec((1, d), lambda i: (0, 0))],
        out_specs=pl.BlockSpec((tm, d), lambda i: (i, 0)),
        out_shape=jax.ShapeDtypeStruct((m, d), jnp.float32),
        compiler_params=_params("parallel"),
        name="rmsnorm",
    )(x, g.reshape(1, d))


def _masked_conv(u, w_ref, pad_l, is_ctx):
    rows = u.shape[0]
    t = lax.broadcasted_iota(jnp.int32, (rows, 1), 0)
    seg = jnp.where(is_ctx, rows, GRID_W)
    pos = t & (seg - 1)
    y = None
    for j in range(w_ref.shape[0]):
        k = j - pad_l
        if k == 0:
            sh = u
        else:
            sh = pltpu.roll(u, (-k) % rows, axis=0)
            sh = jnp.where((pos + k >= 0) & (pos + k < seg), sh, 0.0)
        term = w_ref[j:j + 1, :] * sh
        y = term if y is None else y + term
    return y


def _rglru_kernel(*refs, reverse):
    if reverse:
        (p6_ref, cw_ref, cb_ref, wa_ref, ba_ref, wx_ref, bx_ref, lam_ref, hf_ref, p7_ref,
         o_ref, a_sc, b_sc, h_sc) = refs
    else:
        p6_ref, cw_ref, cb_ref, wa_ref, ba_ref, wx_ref, bx_ref, lam_ref, o_ref, a_sc, b_sc, h_sc = refs
    s = pl.program_id(1)

    @pl.when(s == 0)
    def _():
        h_sc[...] = jnp.zeros_like(h_sc)

    u = _masked_conv(p6_ref[...], cw_ref, RG_CONV_LEFT, s == 0) + cb_ref[...]
    ub = u.astype(jnp.bfloat16)
    for g in range(RG_BLOCKS):
        cols = slice(g * RG_BW, (g + 1) * RG_BW)
        ug = ub[:, cols]
        r = jax.nn.sigmoid(jnp.dot(ug, wa_ref[0, g].astype(jnp.bfloat16), preferred_element_type=jnp.float32)
                           + ba_ref[:, cols])
        i = jax.nn.sigmoid(jnp.dot(ug, wx_ref[0, g].astype(jnp.bfloat16), preferred_element_type=jnp.float32)
                           + bx_ref[:, cols])
        neg_lam = -lam_ref[:, cols]
        softplus = jnp.log1p(jnp.exp(-jnp.abs(neg_lam))) + jnp.maximum(neg_lam, 0.0)
        log_a = (-RG_C * softplus) * r
        a_sc[:, cols] = jnp.exp(log_a)
        b_sc[:, cols] = jnp.sqrt(1.0 - jnp.exp(2.0 * log_a)) * (i * u[:, cols])

    n_groups = a_sc.shape[0] // SUBLANES
    row = lax.broadcasted_iota(jnp.int32, (SUBLANES, a_sc.shape[1]), 0)

    def body(it, h_prev):
        grp = (n_groups - 1 - it) if reverse else it
        off = pl.multiple_of(grp * SUBLANES, SUBLANES)
        a = a_sc[pl.ds(off, SUBLANES), :]
        b = b_sc[pl.ds(off, SUBLANES), :]
        for k in (1, 2, 4):
            shift = (SUBLANES - k) if reverse else k
            inside = (row < SUBLANES - k) if reverse else (row >= k)
            a_s = jnp.where(inside, pltpu.roll(a, shift, axis=0), 1.0)
            b_s = jnp.where(inside, pltpu.roll(b, shift, axis=0), 0.0)
            b = a * b_s + b
            a = a * a_s
        h = b + a * h_prev
        b_sc[pl.ds(off, SUBLANES), :] = h
        return h[0:1] if reverse else h[SUBLANES - 1:SUBLANES]

    h_sc[...] = lax.fori_loop(0, n_groups, body, h_sc[...])
    if reverse:
        o_ref[...] = (jax.nn.gelu(p7_ref[...]) * (hf_ref[...] + b_sc[...])).astype(o_ref.dtype)
    else:
        o_ref[...] = b_sc[...]


def _seq_block(b, s, reverse):
    lat = (LAT_BLOCKS - s) if reverse else (s - 1)
    return jnp.where(s == 0, BATCH * LAT_BLOCKS + b, b * LAT_BLOCKS + lat)


def rglru_mixer(proj, col_in, col_gate, conv_w, conv_b, wa, ba, wx, bx, lam):
    w = BRANCH_W
    row2 = lambda a: a.reshape(1, w)
    outs = None
    for reverse in (False, True):
        d = int(reverse)
        blk = lambda col: (lambda b, s: (_seq_block(b, s, reverse), col))
        const2 = lambda b, s: (0, 0)
        in_specs = [pl.BlockSpec((SEQ_BLOCK, w), blk(col_in)),
                    pl.BlockSpec(conv_w.shape, const2),
                    pl.BlockSpec((1, w), const2),
                    pl.BlockSpec((1,) + wa.shape[1:], lambda b, s: (d, 0, 0, 0)),
                    pl.BlockSpec((1, w), const2),
                    pl.BlockSpec((1,) + wx.shape[1:], lambda b, s: (d, 0, 0, 0)),
                    pl.BlockSpec((1, w), const2),
                    pl.BlockSpec((1, w), const2)]
        args = [proj, conv_w, row2(conv_b), wa, row2(ba[d]), wx, row2(bx[d]), row2(lam[d])]
        if reverse:
            in_specs += [pl.BlockSpec((SEQ_BLOCK, w), blk(0)), pl.BlockSpec((SEQ_BLOCK, w), blk(col_gate))]
            args += [outs, proj]
        outs = pl.pallas_call(
            functools.partial(_rglru_kernel, reverse=reverse),
            grid=(BATCH, LAT_BLOCKS + 1),
            in_specs=in_specs,
            out_specs=pl.BlockSpec((SEQ_BLOCK, w), blk(0)),
            out_shape=jax.ShapeDtypeStruct((ROWS, w), jnp.bfloat16 if reverse else jnp.float32),
            scratch_shapes=[pltpu.VMEM((SEQ_BLOCK, w), jnp.float32), pltpu.VMEM((SEQ_BLOCK, w), jnp.float32),
                            pltpu.VMEM((1, w), jnp.float32)],
            compiler_params=_params("parallel", "arbitrary"),
            name="rglru_bwd" if reverse else "rglru_fwd",
        )(*args)
    return outs


def _sconv_kernel(pb_ref, pc_ref, px_ref, w_ref, o_ref):
    is_ctx = pl.program_id(0) >= BATCH * LAT_BLOCKS
    conv = _masked_conv(pc_ref[...] * px_ref[...], w_ref, SC_CONV_LEFT, is_ctx)
    o_ref[...] = (pb_ref[...] * conv).astype(o_ref.dtype)


def sconv_mixer(proj, col_b, col_c, col_x, conv_w):
    w = BRANCH_W
    spec = lambda col: pl.BlockSpec((SEQ_BLOCK, w), lambda i: (i, col))
    return pl.pallas_call(
        _sconv_kernel,
        grid=(ROWS // SEQ_BLOCK,),
        in_specs=[spec(col_b), spec(col_c), spec(col_x), pl.BlockSpec(conv_w.shape, lambda i: (0, 0))],
        out_specs=pl.BlockSpec((SEQ_BLOCK, w), lambda i: (i, 0)),
        out_shape=jax.ShapeDtypeStruct((ROWS, w), jnp.bfloat16),
        compiler_params=_params("parallel"),
        name="sconv",
    )(proj, proj, proj, conv_w)


def _merge_kernel(h_ref, wg_ref, y_ref, wb_ref, o_ref, acc_ref):
    b = pl.program_id(2)

    @pl.when(b == 0)
    def _():
        acc_ref[...] = jnp.zeros_like(acc_ref)

    gate = jnp.dot(h_ref[...], wg_ref[...], preferred_element_type=jnp.float32)
    proj = jnp.dot(y_ref[0], wb_ref[0], preferred_element_type=jnp.float32)
    acc_ref[...] += jax.nn.sigmoid(gate) * proj

    @pl.when(b == pl.num_programs(2) - 1)
    def _():
        o_ref[...] = acc_ref[...].astype(o_ref.dtype)


def merge_branches(h, w_gate, ys, w_branch, tm=ROW_BLOCK, tn=1024):
    m, d = h.shape
    n_br, _, bw = ys.shape
    n_col = d // tn
    return pl.pallas_call(
        _merge_kernel,
        grid=(m // tm, n_col, n_br),
        in_specs=[pl.BlockSpec((tm, d), lambda i, j, b: (i, 0)),
                  pl.BlockSpec((d, tn), lambda i, j, b: (0, b * n_col + j)),
                  pl.BlockSpec((1, tm, bw), lambda i, j, b: (b, i, 0)),
                  pl.BlockSpec((1, bw, tn), lambda i, j, b: (b, 0, j))],
        out_specs=pl.BlockSpec((tm, tn), lambda i, j, b: (i, j)),
        out_shape=jax.ShapeDtypeStruct((m, d), jnp.bfloat16),
        scratch_shapes=[pltpu.VMEM((tm, tn), jnp.float32)],
        compiler_params=_params("parallel", "parallel", "arbitrary"),
        name="merge_branches",
    )(h, w_gate, ys, w_branch)


def _extract_topk(s, n_top, val_ref, idx_ref, slot):
    n_rows = s.shape[0]
    rid = lax.broadcasted_iota(jnp.int32, s.shape, 0)
    for r in range(n_top):
        m = jnp.max(s, axis=0, keepdims=True)
        am = jnp.min(jnp.where(s == m, rid, n_rows), axis=0, keepdims=True)
        val_ref[slot, r:r + 1, :] = m
        idx_ref[slot, r:r + 1, :] = am
        s = jnp.where(rid == am, -jnp.inf, s)


def _lookup_rows(table, sel):
    out = jnp.zeros(sel.shape, table.dtype)
    for r in range(table.shape[0]):
        out = jnp.where(sel == r, table[r:r + 1, :], out)
    return out


def _peer_topk_kernel(q_ref, keys_ref, i1_ref, i2_ref, w_ref, val_sc, idx_sc, cand_sc, top_sc, pos_sc, ent_sc):
    n_half = keys_ref.shape[0]
    for hp in range(n_half):
        q = q_ref[:, hp * PEER_DKH:(hp + 1) * PEER_DKH]
        s = lax.dot_general(keys_ref[hp], q, (((1,), (1,)), ((), ())), precision=lax.Precision.HIGHEST,
                            preferred_element_type=jnp.float32)
        _extract_topk(s, PEER_TOPK, val_sc, idx_sc, hp)
    for h in range(PEER_HEADS):
        v1, v2 = val_sc[2 * h], val_sc[2 * h + 1]
        for j1 in range(PEER_TOPK):
            cand_sc[j1 * PEER_TOPK:(j1 + 1) * PEER_TOPK, :] = v1[j1:j1 + 1, :] + v2
        _extract_topk(cand_sc[...], PEER_TOPK, top_sc, pos_sc, 0)
        top, pos = top_sc[0], pos_sc[0]
        e = jnp.exp(top - top[0:1, :])
        rows = slice(h * PEER_TOPK, (h + 1) * PEER_TOPK)
        ent_sc[0, rows, :] = _lookup_rows(idx_sc[2 * h], pos >> TOPK_SHIFT).astype(jnp.float32)
        ent_sc[1, rows, :] = _lookup_rows(idx_sc[2 * h + 1], pos & (PEER_TOPK - 1)).astype(jnp.float32)
        ent_sc[2, rows, :] = e / jnp.sum(e, axis=0, keepdims=True)
    i1_ref[...] = ent_sc[0].T.astype(jnp.int32)
    i2_ref[...] = ent_sc[1].T.astype(jnp.int32)
    w_ref[...] = ent_sc[2].T


def peer_topk(q, keys, tt=LANES):
    n_tok = q.shape[0]
    n_ent = PEER_HEADS * PEER_TOPK
    assert n_ent == tt
    ent_spec = pl.BlockSpec((tt, n_ent), lambda i: (i, 0))
    f32, i32 = jnp.float32, jnp.int32
    return pl.pallas_call(
        _peer_topk_kernel,
        grid=(n_tok // tt,),
        in_specs=[pl.BlockSpec((tt, q.shape[1]), lambda i: (i, 0)),
                  pl.BlockSpec(keys.shape, lambda i: (0, 0, 0))],
        out_specs=[ent_spec, ent_spec, ent_spec],
        out_shape=[jax.ShapeDtypeStruct((n_tok, n_ent), i32), jax.ShapeDtypeStruct((n_tok, n_ent), i32),
                   jax.ShapeDtypeStruct((n_tok, n_ent), f32)],
        scratch_shapes=[pltpu.VMEM((2 * PEER_HEADS, PEER_TOPK, tt), f32), pltpu.VMEM((2 * PEER_HEADS, PEER_TOPK, tt), i32),
                        pltpu.VMEM((PEER_TOPK * PEER_TOPK, tt), f32),
                        pltpu.VMEM((1, PEER_TOPK, tt), f32), pltpu.VMEM((1, PEER_TOPK, tt), i32),
                        pltpu.VMEM((3, n_ent, tt), f32)],
        compiler_params=_params("parallel"),
        name="peer_topk",
    )(q, keys)


def _peer_score_kernel(h_ref, ut_ref, i1_ref, i2_ref, o_ref):
    j = pl.program_id(1)

    @pl.when(j == 0)
    def _():
        o_ref[...] = jnp.zeros_like(o_ref)

    s = jnp.dot(h_ref[...], ut_ref[...], preferred_element_type=jnp.float32)
    i1, i2 = i1_ref[...], i2_ref[...]
    acc = o_ref[...]
    n_chunks = s.shape[1] // PEER_NKEYS
    for c in range(n_chunks):
        picked = jnp.take_along_axis(s[:, c * PEER_NKEYS:(c + 1) * PEER_NKEYS], i2, axis=1)
        acc = jnp.where(i1 == j * n_chunks + c, picked, acc)
    o_ref[...] = acc


def peer_scores(h, ut_tab, i1, i2, tm=ROW_BLOCK, tn=1024):
    n_tok, d = h.shape
    n_exp = ut_tab.shape[1]
    n_ent = i1.shape[1]
    tm = min(tm, n_tok)
    assert n_tok % tm == 0 and n_exp % tn == 0 and n_ent == PEER_NKEYS
    ent_spec = pl.BlockSpec((tm, n_ent), lambda i, j: (i, 0))
    return pl.pallas_call(
        _peer_score_kernel,
        grid=(n_tok // tm, n_exp // tn),
        in_specs=[pl.BlockSpec((tm, d), lambda i, j: (i, 0)),
                  pl.BlockSpec((d, tn), lambda i, j: (0, j)),
                  ent_spec, ent_spec],
        out_specs=ent_spec,
        out_shape=jax.ShapeDtypeStruct((n_tok, n_ent), jnp.float32),
        compiler_params=_params("parallel", "arbitrary"),
        name="peer_scores",
    )(h, ut_tab, i1, i2)


def _peer_coef_kernel(sc_ref, w_ref, i1_ref, i2_ref, o_ref, wa_sc):
    wa_sc[...] = w_ref[...] * jax.nn.gelu(sc_ref[...])
    n_keys = o_ref.shape[1]
    n_ent = sc_ref.shape[1]
    key = lax.broadcasted_iota(jnp.int32, (n_keys, n_ent), 0)

    def body(t, carry):
        row = lambda ref: jnp.broadcast_to(ref[pl.ds(t, 1), :], (n_keys, n_ent))
        at = jnp.where(key == row(i1_ref), row(wa_sc), 0.0).astype(jnp.bfloat16)
        bt = jnp.where(key == row(i2_ref), 1.0, 0.0).astype(jnp.bfloat16)
        ct = lax.dot_general(at, bt, (((1,), (1,)), ((), ())), preferred_element_type=jnp.float32)
        o_ref[t] = ct.astype(o_ref.dtype)
        return carry

    lax.fori_loop(0, sc_ref.shape[0], body, 0, unroll=4)


def peer_coef(sc, wts, i1, i2, tb=128):
    n_tok, n_ent = sc.shape
    tb = min(tb, n_tok)
    assert n_tok % tb == 0
    ent_spec = pl.BlockSpec((tb, n_ent), lambda i: (i, 0))
    return pl.pallas_call(
        _peer_coef_kernel,
        grid=(n_tok // tb,),
        in_specs=[ent_spec] * 4,
        out_specs=pl.BlockSpec((tb, PEER_NKEYS, PEER_NKEYS), lambda i: (i, 0, 0)),
        out_shape=jax.ShapeDtypeStruct((n_tok, PEER_NKEYS, PEER_NKEYS), jnp.bfloat16),
        scratch_shapes=[pltpu.VMEM((tb, n_ent), jnp.float32)],
        compiler_params=_params("parallel"),
        name="peer_coef",
    )(sc, wts, i1, i2)


def peer_ffn(h, x, modtab, w_q, keys, ut_tab, v_tab):
    q = mm(h, w_q, tk=D_MODEL)
    i1, i2, wts = peer_topk(q, keys.reshape(2 * PEER_HEADS, PEER_NKEYS, PEER_DKH))
    sc = peer_scores(h, ut_tab, i1, i2)
    coef = peer_coef(sc, wts, i1, i2).reshape(h.shape[0], PEER_NKEYS * PEER_NKEYS)
    return mm_resid(coef, v_tab, x, modtab, MOD_GATE2)


def fourier_mix(u):
    bsz, t, _ = u.shape
    g = u.astype(jnp.float32).reshape(bsz, t, FOURIER_GROUPS, FOURIER_GW)
    z = jnp.fft.fft2(g, axes=(1, 3), norm='ortho').real
    return z.reshape(bsz, t, BRANCH_W).astype(u.dtype)


def mlstm_scan(q, k, v, li, lf, state):
    bsz, nh, t, _ = q.shape
    nc = t // MLSTM_CHUNK

    def chunks(a):
        a = a.reshape(a.shape[:2] + (nc, MLSTM_CHUNK) + a.shape[3:])
        return jnp.moveaxis(a, 2, 0)

    tril = jnp.tril(jnp.ones((MLSTM_CHUNK, MLSTM_CHUNK), dtype=bool))

    def step(carry, inp):
        cmat, nvec, m = carry
        qc, kc, vc, ic, fc = inp
        b = jnp.cumsum(fc, axis=-1)
        logw = jnp.where(tril, b[..., :, None] - b[..., None, :] + ic[..., None, :], -jnp.inf)
        g = b + m[..., None]
        mt = jnp.maximum(g, jnp.max(logw, axis=-1))
        s = jnp.einsum('bhtd,bhsd->bhts', qc, kc) * jnp.exp(logw - mt[..., None])
        w_inter = jnp.exp(g - mt)
        num = jnp.einsum('bhts,bhsv->bhtv', s, vc) + w_inter[..., None] * jnp.einsum('bhvd,bhtd->bhtv', cmat, qc)
        den = jnp.sum(s, axis=-1) + w_inter * jnp.einsum('bhd,bhtd->bht', nvec, qc)
        h = num / jnp.maximum(jnp.abs(den), jnp.exp(-mt))[..., None]
        bl = b[..., -1]
        logu = bl[..., None] - b + ic
        m_new = jnp.maximum(bl + m, jnp.max(logu, axis=-1))
        ws = jnp.exp(logu - m_new[..., None])
        wc = jnp.exp(bl + m - m_new)
        c_new = wc[..., None, None] * cmat + jnp.einsum('bhs,bhsv,bhsd->bhvd', ws, vc, kc)
        n_new = wc[..., None] * nvec + jnp.einsum('bhs,bhsd->bhd', ws, kc)
        return (c_new, n_new, m_new), h

    state, hs = lax.scan(step, state, (chunks(q), chunks(k), chunks(v), chunks(li), chunks(lf)))
    hs = jnp.moveaxis(hs, 0, 2).reshape(bsz, nh, t, MLSTM_DV)
    return hs, state


def mlstm_heads(p_q, p_k, p_v, p_g, gate_b):
    bsz, t, _ = p_q.shape
    f32 = jnp.float32
    q = p_q.astype(f32).reshape(bsz, t, MLSTM_HEADS, MLSTM_DQK).transpose(0, 2, 1, 3) * (MLSTM_DQK ** -0.5)
    k = p_k.astype(f32).reshape(bsz, t, MLSTM_HEADS, MLSTM_DQK).transpose(0, 2, 1, 3)
    v = p_v.astype(f32).reshape(bsz, t, MLSTM_HEADS, MLSTM_DV).transpose(0, 2, 1, 3)
    g = p_g.astype(f32).reshape(bsz, t, 2, 2, MLSTM_HEADS) + gate_b.astype(f32)
    g = g.transpose(2, 3, 0, 4, 1)
    return q, k, v, g


def mlstm_bidir(lat, ctx, gate_b):
    ql, kl, vl, gl = mlstm_heads(*lat, gate_b)
    qc, kc, vc, gc = mlstm_heads(*ctx, gate_b)
    bsz = ql.shape[0]
    outs_l, outs_c = [], []
    for d in range(2):
        flip = (lambda a: jnp.flip(a, axis=2)) if d == 1 else (lambda a: a)
        init = (jnp.zeros((bsz, MLSTM_HEADS, MLSTM_DV, MLSTM_DQK), jnp.float32),
                jnp.zeros((bsz, MLSTM_HEADS, MLSTM_DQK), jnp.float32),
                jnp.zeros((bsz, MLSTM_HEADS), jnp.float32))
        hc, st = mlstm_scan(flip(qc), flip(kc), flip(vc), flip(gc[d, 0]), flip(jax.nn.log_sigmoid(gc[d, 1])), init)
        hl, _ = mlstm_scan(flip(ql), flip(kl), flip(vl), flip(gl[d, 0]), flip(jax.nn.log_sigmoid(gl[d, 1])), st)
        outs_l.append(flip(hl))
        outs_c.append(flip(hc))
    return outs_l[0] + outs_l[1], outs_c[0] + outs_c[1]


def mlstm_out(h, p_o, norm_g):
    h = h * lax.rsqrt(jnp.mean(h * h, axis=-1, keepdims=True) + EPS)
    bsz, nh, t, dv = h.shape
    h = h.transpose(0, 2, 1, 3).reshape(bsz, t, nh * dv) * norm_g.astype(jnp.float32)
    return (h * jax.nn.sigmoid(p_o.astype(jnp.float32))).astype(p_o.dtype)


def _split_streams(a):
    return a[:LAT_ROWS].reshape(BATCH, SEQ, -1), a[LAT_ROWS:].reshape(BATCH, CTX_LEN, -1)


def _join_streams(a_lat, a_ctx):
    return jnp.concatenate([a_lat.reshape(LAT_ROWS, -1), a_ctx.reshape(BATCH * CTX_LEN, -1)], axis=0)


def jnp_mixers(proj, gates, mlstm_gate_b, mlstm_norm_g):
    col = lambda lo, hi: _split_streams(proj[:, lo:hi])
    four_l, four_c = col(0, BRANCH_W)
    q_l, q_c = col(BRANCH_W, BRANCH_W + MLSTM_HEADS * MLSTM_DQK)
    k_l, k_c = col(BRANCH_W + MLSTM_HEADS * MLSTM_DQK, 2 * BRANCH_W)
    v_l, v_c = col(2 * BRANCH_W, 3 * BRANCH_W)
    o_l, o_c = col(3 * BRANCH_W, 4 * BRANCH_W)
    g_l, g_c = _split_streams(gates[:, :N_MLSTM_GATES])
    y_four = _join_streams(fourier_mix(four_l), fourier_mix(four_c))
    ml_l, ml_c = mlstm_bidir((q_l, k_l, v_l, g_l), (q_c, k_c, v_c, g_c), mlstm_gate_b)
    y_ml = _join_streams(mlstm_out(ml_l, o_l, mlstm_norm_g), mlstm_out(ml_c, o_c, mlstm_norm_g))
    return y_four.astype(jnp.bfloat16), y_ml.astype(jnp.bfloat16)


def kernel(x, c, ctx, c_ctx, w_mod, b_mod, g_norm1, g_norm2, w_in, w_branch, w_out, mlstm_gate_b, mlstm_norm_g,
           rg_conv_w, rg_conv_b, rg_wa, rg_ba, rg_wx, rg_bx, rg_lam, sc_conv_w, peer_wq, peer_keys, peer_u,
           peer_v, g_final):
    bf16 = jnp.bfloat16
    xs = jnp.concatenate([x.reshape(LAT_ROWS, D_MODEL), ctx.reshape(BATCH * CTX_LEN, D_MODEL)], axis=0)
    cond = jax.nn.silu(jnp.concatenate([c, c_ctx[None, :]], axis=0))
    for l in range(DEPTH):
        last = l == DEPTH - 1
        modtab = (mm(cond, w_mod[l], tm=16, tn=2048, tk=2048) + b_mod[l]).reshape(BATCH + 1, 6, 1, D_MODEL)
        w_main = jnp.concatenate([w_in[l][:, :COL_MAIN], w_in[l][:, COL_GATES:COL_MERGE]], axis=1).astype(bf16)
        w_gates = jnp.pad(w_in[l][:, COL_MAIN:COL_GATES], ((0, 0), (0, LANES - N_MLSTM_GATES))).astype(bf16)
        w_merge = w_in[l][:, COL_MERGE:].astype(bf16)

        h = norm_mod(xs, g_norm1[l], modtab, MOD_SHIFT1, MOD_SCALE1)
        proj = mm(h, w_main, tk=D_MODEL)
        gates = mm(h, w_gates, tk=D_MODEL)
        y_four, y_ml = jnp_mixers(proj, gates, mlstm_gate_b[l], mlstm_norm_g[l])
        y_rg = rglru_mixer(proj, 4, 5, rg_conv_w[l], rg_conv_b[l], rg_wa[l], rg_ba[l], rg_wx[l], rg_bx[l], rg_lam[l])
        y_sc = sconv_mixer(proj, 6, 7, 8, sc_conv_w[l])
        ys = jnp.stack([y_four, y_ml, y_rg, y_sc], axis=0)
        if last:
            xs, h, ys = xs[:LAT_ROWS], h[:LAT_ROWS], ys[:, :LAT_ROWS]
        merged = merge_branches(h, w_merge, ys, w_branch[l].astype(bf16))
        xs = mm_resid(merged, w_out[l].astype(bf16), xs, modtab, MOD_GATE1)

        h2 = norm_mod(xs, g_norm2[l], modtab, MOD_SHIFT2, MOD_SCALE2)
        xs = peer_ffn(h2, xs, modtab, peer_wq[l].astype(bf16), peer_keys[l], peer_u[l].T.astype(bf16),
                      peer_v[l].astype(bf16))
    return rmsnorm_rows(xs, g_final).reshape(BATCH, SEQ, D_MODEL)
```

```python
import functools

import jax
import jax.numpy as jnp
from jax import lax
from jax.experimental import pallas as pl
from jax.experimental.pallas import tpu as pltpu

D_MODEL = 4096
BATCH = 2
SEQ = 4096
DEPTH = 2
CTX_LEN = 256
GRID_W = 64
N_BRANCHES = 4
BRANCH_W = D_MODEL // 4
FOURIER_GROUPS = 4
FOURIER_GW = BRANCH_W // FOURIER_GROUPS
MLSTM_HEADS = 4
MLSTM_DV = BRANCH_W // MLSTM_HEADS
MLSTM_DQK = MLSTM_DV // 2
MLSTM_CHUNK = 128
RG_BLOCKS = 8
RG_BW = BRANCH_W // RG_BLOCKS
RG_C = 8.0
RG_CONV_LEFT = 2
SC_CONV_LEFT = 1
PEER_HEADS = 8
PEER_NKEYS = 128
PEER_DK = 256
PEER_DKH = PEER_DK // 2
PEER_TOPK = 16
TOPK_SHIFT = PEER_TOPK.bit_length() - 1
EPS = 1e-6

N_MLSTM_GATES = 2 * 2 * MLSTM_HEADS
COL_MAIN = 4 * BRANCH_W
COL_GATES = COL_MAIN + N_MLSTM_GATES
COL_MERGE = COL_GATES + 5 * BRANCH_W

LAT_ROWS = BATCH * SEQ
ROWS = LAT_ROWS + BATCH * CTX_LEN
SEQ_BLOCK = CTX_LEN
LAT_BLOCKS = SEQ // SEQ_BLOCK
ROW_BLOCK = 512
SUBLANES = 8
LANES = 128
VMEM_LIMIT_BYTES = 48 * 1024 * 1024

MOD_SHIFT1, MOD_SCALE1, MOD_GATE1, MOD_SHIFT2, MOD_SCALE2, MOD_GATE2 = range(6)


def _round_up(x, m):
    return (x + m - 1) // m * m


def _params(*semantics):
    return pltpu.CompilerParams(dimension_semantics=semantics, vmem_limit_bytes=VMEM_LIMIT_BYTES)


def _segment(row_block, rows_per_block):
    return jnp.minimum(row_block // (SEQ // rows_per_block), BATCH)


def _mm_kernel(a_ref, b_ref, o_ref, acc_ref):
    @pl.when(pl.program_id(2) == 0)
    def _():
        acc_ref[...] = jnp.zeros_like(acc_ref)

    acc_ref[...] += jnp.dot(a_ref[...].astype(jnp.bfloat16), b_ref[...].astype(jnp.bfloat16),
                            preferred_element_type=jnp.float32)

    @pl.when(pl.program_id(2) == pl.num_programs(2) - 1)
    def _():
        o_ref[...] = acc_ref[...].astype(o_ref.dtype)


def _mm_fullk_kernel(a_ref, b_ref, o_ref):
    o_ref[...] = jnp.dot(a_ref[...].astype(jnp.bfloat16), b_ref[...].astype(jnp.bfloat16),
                         preferred_element_type=jnp.float32).astype(o_ref.dtype)


def mm(a, b, out_dtype=jnp.float32, tm=ROW_BLOCK, tn=1024, tk=2048):
    m, k = a.shape
    _, n = b.shape
    tm = min(tm, _round_up(m, 16))
    tn = min(tn, _round_up(n, LANES))
    tk = min(tk, k)
    mp, np_ = _round_up(m, tm), _round_up(n, tn)
    if mp != m:
        a = jnp.pad(a, ((0, mp - m), (0, 0)))
    if np_ != n:
        b = jnp.pad(b, ((0, 0), (0, np_ - n)))
    if tk == k:
        out = pl.pallas_call(
            _mm_fullk_kernel,
            grid=(mp // tm, np_ // tn),
            in_specs=[pl.BlockSpec((tm, k), lambda i, j: (i, 0)),
                      pl.BlockSpec((k, tn), lambda i, j: (0, j))],
            out_specs=pl.BlockSpec((tm, tn), lambda i, j: (i, j)),
            out_shape=jax.ShapeDtypeStruct((mp, np_), out_dtype),
            compiler_params=_params("parallel", "parallel"),
            name="mm_fullk",
        )(a, b)
    else:
        out = pl.pallas_call(
            _mm_kernel,
            grid=(mp // tm, np_ // tn, k // tk),
            in_specs=[pl.BlockSpec((tm, tk), lambda i, j, kk: (i, kk)),
                      pl.BlockSpec((tk, tn), lambda i, j, kk: (kk, j))],
            out_specs=pl.BlockSpec((tm, tn), lambda i, j, kk: (i, j)),
            out_shape=jax.ShapeDtypeStruct((mp, np_), out_dtype),
            scratch_shapes=[pltpu.VMEM((tm, tn), jnp.float32)],
            compiler_params=_params("parallel", "parallel", "arbitrary"),
            name="mm",
        )(a, b)
    if mp != m or np_ != n:
        out = out[:m, :n]
    return out


def _mm_resid_kernel(a_ref, b_ref, x_ref, g_ref, o_ref, acc_ref):
    @pl.when(pl.program_id(2) == 0)
    def _():
        acc_ref[...] = jnp.zeros_like(acc_ref)

    acc_ref[...] += jnp.dot(a_ref[...], b_ref[...], preferred_element_type=jnp.float32)

    @pl.when(pl.program_id(2) == pl.num_programs(2) - 1)
    def _():
        o_ref[...] = x_ref[...] + g_ref[0, 0] * acc_ref[...]


def mm_resid(a, b, x, modtab, which, tm=ROW_BLOCK, tn=1024, tk=2048):
    m, k = a.shape
    _, n = b.shape
    tn, tk = min(tn, n), min(tk, k)
    assert m % tm == 0 and n % tn == 0 and k % tk == 0
    return pl.pallas_call(
        _mm_resid_kernel,
        grid=(m // tm, n // tn, k // tk),
        in_specs=[pl.BlockSpec((tm, tk), lambda i, j, kk: (i, kk)),
                  pl.BlockSpec((tk, tn), lambda i, j, kk: (kk, j)),
                  pl.BlockSpec((tm, tn), lambda i, j, kk: (i, j)),
                  pl.BlockSpec((1, 1, 1, tn), lambda i, j, kk: (_segment(i, tm), which, 0, j))],
        out_specs=pl.BlockSpec((tm, tn), lambda i, j, kk: (i, j)),
        out_shape=jax.ShapeDtypeStruct((m, n), jnp.float32),
        scratch_shapes=[pltpu.VMEM((tm, tn), jnp.float32)],
        compiler_params=_params("parallel", "parallel", "arbitrary"),
        name="mm_resid",
    )(a, b, x, modtab)


def _norm_mod_kernel(x_ref, g_ref, sh_ref, sc_ref, o_ref):
    x = x_ref[...]
    y = x * lax.rsqrt(jnp.mean(x * x, axis=-1, keepdims=True) + EPS)
    o_ref[...] = ((y * g_ref[...]) * (1.0 + sc_ref[0, 0]) + sh_ref[0, 0]).astype(o_ref.dtype)


def norm_mod(x, g, modtab, which_shift, which_scale, tm=SEQ_BLOCK):
    m, d = x.shape
    mod_spec = lambda which: pl.BlockSpec((1, 1, 1, d), lambda i: (_segment(i, tm), which, 0, 0))
    return pl.pallas_call(
        _norm_mod_kernel,
        grid=(m // tm,),
        in_specs=[pl.BlockSpec((tm, d), lambda i: (i, 0)),
                  pl.BlockSpec((1, d), lambda i: (0, 0)),
                  mod_spec(which_shift), mod_spec(which_scale)],
        out_specs=pl.BlockSpec((tm, d), lambda i: (i, 0)),
        out_shape=jax.ShapeDtypeStruct((m, d), jnp.bfloat16),
        compiler_params=_params("parallel"),
        name="norm_mod",
    )(x, g.reshape(1, d), modtab, modtab)


def _rmsnorm_kernel(x_ref, g_ref, o_ref):
    x = x_ref[...]
    o_ref[...] = x * lax.rsqrt(jnp.mean(x * x, axis=-1, keepdims=True) + EPS) * g_ref[...]


def rmsnorm_rows(x, g, tm=SEQ_BLOCK):
    m, d = x.shape
    return pl.pallas_call(
        _rmsnorm_kernel,
        grid=(m // tm,),
        in_specs=[pl.BlockSpec((tm, d), lambda i: (i, 0)), pl.BlockSpec((1, d), lambda i: (0, 0))],
        out_specs=pl.BlockSpec((tm, d), lambda i: (i, 0)),
        out_shape=jax.ShapeDtypeStruct((m, d), jnp.float32),
        compiler_params=_params("parallel"),
        name="rmsnorm",
    )(x, g.reshape(1, d))


def _masked_conv(u, w_ref, pad_l, is_ctx):
    rows = u.shape[0]
    t = lax.broadcasted_iota(jnp.int32, (rows, 1), 0)
    seg = jnp.where(is_ctx, rows, GRID_W)
    pos = t & (seg - 1)
    y = None
    for j in range(w_ref.shape[0]):
        k = j - pad_l
        if k == 0:
            sh = u
        else:
            sh = pltpu.roll(u, (-k) % rows, axis=0)
            sh = jnp.where((pos + k >= 0) & (pos + k < seg), sh, 0.0)
        term = w_ref[j:j + 1, :] * sh
        y = term if y is None else y + term
    return y


def _rglru_kernel(*refs, reverse):
    if reverse:
        (p6_ref, cw_ref, cb_ref, wa_ref, ba_ref, wx_ref, bx_ref, lam_ref, hf_ref, p7_ref,
         o_ref, a_sc, b_sc, h_sc) = refs
    else:
        p6_ref, cw_ref, cb_ref, wa_ref, ba_ref, wx_ref, bx_ref, lam_ref, o_ref, a_sc, b_sc, h_sc = refs
    s = pl.program_id(1)

    @pl.when(s == 0)
    def _():
        h_sc[...] = jnp.zeros_like(h_sc)

    u = _masked_conv(p6_ref[...], cw_ref, RG_CONV_LEFT, s == 0) + cb_ref[...]
    ub = u.astype(jnp.bfloat16)
    for g in range(RG_BLOCKS):
        cols = slice(g * RG_BW, (g + 1) * RG_BW)
        ug = ub[:, cols]
        r = jax.nn.sigmoid(jnp.dot(ug, wa_ref[0, g].astype(jnp.bfloat16), preferred_element_type=jnp.float32)
                           + ba_ref[:, cols])
        i = jax.nn.sigmoid(jnp.dot(ug, wx_ref[0, g].astype(jnp.bfloat16), preferred_element_type=jnp.float32)
                           + bx_ref[:, cols])
        neg_lam = -lam_ref[:, cols]
        softplus = jnp.log1p(jnp.exp(-jnp.abs(neg_lam))) + jnp.maximum(neg_lam, 0.0)
        log_a = (-RG_C * softplus) * r
        a_sc[:, cols] = jnp.exp(log_a)
        b_sc[:, cols] = jnp.sqrt(1.0 - jnp.exp(2.0 * log_a)) * (i * u[:, cols])

    n_groups = a_sc.shape[0] // SUBLANES
    row = lax.broadcasted_iota(jnp.int32, (SUBLANES, a_sc.shape[1]), 0)

    def body(it, h_prev):
        grp = (n_groups - 1 - it) if reverse else it
        off = pl.multiple_of(grp * SUBLANES, SUBLANES)
        a = a_sc[pl.ds(off, SUBLANES), :]
        b = b_sc[pl.ds(off, SUBLANES), :]
        for k in (1, 2, 4):
            shift = (SUBLANES - k) if reverse else k
            inside = (row < SUBLANES - k) if reverse else (row >= k)
            a_s = jnp.where(inside, pltpu.roll(a, shift, axis=0), 1.0)
            b_s = jnp.where(inside, pltpu.roll(b, shift, axis=0), 0.0)
            b = a * b_s + b
            a = a * a_s
        h = b + a * h_prev
        b_sc[pl.ds(off, SUBLANES), :] = h
        return h[0:1] if reverse else h[SUBLANES - 1:SUBLANES]

    h_sc[...] = lax.fori_loop(0, n_groups, body, h_sc[...])
    if reverse:
        o_ref[...] = (jax.nn.gelu(p7_ref[...]) * (hf_ref[...] + b_sc[...])).astype(o_ref.dtype)
    else:
        o_ref[...] = b_sc[...]


def _seq_block(b, s, reverse):
    lat = (LAT_BLOCKS - s) if reverse else (s - 1)
    return jnp.where(s == 0, BATCH * LAT_BLOCKS + b, b * LAT_BLOCKS + lat)


def rglru_mixer(proj, col_in, col_gate, conv_w, conv_b, wa, ba, wx, bx, lam):
    w = BRANCH_W
    row2 = lambda a: a.reshape(1, w)
    outs = None
    for reverse in (False, True):
        d = int(reverse)
        blk = lambda col: (lambda b, s: (_seq_block(b, s, reverse), col))
        const2 = lambda b, s: (0, 0)
        in_specs = [pl.BlockSpec((SEQ_BLOCK, w), blk(col_in)),
                    pl.BlockSpec(conv_w.shape, const2),
                    pl.BlockSpec((1, w), const2),
                    pl.BlockSpec((1,) + wa.shape[1:], lambda b, s: (d, 0, 0, 0)),
                    pl.BlockSpec((1, w), const2),
                    pl.BlockSpec((1,) + wx.shape[1:], lambda b, s: (d, 0, 0, 0)),
                    pl.BlockSpec((1, w), const2),
                    pl.BlockSpec((1, w), const2)]
        args = [proj, conv_w, row2(conv_b), wa, row2(ba[d]), wx, row2(bx[d]), row2(lam[d])]
        if reverse:
            in_specs += [pl.BlockSpec((SEQ_BLOCK, w), blk(0)), pl.BlockSpec((SEQ_BLOCK, w), blk(col_gate))]
            args += [outs, proj]
        outs = pl.pallas_call(
            functools.partial(_rglru_kernel, reverse=reverse),
            grid=(BATCH, LAT_BLOCKS + 1),
            in_specs=in_specs,
            out_specs=pl.BlockSpec((SEQ_BLOCK, w), blk(0)),
            out_shape=jax.ShapeDtypeStruct((ROWS, w), jnp.bfloat16 if reverse else jnp.float32),
            scratch_shapes=[pltpu.VMEM((SEQ_BLOCK, w), jnp.float32), pltpu.VMEM((SEQ_BLOCK, w), jnp.float32),
                            pltpu.VMEM((1, w), jnp.float32)],
            compiler_params=_params("parallel", "arbitrary"),
            name="rglru_bwd" if reverse else "rglru_fwd",
        )(*args)
    return outs


def _sconv_kernel(pb_ref, pc_ref, px_ref, w_ref, o_ref):
    is_ctx = pl.program_id(0) >= BATCH * LAT_BLOCKS
    conv = _masked_conv(pc_ref[...] * px_ref[...], w_ref, SC_CONV_LEFT, is_ctx)
    o_ref[...] = (pb_ref[...] * conv).astype(o_ref.dtype)


def sconv_mixer(proj, col_b, col_c, col_x, conv_w):
    w = BRANCH_W
    spec = lambda col: pl.BlockSpec((SEQ_BLOCK, w), lambda i: (i, col))
    return pl.pallas_call(
        _sconv_kernel,
        grid=(ROWS // SEQ_BLOCK,),
        in_specs=[spec(col_b), spec(col_c), spec(col_x), pl.BlockSpec(conv_w.shape, lambda i: (0, 0))],
        out_specs=pl.BlockSpec((SEQ_BLOCK, w), lambda i: (i, 0)),
        out_shape=jax.ShapeDtypeStruct((ROWS, w), jnp.bfloat16),
        compiler_params=_params("parallel"),
        name="sconv",
    )(proj, proj, proj, conv_w)


def _merge_kernel(h_ref, wg_ref, y_ref, wb_ref, o_ref, acc_ref):
    b = pl.program_id(2)

    @pl.when(b == 0)
    def _():
        acc_ref[...] = jnp.zeros_like(acc_ref)

    gate = jnp.dot(h_ref[...], wg_ref[...], preferred_element_type=jnp.float32)
    proj = jnp.dot(y_ref[0], wb_ref[0], preferred_element_type=jnp.float32)
    acc_ref[...] += jax.nn.sigmoid(gate) * proj

    @pl.when(b == pl.num_programs(2) - 1)
    def _():
        o_ref[...] = acc_ref[...].astype(o_ref.dtype)


def merge_branches(h, w_gate, ys, w_branch, tm=ROW_BLOCK, tn=1024):
    m, d = h.shape
    n_br, _, bw = ys.shape
    n_col = d // tn
    return pl.pallas_call(
        _merge_kernel,
        grid=(m // tm, n_col, n_br),
        in_specs=[pl.BlockSpec((tm, d), lambda i, j, b: (i, 0)),
                  pl.BlockSpec((d, tn), lambda i, j, b: (0, b * n_col + j)),
                  pl.BlockSpec((1, tm, bw), lambda i, j, b: (b, i, 0)),
                  pl.BlockSpec((1, bw, tn), lambda i, j, b: (b, 0, j))],
        out_specs=pl.BlockSpec((tm, tn), lambda i, j, b: (i, j)),
        out_shape=jax.ShapeDtypeStruct((m, d), jnp.bfloat16),
        scratch_shapes=[pltpu.VMEM((tm, tn), jnp.float32)],
        compiler_params=_params("parallel", "parallel", "arbitrary"),
        name="merge_branches",
    )(h, w_gate, ys, w_branch)


def _extract_topk(s, n_top, val_ref, idx_ref, slot):
    n_rows = s.shape[0]
    rid = lax.broadcasted_iota(jnp.int32, s.shape, 0)
    for r in range(n_top):
        m = jnp.max(s, axis=0, keepdims=True)
        am = jnp.min(jnp.where(s == m, rid, n_rows), axis=0, keepdims=True)
        val_ref[slot, r:r + 1, :] = m
        idx_ref[slot, r:r + 1, :] = am
        s = jnp.where(rid == am, -jnp.inf, s)


def _lookup_rows(table, sel):
    out = jnp.zeros(sel.shape, table.dtype)
    for r in range(table.shape[0]):
        out = jnp.where(sel == r, table[r:r + 1, :], out)
    return out


def _peer_topk_kernel(q_ref, keys_ref, i1_ref, i2_ref, w_ref, val_sc, idx_sc, cand_sc, top_sc, pos_sc, ent_sc):
    n_half = keys_ref.shape[0]
    for hp in range(n_half):
        q = q_ref[:, hp * PEER_DKH:(hp + 1) * PEER_DKH]
        s = lax.dot_general(keys_ref[hp], q, (((1,), (1,)), ((), ())), precision=lax.Precision.HIGHEST,
                            preferred_element_type=jnp.float32)
        _extract_topk(s, PEER_TOPK, val_sc, idx_sc, hp)
    for h in range(PEER_HEADS):
        v1, v2 = val_sc[2 * h], val_sc[2 * h + 1]
        for j1 in range(PEER_TOPK):
            cand_sc[j1 * PEER_TOPK:(j1 + 1) * PEER_TOPK, :] = v1[j1:j1 + 1, :] + v2
        _extract_topk(cand_sc[...], PEER_TOPK, top_sc, pos_sc, 0)
        top, pos = top_sc[0], pos_sc[0]
        e = jnp.exp(top - top[0:1, :])
        rows = slice(h * PEER_TOPK, (h + 1) * PEER_TOPK)
        ent_sc[0, rows, :] = _lookup_rows(idx_sc[2 * h], pos >> TOPK_SHIFT).astype(jnp.float32)
        ent_sc[1, rows, :] = _lookup_rows(idx_sc[2 * h + 1], pos & (PEER_TOPK - 1)).astype(jnp.float32)
        ent_sc[2, rows, :] = e / jnp.sum(e, axis=0, keepdims=True)
    i1_ref[...] = ent_sc[0].T.astype(jnp.int32)
    i2_ref[...] = ent_sc[1].T.astype(jnp.int32)
    w_ref[...] = ent_sc[2].T


def peer_topk(q, keys, tt=LANES):
    n_tok = q.shape[0]
    n_ent = PEER_HEADS * PEER_TOPK
    assert n_ent == tt
    ent_spec = pl.BlockSpec((tt, n_ent), lambda i: (i, 0))
    f32, i32 = jnp.float32, jnp.int32
    return pl.pallas_call(
        _peer_topk_kernel,
        grid=(n_tok // tt,),
        in_specs=[pl.BlockSpec((tt, q.shape[1]), lambda i: (i, 0)),
                  pl.BlockSpec(keys.shape, lambda i: (0, 0, 0))],
        out_specs=[ent_spec, ent_spec, ent_spec],
        out_shape=[jax.ShapeDtypeStruct((n_tok, n_ent), i32), jax.ShapeDtypeStruct((n_tok, n_ent), i32),
                   jax.ShapeDtypeStruct((n_tok, n_ent), f32)],
        scratch_shapes=[pltpu.VMEM((2 * PEER_HEADS, PEER_TOPK, tt), f32), pltpu.VMEM((2 * PEER_HEADS, PEER_TOPK, tt), i32),
                        pltpu.VMEM((PEER_TOPK * PEER_TOPK, tt), f32),
                        pltpu.VMEM((1, PEER_TOPK, tt), f32), pltpu.VMEM((1, PEER_TOPK, tt), i32),
                        pltpu.VMEM((3, n_ent, tt), f32)],
        compiler_params=_params("parallel"),
        name="peer_topk",
    )(q, keys)


def _peer_score_kernel(h_ref, ut_ref, i1_ref, i2_ref, o_ref):
    j = pl.program_id(1)

    @pl.when(j == 0)
    def _():
        o_ref[...] = jnp.zeros_like(o_ref)

    s = jnp.dot(h_ref[...], ut_ref[...], preferred_element_type=jnp.float32)
    i1, i2 = i1_ref[...], i2_ref[...]
    acc = o_ref[...]
    n_chunks = s.shape[1] // PEER_NKEYS
    for c in range(n_chunks):
        picked = jnp.take_along_axis(s[:, c * PEER_NKEYS:(c + 1) * PEER_NKEYS], i2, axis=1)
        acc = jnp.where(i1 == j * n_chunks + c, picked, acc)
    o_ref[...] = acc


def peer_scores(h, ut_tab, i1, i2, tm=ROW_BLOCK, tn=1024):
    n_tok, d = h.shape
    n_exp = ut_tab.shape[1]
    n_ent = i1.shape[1]
    tm = min(tm, n_tok)
    assert n_tok % tm == 0 and n_exp % tn == 0 and n_ent == PEER_NKEYS
    ent_spec = pl.BlockSpec((tm, n_ent), lambda i, j: (i, 0))
    return pl.pallas_call(
        _peer_score_kernel,
        grid=(n_tok // tm, n_exp // tn),
        in_specs=[pl.BlockSpec((tm, d), lambda i, j: (i, 0)),
                  pl.BlockSpec((d, tn), lambda i, j: (0, j)),
                  ent_spec, ent_spec],
        out_specs=ent_spec,
        out_shape=jax.ShapeDtypeStruct((n_tok, n_ent), jnp.float32),
        compiler_params=_params("parallel", "arbitrary"),
        name="peer_scores",
    )(h, ut_tab, i1, i2)


def _peer_coef_kernel(sc_ref, w_ref, i1_ref, i2_ref, o_ref, wa_sc):
    wa_sc[...] = w_ref[...] * jax.nn.gelu(sc_ref[...])
    n_keys = o_ref.shape[1]
    n_ent = sc_ref.shape[1]
    key = lax.broadcasted_iota(jnp.int32, (n_keys, n_ent), 0)

    def body(t, carry):
        row = lambda ref: jnp.broadcast_to(ref[pl.ds(t, 1), :], (n_keys, n_ent))
        at = jnp.where(key == row(i1_ref), row(wa_sc), 0.0).astype(jnp.bfloat16)
        bt = jnp.where(key == row(i2_ref), 1.0, 0.0).astype(jnp.bfloat16)
        ct = lax.dot_general(at, bt, (((1,), (1,)), ((), ())), preferred_element_type=jnp.float32)
        o_ref[t] = ct.astype(o_ref.dtype)
        return carry

    lax.fori_loop(0, sc_ref.shape[0], body, 0, unroll=4)


def peer_coef(sc, wts, i1, i2, tb=128):
    n_tok, n_ent = sc.shape
    tb = min(tb, n_tok)
    assert n_tok % tb == 0
    ent_spec = pl.BlockSpec((tb, n_ent), lambda i: (i, 0))
    return pl.pallas_call(
        _peer_coef_kernel,
        grid=(n_tok // tb,),
        in_specs=[ent_spec] * 4,
        out_specs=pl.BlockSpec((tb, PEER_NKEYS, PEER_NKEYS), lambda i: (i, 0, 0)),
        out_shape=jax.ShapeDtypeStruct((n_tok, PEER_NKEYS, PEER_NKEYS), jnp.bfloat16),
        scratch_shapes=[pltpu.VMEM((tb, n_ent), jnp.float32)],
        compiler_params=_params("parallel"),
        name="peer_coef",
    )(sc, wts, i1, i2)


def peer_ffn(h, x, modtab, w_q, keys, ut_tab, v_tab):
    q = mm(h, w_q, tk=D_MODEL)
    i1, i2, wts = peer_topk(q, keys.reshape(2 * PEER_HEADS, PEER_NKEYS, PEER_DKH))
    sc = peer_scores(h, ut_tab, i1, i2)
    coef = peer_coef(sc, wts, i1, i2).reshape(h.shape[0], PEER_NKEYS * PEER_NKEYS)
    return mm_resid(coef, v_tab, x, modtab, MOD_GATE2)


def _dft_parts(n):
    idx = jnp.arange(n, dtype=jnp.int32)
    ang = ((idx[:, None] * idx[None, :]) % n).astype(jnp.float32) * (2.0 * jnp.pi / n)
    scale = n ** -0.5
    return jnp.cos(ang) * scale, jnp.sin(ang) * scale


def dft_tables():
    cc, sc = _dft_parts(FOURIER_GW)
    ct_l, st_l = _dft_parts(SEQ)
    ct_c, st_c = _dft_parts(CTX_LEN)
    bf16 = jnp.bfloat16
    return (jnp.concatenate([cc, sc], axis=1).astype(bf16),
            jnp.concatenate([ct_l, -st_l], axis=1).astype(bf16),
            jnp.concatenate([ct_c, -st_c], axis=1).astype(bf16))


def _fourier_kernel(p_ref, chan_ref, pos_ref, *rest):
    o_ref, gcs_sc = rest[-2:]
    seq = p_ref.shape[0]

    @pl.when(pl.program_id(2) == 0)
    def _():
        gc = jnp.dot(p_ref[...].astype(jnp.bfloat16), chan_ref[...], preferred_element_type=jnp.float32)
        gcs_sc[0:seq, :] = gc[:, :FOURIER_GW].astype(gcs_sc.dtype)
        gcs_sc[seq:2 * seq, :] = gc[:, FOURIER_GW:].astype(gcs_sc.dtype)

    o_ref[...] = jnp.dot(pos_ref[...], gcs_sc[...], preferred_element_type=jnp.float32).astype(o_ref.dtype)


def fourier_mixer(proj, tables, tr=ROW_BLOCK):
    chan, pos_lat, pos_ctx = tables
    gw = FOURIER_GW

    def call(seq, first_row, pos, tr, prev):
        n_r = seq // tr
        in_specs = [pl.BlockSpec((seq, gw), lambda b, g, r: (first_row // seq + b, g)),
                    pl.BlockSpec(chan.shape, lambda b, g, r: (0, 0)),
                    pl.BlockSpec((tr, 2 * seq), lambda b, g, r: (r, 0))]
        args = [proj, chan, pos]
        if prev is not None:
            in_specs.append(pl.BlockSpec(memory_space=pl.ANY))
            args.append(prev)
        return pl.pallas_call(
            _fourier_kernel,
            grid=(BATCH, FOURIER_GROUPS, n_r),
            in_specs=in_specs,
            out_specs=pl.BlockSpec((tr, gw), lambda b, g, r: (first_row // tr + b * n_r + r, g)),
            out_shape=jax.ShapeDtypeStruct((ROWS, BRANCH_W), jnp.bfloat16),
            scratch_shapes=[pltpu.VMEM((2 * seq, gw), jnp.bfloat16)],
            input_output_aliases={} if prev is None else {3: 0},
            compiler_params=_params("parallel", "parallel", "arbitrary"),
            name="fourier_lat" if prev is None else "fourier_ctx",
        )(*args)

    y = call(SEQ, 0, pos_lat, tr, None)
    return call(CTX_LEN, LAT_ROWS, pos_ctx, CTX_LEN, y)


def _mlstm_kernel(q_ref, k_ref, v_ref, gt_ref, o_ref, ct_sc, n_sc, m_sc):
    d = pl.program_id(0)
    head = pl.program_id(1) % MLSTM_HEADS
    f32, bf16 = jnp.float32, jnp.bfloat16
    n_t = q_ref.shape[0]

    @pl.when(pl.program_id(2) == 0)
    def _():
        ct_sc[...] = jnp.zeros_like(ct_sc)
        n_sc[...] = jnp.zeros_like(n_sc)
        m_sc[...] = jnp.zeros_like(m_sc)

    q = q_ref[...] * (MLSTM_DQK ** -0.5)
    k = k_ref[...]
    v = v_ref[...]
    gate0 = d * (2 * MLSTM_HEADS) + head
    li = gt_ref[pl.ds(gate0, 1), :]
    gf = gt_ref[pl.ds(gate0 + MLSTM_HEADS, 1), :]
    lf = -(jnp.log1p(jnp.exp(-jnp.abs(gf))) + jnp.maximum(-gf, 0.0))

    r = lax.broadcasted_iota(jnp.int32, (n_t, n_t), 0)
    c = lax.broadcasted_iota(jnp.int32, (n_t, n_t), 1)
    upto = (c - r) * (1 - 2 * d) <= 0
    eye = r == c
    b_col = jnp.sum(jnp.where(upto, lf, 0.0), axis=1, keepdims=True)
    b_row = jnp.sum(jnp.where(eye, b_col, 0.0), axis=0, keepdims=True)
    m_prev = m_sc[...]
    logw = jnp.where(upto, b_col - b_row + li, -jnp.inf)
    g_col = b_col + m_prev
    mt = jnp.maximum(g_col, jnp.max(logw, axis=1, keepdims=True))
    qb = q.astype(bf16)
    s = lax.dot_general(qb, k.astype(bf16), (((1,), (1,)), ((), ())), preferred_element_type=f32) * jnp.exp(logw - mt)
    w_inter = jnp.exp(g_col - mt)
    num = (jnp.dot(s.astype(bf16), v.astype(bf16), preferred_element_type=f32)
           + w_inter * jnp.dot(qb, ct_sc[...].astype(bf16), preferred_element_type=f32))
    den = jnp.sum(s, axis=1, keepdims=True) + w_inter * jnp.sum(q * n_sc[...], axis=1, keepdims=True)
    o_ref[0] = num / jnp.maximum(jnp.abs(den), jnp.exp(-mt))

    total = jnp.sum(lf, axis=1, keepdims=True)
    logu = total - b_row + li
    m_new = jnp.maximum(total + m_prev, jnp.max(logu, axis=1, keepdims=True))
    ws_row = jnp.exp(logu - m_new)
    wc = jnp.exp(total + m_prev - m_new)
    ws_col = jnp.sum(jnp.where(eye, ws_row, 0.0), axis=1, keepdims=True)
    kv = lax.dot_general(k.astype(bf16), (ws_col * v).astype(bf16), (((0,), (0,)), ((), ())),
                         preferred_element_type=f32)
    ct_sc[...] = wc * ct_sc[...] + kv
    n_sc[...] = wc * n_sc[...] + jnp.sum(ws_col * k, axis=0, keepdims=True)
    m_sc[...] = m_new


def _mlstm_chunk(d, bh, s):
    b = bh // MLSTM_HEADS
    n_ctx, n_lat = CTX_LEN // MLSTM_CHUNK, SEQ // MLSTM_CHUNK
    ctx_j = jnp.where(d == 0, s, n_ctx - 1 - s)
    lat_j = jnp.where(d == 0, s - n_ctx, n_ctx + n_lat - 1 - s)
    return jnp.where(s < n_ctx, BATCH * n_lat + b * n_ctx + ctx_j, b * n_lat + lat_j)


def mlstm_mixer(proj, gates_t):
    t = MLSTM_CHUNK
    q_col0 = BRANCH_W // MLSTM_DQK
    k_col0 = q_col0 + MLSTM_HEADS
    v_col0 = 2 * BRANCH_W // MLSTM_DV
    n_steps = (CTX_LEN + SEQ) // t
    return pl.pallas_call(
        _mlstm_kernel,
        grid=(2, BATCH * MLSTM_HEADS, n_steps),
        in_specs=[pl.BlockSpec((t, MLSTM_DQK), lambda d, bh, s: (_mlstm_chunk(d, bh, s), q_col0 + bh % MLSTM_HEADS)),
                  pl.BlockSpec((t, MLSTM_DQK), lambda d, bh, s: (_mlstm_chunk(d, bh, s), k_col0 + bh % MLSTM_HEADS)),
                  pl.BlockSpec((t, MLSTM_DV), lambda d, bh, s: (_mlstm_chunk(d, bh, s), v_col0 + bh % MLSTM_HEADS)),
                  pl.BlockSpec((N_MLSTM_GATES, t), lambda d, bh, s: (0, _mlstm_chunk(d, bh, s)))],
        out_specs=pl.BlockSpec((1, t, MLSTM_DV), lambda d, bh, s: (d, _mlstm_chunk(d, bh, s), bh % MLSTM_HEADS)),
        out_shape=jax.ShapeDtypeStruct((2, ROWS, BRANCH_W), jnp.float32),
        scratch_shapes=[pltpu.VMEM((MLSTM_DQK, MLSTM_DV), jnp.float32), pltpu.VMEM((1, MLSTM_DQK), jnp.float32),
                        pltpu.VMEM((1, 1), jnp.float32)],
        compiler_params=_params("parallel", "parallel", "arbitrary"),
        name="mlstm",
    )(proj, proj, proj, gates_t)


def _mlstm_out_kernel(hs_ref, og_ref, g_ref, o_ref):
    h = hs_ref[0] + hs_ref[1]
    for head in range(MLSTM_HEADS):
        cols = slice(head * MLSTM_DV, (head + 1) * MLSTM_DV)
        hh = h[:, cols]
        hn = hh * lax.rsqrt(jnp.mean(hh * hh, axis=-1, keepdims=True) + EPS)
        o_ref[:, cols] = (hn * g_ref[:, cols] * jax.nn.sigmoid(og_ref[:, cols])).astype(o_ref.dtype)


def mlstm_out(hs, proj, col_gate, norm_g, tm=SEQ_BLOCK):
    w = BRANCH_W
    return pl.pallas_call(
        _mlstm_out_kernel,
        grid=(ROWS // tm,),
        in_specs=[pl.BlockSpec((2, tm, w), lambda i: (0, i, 0)),
                  pl.BlockSpec((tm, w), lambda i: (i, col_gate)),
                  pl.BlockSpec((1, w), lambda i: (0, 0))],
        out_specs=pl.BlockSpec((tm, w), lambda i: (i, 0)),
        out_shape=jax.ShapeDtypeStruct((ROWS, w), jnp.bfloat16),
        compiler_params=_params("parallel"),
        name="mlstm_out",
    )(hs, proj, norm_g.reshape(1, w))


def kernel(x, c, ctx, c_ctx, w_mod, b_mod, g_norm1, g_norm2, w_in, w_branch, w_out, mlstm_gate_b, mlstm_norm_g,
           rg_conv_w, rg_conv_b, rg_wa, rg_ba, rg_wx, rg_bx, rg_lam, sc_conv_w, peer_wq, peer_keys, peer_u,
           peer_v, g_final):
    bf16 = jnp.bfloat16
    xs = jnp.concatenate([x.reshape(LAT_ROWS, D_MODEL), ctx.reshape(BATCH * CTX_LEN, D_MODEL)], axis=0)
    cond = jax.nn.silu(jnp.concatenate([c, c_ctx[None, :]], axis=0))
    tables = dft_tables()
    for l in range(DEPTH):
        last = l == DEPTH - 1
        modtab = (mm(cond, w_mod[l], tm=16, tn=2048, tk=2048) + b_mod[l]).reshape(BATCH + 1, 6, 1, D_MODEL)
        w_main = jnp.concatenate([w_in[l][:, :COL_MAIN], w_in[l][:, COL_GATES:COL_MERGE]], axis=1).astype(bf16)
        w_gates = jnp.pad(w_in[l][:, COL_MAIN:COL_GATES], ((0, 0), (0, LANES - N_MLSTM_GATES))).astype(bf16)
        w_merge = w_in[l][:, COL_MERGE:].astype(bf16)

        h = norm_mod(xs, g_norm1[l], modtab, MOD_SHIFT1, MOD_SCALE1)
        proj = mm(h, w_main, tk=D_MODEL)
        gates = mm(h, w_gates, tk=D_MODEL)
        y_four = fourier_mixer(proj, tables)
        gates_t = gates[:, :N_MLSTM_GATES].T + mlstm_gate_b[l].reshape(N_MLSTM_GATES, 1)
        y_ml = mlstm_out(mlstm_mixer(proj, gates_t), proj, 3, mlstm_norm_g[l])
        y_rg = rglru_mixer(proj, 4, 5, rg_conv_w[l], rg_conv_b[l], rg_wa[l], rg_ba[l], rg_wx[l], rg_bx[l], rg_lam[l])
        y_sc = sconv_mixer(proj, 6, 7, 8, sc_conv_w[l])
        ys = jnp.stack([y_four, y_ml, y_rg, y_sc], axis=0)
        if last:
            xs, h, ys = xs[:LAT_ROWS], h[:LAT_ROWS], ys[:, :LAT_ROWS]
        merged = merge_branches(h, w_merge, ys, w_branch[l].astype(bf16))
        xs = mm_resid(merged, w_out[l].astype(bf16), xs, modtab, MOD_GATE1)

        h2 = norm_mod(xs, g_norm2[l], modtab, MOD_SHIFT2, MOD_SCALE2)
        xs = peer_ffn(h2, xs, modtab, peer_wq[l].astype(bf16), peer_keys[l], peer_u[l].T.astype(bf16),
                      peer_v[l].astype(bf16))
    return rmsnorm_rows(xs, g_final).reshape(BATCH, SEQ, D_MODEL)
```

```python
import functools

import jax
import jax.numpy as jnp
from jax import lax
from jax.experimental import pallas as pl
from jax.experimental.pallas import tpu as pltpu

D_MODEL = 4096
BATCH = 2
SEQ = 4096
DEPTH = 2
CTX_LEN = 256
GRID_W = 64
N_BRANCHES = 4
BRANCH_W = D_MODEL // 4
FOURIER_GROUPS = 4
FOURIER_GW = BRANCH_W // FOURIER_GROUPS
MLSTM_HEADS = 4
MLSTM_DV = BRANCH_W // MLSTM_HEADS
MLSTM_DQK = MLSTM_DV // 2
MLSTM_CHUNK = 128
RG_BLOCKS = 8
RG_BW = BRANCH_W // RG_BLOCKS
RG_C = 8.0
RG_CONV_LEFT = 2
SC_CONV_LEFT = 1
PEER_HEADS = 8
PEER_NKEYS = 128
PEER_DK = 256
PEER_DKH = PEER_DK // 2
PEER_TOPK = 16
TOPK_SHIFT = PEER_TOPK.bit_length() - 1
EPS = 1e-6

N_MLSTM_GATES = 2 * 2 * MLSTM_HEADS
COL_MAIN = 4 * BRANCH_W
COL_GATES = COL_MAIN + N_MLSTM_GATES
COL_MERGE = COL_GATES + 5 * BRANCH_W

LAT_ROWS = BATCH * SEQ
ROWS = LAT_ROWS + BATCH * CTX_LEN
SEQ_BLOCK = CTX_LEN
LAT_BLOCKS = SEQ // SEQ_BLOCK
ROW_BLOCK = 512
SUBLANES = 8
LANES = 128
VMEM_LIMIT_BYTES = 48 * 1024 * 1024

MOD_SHIFT1, MOD_SCALE1, MOD_GATE1, MOD_SHIFT2, MOD_SCALE2, MOD_GATE2 = range(6)


def _round_up(x, m):
    return (x + m - 1) // m * m


def _params(*semantics):
    return pltpu.CompilerParams(dimension_semantics=semantics, vmem_limit_bytes=VMEM_LIMIT_BYTES)


def _segment(row_block, rows_per_block):
    return jnp.minimum(row_block // (SEQ // rows_per_block), BATCH)


def _mm_kernel(a_ref, b_ref, o_ref, acc_ref):
    @pl.when(pl.program_id(2) == 0)
    def _():
        acc_ref[...] = jnp.zeros_like(acc_ref)

    acc_ref[...] += jnp.dot(a_ref[...].astype(jnp.bfloat16), b_ref[...].astype(jnp.bfloat16),
                            preferred_element_type=jnp.float32)

    @pl.when(pl.program_id(2) == pl.num_programs(2) - 1)
    def _():
        o_ref[...] = acc_ref[...].astype(o_ref.dtype)


def _mm_fullk_kernel(a_ref, b_ref, o_ref):
    o_ref[...] = jnp.dot(a_ref[...].astype(jnp.bfloat16), b_ref[...].astype(jnp.bfloat16),
                         preferred_element_type=jnp.float32).astype(o_ref.dtype)


def mm(a, b, out_dtype=jnp.float32, tm=ROW_BLOCK, tn=1024, tk=2048):
    m, k = a.shape
    _, n = b.shape
    tm = min(tm, _round_up(m, 16))
    tn = min(tn, _round_up(n, LANES))
    tk = min(tk, k)
    mp, np_ = _round_up(m, tm), _round_up(n, tn)
    if mp != m:
        a = jnp.pad(a, ((0, mp - m), (0, 0)))
    if np_ != n:
        b = jnp.pad(b, ((0, 0), (0, np_ - n)))
    if tk == k:
        out = pl.pallas_call(
            _mm_fullk_kernel,
            grid=(mp // tm, np_ // tn),
            in_specs=[pl.BlockSpec((tm, k), lambda i, j: (i, 0)),
                      pl.BlockSpec((k, tn), lambda i, j: (0, j))],
            out_specs=pl.BlockSpec((tm, tn), lambda i, j: (i, j)),
            out_shape=jax.ShapeDtypeStruct((mp, np_), out_dtype),
            compiler_params=_params("parallel", "parallel"),
            name="mm_fullk",
        )(a, b)
    else:
        out = pl.pallas_call(
            _mm_kernel,
            grid=(mp // tm, np_ // tn, k // tk),
            in_specs=[pl.BlockSpec((tm, tk), lambda i, j, kk: (i, kk)),
                      pl.BlockSpec((tk, tn), lambda i, j, kk: (kk, j))],
            out_specs=pl.BlockSpec((tm, tn), lambda i, j, kk: (i, j)),
            out_shape=jax.ShapeDtypeStruct((mp, np_), out_dtype),
            scratch_shapes=[pltpu.VMEM((tm, tn), jnp.float32)],
            compiler_params=_params("parallel", "parallel", "arbitrary"),
            name="mm",
        )(a, b)
    if mp != m or np_ != n:
        out = out[:m, :n]
    return out


def _mm_resid_kernel(a_ref, b_ref, x_ref, g_ref, o_ref, acc_ref):
    @pl.when(pl.program_id(2) == 0)
    def _():
        acc_ref[...] = jnp.zeros_like(acc_ref)

    acc_ref[...] += jnp.dot(a_ref[...], b_ref[...], preferred_element_type=jnp.float32)

    @pl.when(pl.program_id(2) == pl.num_programs(2) - 1)
    def _():
        o_ref[...] = x_ref[...] + g_ref[0, 0] * acc_ref[...]


def mm_resid(a, b, x, modtab, which, tn=1024, tk=2048):
    m, k = a.shape
    _, n = b.shape
    tm = _row_tile(m)
    tn, tk = min(tn, n), min(tk, k)
    assert n % tn == 0 and k % tk == 0
    return pl.pallas_call(
        _mm_resid_kernel,
        grid=(m // tm, n // tn, k // tk),
        in_specs=[pl.BlockSpec((tm, tk), lambda i, j, kk: (i, kk)),
                  pl.BlockSpec((tk, tn), lambda i, j, kk: (kk, j)),
                  pl.BlockSpec((tm, tn), lambda i, j, kk: (i, j)),
                  pl.BlockSpec((1, 1, 1, tn), lambda i, j, kk: (_segment(i, tm), which, 0, j))],
        out_specs=pl.BlockSpec((tm, tn), lambda i, j, kk: (i, j)),
        out_shape=jax.ShapeDtypeStruct((m, n), jnp.float32),
        scratch_shapes=[pltpu.VMEM((tm, tn), jnp.float32)],
        compiler_params=_params("parallel", "parallel", "arbitrary"),
        name="mm_resid",
    )(a, b, x, modtab)


def _norm_mod_kernel(x_ref, g_ref, sh_ref, sc_ref, o_ref):
    x = x_ref[...]
    y = x * lax.rsqrt(jnp.mean(x * x, axis=-1, keepdims=True) + EPS)
    o_ref[...] = ((y * g_ref[...]) * (1.0 + sc_ref[0, 0]) + sh_ref[0, 0]).astype(o_ref.dtype)


def norm_mod(x, g, modtab, which_shift, which_scale, tm=SEQ_BLOCK):
    m, d = x.shape
    mod_spec = lambda which: pl.BlockSpec((1, 1, 1, d), lambda i: (_segment(i, tm), which, 0, 0))
    return pl.pallas_call(
        _norm_mod_kernel,
        grid=(m // tm,),
        in_specs=[pl.BlockSpec((tm, d), lambda i: (i, 0)),
                  pl.BlockSpec((1, d), lambda i: (0, 0)),
                  mod_spec(which_shift), mod_spec(which_scale)],
        out_specs=pl.BlockSpec((tm, d), lambda i: (i, 0)),
        out_shape=jax.ShapeDtypeStruct((m, d), jnp.bfloat16),
        compiler_params=_params("parallel"),
        name="norm_mod",
    )(x, g.reshape(1, d), modtab, modtab)


def _rmsnorm_kernel(x_ref, g_ref, o_ref):
    x = x_ref[...]
    o_ref[...] = x * lax.rsqrt(jnp.mean(x * x, axis=-1, keepdims=True) + EPS) * g_ref[...]


def rmsnorm_rows(x, g, tm=SEQ_BLOCK):
    m, d = x.shape
    return pl.pallas_call(
        _rmsnorm_kernel,
        grid=(m // tm,),
        in_specs=[pl.BlockSpec((tm, d), lambda i: (i, 0)), pl.BlockSpec((1, d), lambda i: (0, 0))],
        out_specs=pl.BlockSpec((tm, d), lambda i: (i, 0)),
        out_shape=jax.ShapeDtypeStruct((m, d), jnp.float32),
        compiler_params=_params("parallel"),
        name="rmsnorm",
    )(x, g.reshape(1, d))


def _masked_conv(u, w_ref, pad_l, is_ctx):
    rows = u.shape[0]
    t = lax.broadcasted_iota(jnp.int32, (rows, 1), 0)
    seg = jnp.where(is_ctx, rows, GRID_W)
    pos = t & (seg - 1)
    y = None
    for j in range(w_ref.shape[0]):
        k = j - pad_l
        if k == 0:
            sh = u
        else:
            sh = pltpu.roll(u, (-k) % rows, axis=0)
            sh = jnp.where((pos + k >= 0) & (pos + k < seg), sh, 0.0)
        term = w_ref[j:j + 1, :] * sh
        y = term if y is None else y + term
    return y


def _rglru_kernel(*refs, reverse):
    if reverse:
        (p6_ref, cw_ref, cb_ref, wa_ref, ba_ref, wx_ref, bx_ref, lam_ref, hf_ref, p7_ref,
         o_ref, a_sc, b_sc, h_sc) = refs
    else:
        p6_ref, cw_ref, cb_ref, wa_ref, ba_ref, wx_ref, bx_ref, lam_ref, o_ref, a_sc, b_sc, h_sc = refs
    s = pl.program_id(1)

    @pl.when(s == 0)
    def _():
        h_sc[...] = jnp.zeros_like(h_sc)

    u = _masked_conv(p6_ref[...], cw_ref, RG_CONV_LEFT, s == 0) + cb_ref[...]
    ub = u.astype(jnp.bfloat16)
    for g in range(RG_BLOCKS):
        cols = slice(g * RG_BW, (g + 1) * RG_BW)
        ug = ub[:, cols]
        r = jax.nn.sigmoid(jnp.dot(ug, wa_ref[0, g].astype(jnp.bfloat16), preferred_element_type=jnp.float32)
                           + ba_ref[:, cols])
        i = jax.nn.sigmoid(jnp.dot(ug, wx_ref[0, g].astype(jnp.bfloat16), preferred_element_type=jnp.float32)
                           + bx_ref[:, cols])
        neg_lam = -lam_ref[:, cols]
        softplus = jnp.log1p(jnp.exp(-jnp.abs(neg_lam))) + jnp.maximum(neg_lam, 0.0)
        log_a = (-RG_C * softplus) * r
        a_sc[:, cols] = jnp.exp(log_a)
        b_sc[:, cols] = jnp.sqrt(1.0 - jnp.exp(2.0 * log_a)) * (i * u[:, cols])

    n_groups = a_sc.shape[0] // SUBLANES
    row = lax.broadcasted_iota(jnp.int32, (SUBLANES, a_sc.shape[1]), 0)

    def body(it, h_prev):
        grp = (n_groups - 1 - it) if reverse else it
        off = pl.multiple_of(grp * SUBLANES, SUBLANES)
        a = a_sc[pl.ds(off, SUBLANES), :]
        b = b_sc[pl.ds(off, SUBLANES), :]
        for k in (1, 2, 4):
            shift = (SUBLANES - k) if reverse else k
            inside = (row < SUBLANES - k) if reverse else (row >= k)
            a_s = jnp.where(inside, pltpu.roll(a, shift, axis=0), 1.0)
            b_s = jnp.where(inside, pltpu.roll(b, shift, axis=0), 0.0)
            b = a * b_s + b
            a = a * a_s
        h = b + a * h_prev
        b_sc[pl.ds(off, SUBLANES), :] = h
        return h[0:1] if reverse else h[SUBLANES - 1:SUBLANES]

    h_sc[...] = lax.fori_loop(0, n_groups, body, h_sc[...])
    if reverse:
        o_ref[...] = (jax.nn.gelu(p7_ref[...]) * (hf_ref[...] + b_sc[...])).astype(o_ref.dtype)
    else:
        o_ref[...] = b_sc[...]


def _seq_block(b, s, reverse):
    lat = (LAT_BLOCKS - s) if reverse else (s - 1)
    return jnp.where(s == 0, BATCH * LAT_BLOCKS + b, b * LAT_BLOCKS + lat)


def rglru_mixer(proj, col_in, col_gate, conv_w, conv_b, wa, ba, wx, bx, lam):
    w = BRANCH_W
    row2 = lambda a: a.reshape(1, w)
    outs = None
    for reverse in (False, True):
        d = int(reverse)
        blk = lambda col: (lambda b, s: (_seq_block(b, s, reverse), col))
        const2 = lambda b, s: (0, 0)
        in_specs = [pl.BlockSpec((SEQ_BLOCK, w), blk(col_in)),
                    pl.BlockSpec(conv_w.shape, const2),
                    pl.BlockSpec((1, w), const2),
                    pl.BlockSpec((1,) + wa.shape[1:], lambda b, s: (d, 0, 0, 0)),
                    pl.BlockSpec((1, w), const2),
                    pl.BlockSpec((1,) + wx.shape[1:], lambda b, s: (d, 0, 0, 0)),
                    pl.BlockSpec((1, w), const2),
                    pl.BlockSpec((1, w), const2)]
        args = [proj, conv_w, row2(conv_b), wa, row2(ba[d]), wx, row2(bx[d]), row2(lam[d])]
        if reverse:
            in_specs += [pl.BlockSpec((SEQ_BLOCK, w), blk(0)), pl.BlockSpec((SEQ_BLOCK, w), blk(col_gate))]
            args += [outs, proj]
        outs = pl.pallas_call(
            functools.partial(_rglru_kernel, reverse=reverse),
            grid=(BATCH, LAT_BLOCKS + 1),
            in_specs=in_specs,
            out_specs=pl.BlockSpec((SEQ_BLOCK, w), blk(0)),
            out_shape=jax.ShapeDtypeStruct((ROWS, w), jnp.bfloat16 if reverse else jnp.float32),
            scratch_shapes=[pltpu.VMEM((SEQ_BLOCK, w), jnp.float32), pltpu.VMEM((SEQ_BLOCK, w), jnp.float32),
                            pltpu.VMEM((1, w), jnp.float32)],
            compiler_params=_params("parallel", "arbitrary"),
            name="rglru_bwd" if reverse else "rglru_fwd",
        )(*args)
    return outs


def _sconv_kernel(pb_ref, pc_ref, px_ref, w_ref, o_ref):
    is_ctx = pl.program_id(0) >= BATCH * LAT_BLOCKS
    conv = _masked_conv(pc_ref[...] * px_ref[...], w_ref, SC_CONV_LEFT, is_ctx)
    o_ref[...] = (pb_ref[...] * conv).astype(o_ref.dtype)


def sconv_mixer(proj, col_b, col_c, col_x, conv_w):
    w = BRANCH_W
    spec = lambda col: pl.BlockSpec((SEQ_BLOCK, w), lambda i: (i, col))
    return pl.pallas_call(
        _sconv_kernel,
        grid=(ROWS // SEQ_BLOCK,),
        in_specs=[spec(col_b), spec(col_c), spec(col_x), pl.BlockSpec(conv_w.shape, lambda i: (0, 0))],
        out_specs=pl.BlockSpec((SEQ_BLOCK, w), lambda i: (i, 0)),
        out_shape=jax.ShapeDtypeStruct((ROWS, w), jnp.bfloat16),
        compiler_params=_params("parallel"),
        name="sconv",
    )(proj, proj, proj, conv_w)


def _merge_kernel(h_ref, wg_ref, *rest):
    y_refs, (wb_ref, o_ref, acc_ref) = rest[:N_BRANCHES], rest[N_BRANCHES:]
    b = pl.program_id(2)
    gate = jax.nn.sigmoid(jnp.dot(h_ref[...], wg_ref[...], preferred_element_type=jnp.float32))
    for branch, y_ref in enumerate(y_refs):
        @pl.when(b == branch)
        def _(y_ref=y_ref, first=branch == 0):
            term = gate * jnp.dot(y_ref[...], wb_ref[0], preferred_element_type=jnp.float32)
            acc_ref[...] = term if first else acc_ref[...] + term

    @pl.when(b == N_BRANCHES - 1)
    def _():
        o_ref[...] = acc_ref[...].astype(o_ref.dtype)


def _row_tile(m):
    return 2 * ROW_BLOCK if m % (2 * ROW_BLOCK) == 0 else ROW_BLOCK


def merge_branches(h, w_gate, ys, w_branch, m):
    d = h.shape[1]
    bw = ys[0].shape[1]
    tm = _row_tile(m)
    tn = 1024 * ROW_BLOCK // tm
    n_col = d // tn
    y_spec = pl.BlockSpec((tm, bw), lambda i, j, b: (i, 0))
    return pl.pallas_call(
        _merge_kernel,
        grid=(m // tm, n_col, N_BRANCHES),
        in_specs=[pl.BlockSpec((tm, d), lambda i, j, b: (i, 0)),
                  pl.BlockSpec((d, tn), lambda i, j, b: (0, b * n_col + j))]
                 + [y_spec] * N_BRANCHES
                 + [pl.BlockSpec((1, bw, tn), lambda i, j, b: (b, 0, j))],
        out_specs=pl.BlockSpec((tm, tn), lambda i, j, b: (i, j)),
        out_shape=jax.ShapeDtypeStruct((m, d), jnp.bfloat16),
        scratch_shapes=[pltpu.VMEM((tm, tn), jnp.float32)],
        compiler_params=_params("parallel", "parallel", "arbitrary"),
        name="merge_branches",
    )(h, w_gate, *ys, w_branch)


def _extract_topk(s, n_top, val_ref, idx_ref, slot):
    n_rows = s.shape[0]
    rid = lax.broadcasted_iota(jnp.int32, s.shape, 0).astype(jnp.float32)
    for r in range(n_top):
        m = jnp.max(s, axis=0, keepdims=True)
        am = jnp.min(jnp.where(s == m, rid, float(n_rows)), axis=0, keepdims=True)
        val_ref[slot, r:r + 1, :] = m
        idx_ref[slot, r:r + 1, :] = am
        s = jnp.where(rid == am, -jnp.inf, s)


def _lookup_rows(table, sel):
    out = jnp.zeros(sel.shape, table.dtype)
    for r in range(table.shape[0]):
        out = jnp.where(sel == r, table[r:r + 1, :], out)
    return out


def _peer_topk_kernel(q_ref, keys_ref, i1_ref, i2_ref, w_ref, val_sc, idx_sc, cand_sc, top_sc, pos_sc, ent_sc):
    n_half = keys_ref.shape[0]
    for hp in range(n_half):
        q = q_ref[:, hp * PEER_DKH:(hp + 1) * PEER_DKH]
        s = lax.dot_general(keys_ref[hp], q, (((1,), (1,)), ((), ())), precision=lax.Precision.HIGHEST,
                            preferred_element_type=jnp.float32)
        _extract_topk(s, PEER_TOPK, val_sc, idx_sc, hp)
    for h in range(PEER_HEADS):
        v1, v2 = val_sc[2 * h], val_sc[2 * h + 1]
        for j1 in range(PEER_TOPK):
            cand_sc[j1 * PEER_TOPK:(j1 + 1) * PEER_TOPK, :] = v1[j1:j1 + 1, :] + v2
        _extract_topk(cand_sc[...], PEER_TOPK, top_sc, pos_sc, 0)
        top, pos = top_sc[0], pos_sc[0].astype(jnp.int32)
        e = jnp.exp(top - top[0:1, :])
        rows = slice(h * PEER_TOPK, (h + 1) * PEER_TOPK)
        ent_sc[0, rows, :] = _lookup_rows(idx_sc[2 * h], pos >> TOPK_SHIFT)
        ent_sc[1, rows, :] = _lookup_rows(idx_sc[2 * h + 1], pos & (PEER_TOPK - 1))
        ent_sc[2, rows, :] = e / jnp.sum(e, axis=0, keepdims=True)
    i1_ref[...] = ent_sc[0].T.astype(jnp.int32)
    i2_ref[...] = ent_sc[1].T.astype(jnp.int32)
    w_ref[...] = ent_sc[2].T


def peer_topk(q, keys, tt=LANES):
    n_tok = q.shape[0]
    n_ent = PEER_HEADS * PEER_TOPK
    assert n_ent == tt
    ent_spec = pl.BlockSpec((tt, n_ent), lambda i: (i, 0))
    f32, i32 = jnp.float32, jnp.int32
    return pl.pallas_call(
        _peer_topk_kernel,
        grid=(n_tok // tt,),
        in_specs=[pl.BlockSpec((tt, q.shape[1]), lambda i: (i, 0)),
                  pl.BlockSpec(keys.shape, lambda i: (0, 0, 0))],
        out_specs=[ent_spec, ent_spec, ent_spec],
        out_shape=[jax.ShapeDtypeStruct((n_tok, n_ent), i32), jax.ShapeDtypeStruct((n_tok, n_ent), i32),
                   jax.ShapeDtypeStruct((n_tok, n_ent), f32)],
        scratch_shapes=[pltpu.VMEM((2 * PEER_HEADS, PEER_TOPK, tt), f32), pltpu.VMEM((2 * PEER_HEADS, PEER_TOPK, tt), f32),
                        pltpu.VMEM((PEER_TOPK * PEER_TOPK, tt), f32),
                        pltpu.VMEM((1, PEER_TOPK, tt), f32), pltpu.VMEM((1, PEER_TOPK, tt), f32),
                        pltpu.VMEM((3, n_ent, tt), f32)],
        compiler_params=_params("parallel"),
        name="peer_topk",
    )(q, keys)


def _peer_score_kernel(h_ref, u_ref, i1_ref, i2_ref, o_ref):
    j = pl.program_id(1)

    @pl.when(j == 0)
    def _():
        o_ref[...] = jnp.zeros_like(o_ref)

    s = lax.dot_general(h_ref[...], u_ref[...], (((1,), (1,)), ((), ())), preferred_element_type=jnp.float32)
    i1, i2 = i1_ref[...], i2_ref[...]
    acc = o_ref[...]
    n_chunks = s.shape[1] // PEER_NKEYS
    for c in range(n_chunks):
        picked = jnp.take_along_axis(s[:, c * PEER_NKEYS:(c + 1) * PEER_NKEYS], i2, axis=1)
        acc = jnp.where(i1 == j * n_chunks + c, picked, acc)
    o_ref[...] = acc


def peer_scores(h, u_tab, i1, i2, tn=1024):
    n_tok, d = h.shape
    n_exp = u_tab.shape[0]
    n_ent = i1.shape[1]
    tm = _row_tile(n_tok)
    assert n_exp % tn == 0 and n_ent == PEER_NKEYS
    ent_spec = pl.BlockSpec((tm, n_ent), lambda i, j: (i, 0))
    return pl.pallas_call(
        _peer_score_kernel,
        grid=(n_tok // tm, n_exp // tn),
        in_specs=[pl.BlockSpec((tm, d), lambda i, j: (i, 0)),
                  pl.BlockSpec((tn, d), lambda i, j: (j, 0)),
                  ent_spec, ent_spec],
        out_specs=ent_spec,
        out_shape=jax.ShapeDtypeStruct((n_tok, n_ent), jnp.float32),
        compiler_params=_params("parallel", "arbitrary"),
        name="peer_scores",
    )(h, u_tab, i1, i2)


def _peer_coef_kernel(sc_ref, w_ref, i1_ref, i2_ref, o_ref, wa_sc):
    wa_sc[...] = w_ref[...] * jax.nn.gelu(sc_ref[...])
    n_keys = o_ref.shape[1]
    n_ent = sc_ref.shape[1]
    key = lax.broadcasted_iota(jnp.int32, (n_keys, n_ent), 0)

    def body(t, carry):
        row = lambda ref: jnp.broadcast_to(ref[pl.ds(t, 1), :], (n_keys, n_ent))
        at = jnp.where(key == row(i1_ref), row(wa_sc), 0.0).astype(jnp.bfloat16)
        bt = jnp.where(key == row(i2_ref), 1.0, 0.0).astype(jnp.bfloat16)
        ct = lax.dot_general(at, bt, (((1,), (1,)), ((), ())), preferred_element_type=jnp.float32)
        o_ref[t] = ct.astype(o_ref.dtype)
        return carry

    lax.fori_loop(0, sc_ref.shape[0], body, 0, unroll=4)


def peer_coef(sc, wts, i1, i2, tb=128):
    n_tok, n_ent = sc.shape
    tb = min(tb, n_tok)
    assert n_tok % tb == 0
    ent_spec = pl.BlockSpec((tb, n_ent), lambda i: (i, 0))
    return pl.pallas_call(
        _peer_coef_kernel,
        grid=(n_tok // tb,),
        in_specs=[ent_spec] * 4,
        out_specs=pl.BlockSpec((tb, PEER_NKEYS, PEER_NKEYS), lambda i: (i, 0, 0)),
        out_shape=jax.ShapeDtypeStruct((n_tok, PEER_NKEYS, PEER_NKEYS), jnp.bfloat16),
        scratch_shapes=[pltpu.VMEM((tb, n_ent), jnp.float32)],
        compiler_params=_params("parallel"),
        name="peer_coef",
    )(sc, wts, i1, i2)


def peer_ffn(h, x, modtab, w_q, keys, u_tab, v_tab):
    q = mm(h, w_q, tk=D_MODEL)
    i1, i2, wts = peer_topk(q, keys.reshape(2 * PEER_HEADS, PEER_NKEYS, PEER_DKH))
    sc = peer_scores(h, u_tab, i1, i2)
    coef = peer_coef(sc, wts, i1, i2).reshape(h.shape[0], PEER_NKEYS * PEER_NKEYS)
    return mm_resid(coef, v_tab, x, modtab, MOD_GATE2)


def _dft_parts(n):
    idx = jnp.arange(n, dtype=jnp.int32)
    ang = ((idx[:, None] * idx[None, :]) % n).astype(jnp.float32) * (2.0 * jnp.pi / n)
    scale = n ** -0.5
    return jnp.cos(ang) * scale, jnp.sin(ang) * scale


def dft_tables():
    cc, sc = _dft_parts(FOURIER_GW)
    ct_l, st_l = _dft_parts(SEQ)
    ct_c, st_c = _dft_parts(CTX_LEN)
    bf16 = jnp.bfloat16
    return (jnp.concatenate([cc, sc], axis=1).astype(bf16),
            jnp.concatenate([ct_l, -st_l], axis=1).astype(bf16),
            jnp.concatenate([ct_c, -st_c], axis=1).astype(bf16))


def _fourier_kernel(p_ref, chan_ref, pos_ref, *rest):
    o_ref, gcs_sc = rest[-2:]
    seq = p_ref.shape[0]
    n_r = seq // o_ref.shape[0]
    r = pl.program_id(2)

    @pl.when(r == 0)
    def _():
        gc = jnp.dot(p_ref[...].astype(jnp.bfloat16), chan_ref[...], preferred_element_type=jnp.float32)
        gcs_sc[0:seq, :] = gc[:, :FOURIER_GW].astype(gcs_sc.dtype)
        gcs_sc[seq:2 * seq, :] = gc[:, FOURIER_GW:].astype(gcs_sc.dtype)

    @pl.when(r < n_r)
    def _():
        o_ref[...] = jnp.dot(pos_ref[...], gcs_sc[...], preferred_element_type=jnp.float32).astype(o_ref.dtype)

    if len(rest) == 3:
        @pl.when(r == n_r)
        def _():
            o_ref[...] = rest[0][...]


def fourier_mixer(proj, tables, tr=ROW_BLOCK):
    chan, pos_lat, pos_ctx = tables
    gw = FOURIER_GW
    ctx_rows = BATCH * CTX_LEN
    assert ctx_rows == tr
    n_r = SEQ // tr
    chan_spec = pl.BlockSpec(chan.shape, lambda b, g, r: (0, 0))
    y_ctx = pl.pallas_call(
        _fourier_kernel,
        grid=(BATCH, FOURIER_GROUPS, 1),
        in_specs=[pl.BlockSpec((CTX_LEN, gw), lambda b, g, r: (LAT_ROWS // CTX_LEN + b, g)), chan_spec,
                  pl.BlockSpec(pos_ctx.shape, lambda b, g, r: (0, 0))],
        out_specs=pl.BlockSpec((CTX_LEN, gw), lambda b, g, r: (b, g)),
        out_shape=jax.ShapeDtypeStruct((ctx_rows, BRANCH_W), jnp.bfloat16),
        scratch_shapes=[pltpu.VMEM((2 * CTX_LEN, gw), jnp.bfloat16)],
        compiler_params=_params("parallel", "parallel", "arbitrary"),
        name="fourier_ctx",
    )(proj, chan, pos_ctx)
    return pl.pallas_call(
        _fourier_kernel,
        grid=(BATCH, FOURIER_GROUPS, n_r + 1),
        in_specs=[pl.BlockSpec((SEQ, gw), lambda b, g, r: (b, g)), chan_spec,
                  pl.BlockSpec((tr, 2 * SEQ), lambda b, g, r: (jnp.minimum(r, n_r - 1), 0)),
                  pl.BlockSpec((ctx_rows, gw), lambda b, g, r: (0, g))],
        out_specs=pl.BlockSpec((tr, gw), lambda b, g, r: (jnp.where(r == n_r, BATCH * n_r, b * n_r + r), g)),
        out_shape=jax.ShapeDtypeStruct((ROWS, BRANCH_W), jnp.bfloat16),
        scratch_shapes=[pltpu.VMEM((2 * SEQ, gw), jnp.bfloat16)],
        compiler_params=_params("arbitrary", "arbitrary", "arbitrary"),
        name="fourier_lat",
    )(proj, chan, pos_lat, y_ctx)


def _mlstm_kernel(q_ref, k_ref, v_ref, gt_ref, o_ref, ct_sc, n_sc, m_sc):
    d = pl.program_id(0)
    head = pl.program_id(1) % MLSTM_HEADS
    f32, bf16 = jnp.float32, jnp.bfloat16
    n_t = q_ref.shape[0]

    @pl.when(pl.program_id(2) == 0)
    def _():
        ct_sc[...] = jnp.zeros_like(ct_sc)
        n_sc[...] = jnp.zeros_like(n_sc)
        m_sc[...] = jnp.zeros_like(m_sc)

    q = q_ref[...] * (MLSTM_DQK ** -0.5)
    k = k_ref[...]
    v = v_ref[...]
    gate0 = d * (2 * MLSTM_HEADS) + head
    li = gt_ref[pl.ds(gate0, 1), :]
    gf = gt_ref[pl.ds(gate0 + MLSTM_HEADS, 1), :]
    lf = -(jnp.log1p(jnp.exp(-jnp.abs(gf))) + jnp.maximum(-gf, 0.0))

    r = lax.broadcasted_iota(jnp.int32, (n_t, n_t), 0)
    c = lax.broadcasted_iota(jnp.int32, (n_t, n_t), 1)
    upto = (c - r) * (1 - 2 * d) <= 0
    eye = r == c
    b_col = jnp.sum(jnp.where(upto, lf, 0.0), axis=1, keepdims=True)
    b_row = jnp.sum(jnp.where(eye, b_col, 0.0), axis=0, keepdims=True)
    m_prev = m_sc[...]
    logw = jnp.where(upto, b_col - b_row + li, -jnp.inf)
    g_col = b_col + m_prev
    mt = jnp.maximum(g_col, jnp.max(logw, axis=1, keepdims=True))
    qb = q.astype(bf16)
    s = lax.dot_general(qb, k.astype(bf16), (((1,), (1,)), ((), ())), preferred_element_type=f32) * jnp.exp(logw - mt)
    w_inter = jnp.exp(g_col - mt)
    num = (jnp.dot(s.astype(bf16), v.astype(bf16), preferred_element_type=f32)
           + w_inter * jnp.dot(qb, ct_sc[...].astype(bf16), preferred_element_type=f32))
    den = jnp.sum(s, axis=1, keepdims=True) + w_inter * jnp.sum(q * n_sc[...], axis=1, keepdims=True)
    o_ref[0] = num / jnp.maximum(jnp.abs(den), jnp.exp(-mt))

    total = jnp.sum(lf, axis=1, keepdims=True)
    logu = total - b_row + li
    m_new = jnp.maximum(total + m_prev, jnp.max(logu, axis=1, keepdims=True))
    ws_row = jnp.exp(logu - m_new)
    wc = jnp.exp(total + m_prev - m_new)
    ws_col = jnp.sum(jnp.where(eye, ws_row, 0.0), axis=1, keepdims=True)
    kv = lax.dot_general(k.astype(bf16), (ws_col * v).astype(bf16), (((0,), (0,)), ((), ())),
                         preferred_element_type=f32)
    ct_sc[...] = wc * ct_sc[...] + kv
    n_sc[...] = wc * n_sc[...] + jnp.sum(ws_col * k, axis=0, keepdims=True)
    m_sc[...] = m_new


def _mlstm_chunk(d, bh, s):
    b = bh // MLSTM_HEADS
    n_ctx, n_lat = CTX_LEN // MLSTM_CHUNK, SEQ // MLSTM_CHUNK
    ctx_j = jnp.where(d == 0, s, n_ctx - 1 - s)
    lat_j = jnp.where(d == 0, s - n_ctx, n_ctx + n_lat - 1 - s)
    return jnp.where(s < n_ctx, BATCH * n_lat + b * n_ctx + ctx_j, b * n_lat + lat_j)


def mlstm_mixer(proj, gates_t):
    t = MLSTM_CHUNK
    q_col0 = BRANCH_W // MLSTM_DQK
    k_col0 = q_col0 + MLSTM_HEADS
    v_col0 = 2 * BRANCH_W // MLSTM_DV
    n_steps = (CTX_LEN + SEQ) // t
    return pl.pallas_call(
        _mlstm_kernel,
        grid=(2, BATCH * MLSTM_HEADS, n_steps),
        in_specs=[pl.BlockSpec((t, MLSTM_DQK), lambda d, bh, s: (_mlstm_chunk(d, bh, s), q_col0 + bh % MLSTM_HEADS)),
                  pl.BlockSpec((t, MLSTM_DQK), lambda d, bh, s: (_mlstm_chunk(d, bh, s), k_col0 + bh % MLSTM_HEADS)),
                  pl.BlockSpec((t, MLSTM_DV), lambda d, bh, s: (_mlstm_chunk(d, bh, s), v_col0 + bh % MLSTM_HEADS)),
                  pl.BlockSpec((N_MLSTM_GATES, t), lambda d, bh, s: (0, _mlstm_chunk(d, bh, s)))],
        out_specs=pl.BlockSpec((1, t, MLSTM_DV), lambda d, bh, s: (d, _mlstm_chunk(d, bh, s), bh % MLSTM_HEADS)),
        out_shape=jax.ShapeDtypeStruct((2, ROWS, BRANCH_W), jnp.float32),
        scratch_shapes=[pltpu.VMEM((MLSTM_DQK, MLSTM_DV), jnp.float32), pltpu.VMEM((1, MLSTM_DQK), jnp.float32),
                        pltpu.VMEM((1, 1), jnp.float32)],
        compiler_params=_params("parallel", "parallel", "arbitrary"),
        name="mlstm",
    )(proj, proj, proj, gates_t)


def _mlstm_out_kernel(hs_ref, og_ref, g_ref, o_ref):
    h = hs_ref[0] + hs_ref[1]
    for head in range(MLSTM_HEADS):
        cols = slice(head * MLSTM_DV, (head + 1) * MLSTM_DV)
        hh = h[:, cols]
        hn = hh * lax.rsqrt(jnp.mean(hh * hh, axis=-1, keepdims=True) + EPS)
        o_ref[:, cols] = (hn * g_ref[:, cols] * jax.nn.sigmoid(og_ref[:, cols])).astype(o_ref.dtype)


def mlstm_out(hs, proj, col_gate, norm_g, tm=SEQ_BLOCK):
    w = BRANCH_W
    return pl.pallas_call(
        _mlstm_out_kernel,
        grid=(ROWS // tm,),
        in_specs=[pl.BlockSpec((2, tm, w), lambda i: (0, i, 0)),
                  pl.BlockSpec((tm, w), lambda i: (i, col_gate)),
                  pl.BlockSpec((1, w), lambda i: (0, 0))],
        out_specs=pl.BlockSpec((tm, w), lambda i: (i, 0)),
        out_shape=jax.ShapeDtypeStruct((ROWS, w), jnp.bfloat16),
        compiler_params=_params("parallel"),
        name="mlstm_out",
    )(hs, proj, norm_g.reshape(1, w))


def kernel(x, c, ctx, c_ctx, w_mod, b_mod, g_norm1, g_norm2, w_in, w_branch, w_out, mlstm_gate_b, mlstm_norm_g,
           rg_conv_w, rg_conv_b, rg_wa, rg_ba, rg_wx, rg_bx, rg_lam, sc_conv_w, peer_wq, peer_keys, peer_u,
           peer_v, g_final):
    bf16 = jnp.bfloat16
    xs = jnp.concatenate([x.reshape(LAT_ROWS, D_MODEL), ctx.reshape(BATCH * CTX_LEN, D_MODEL)], axis=0)
    cond = jax.nn.silu(jnp.concatenate([c, c_ctx[None, :]], axis=0))
    tables = dft_tables()
    for l in range(DEPTH):
        last = l == DEPTH - 1
        modtab = (mm(cond, w_mod[l], tm=16, tn=2048, tk=2048) + b_mod[l]).reshape(BATCH + 1, 6, 1, D_MODEL)
        w_main = jnp.concatenate([w_in[l][:, :COL_MAIN], w_in[l][:, COL_GATES:COL_MERGE]], axis=1).astype(bf16)
        w_gates = jnp.pad(w_in[l][:, COL_MAIN:COL_GATES], ((0, 0), (0, LANES - N_MLSTM_GATES))).astype(bf16)
        w_merge = w_in[l][:, COL_MERGE:].astype(bf16)

        h = norm_mod(xs, g_norm1[l], modtab, MOD_SHIFT1, MOD_SCALE1)
        proj = mm(h, w_main, tk=D_MODEL)
        gates = mm(h, w_gates, tk=D_MODEL)
        y_four = fourier_mixer(proj, tables)
        gates_t = gates[:, :N_MLSTM_GATES].T + mlstm_gate_b[l].reshape(N_MLSTM_GATES, 1)
        y_ml = mlstm_out(mlstm_mixer(proj, gates_t), proj, 3, mlstm_norm_g[l])
        y_rg = rglru_mixer(proj, 4, 5, rg_conv_w[l], rg_conv_b[l], rg_wa[l], rg_ba[l], rg_wx[l], rg_bx[l], rg_lam[l])
        y_sc = sconv_mixer(proj, 6, 7, 8, sc_conv_w[l])
        m = LAT_ROWS if last else ROWS
        merged = merge_branches(h, w_merge, (y_four, y_ml, y_rg, y_sc), w_branch[l].astype(bf16), m)
        xs = mm_resid(merged, w_out[l].astype(bf16), xs, modtab, MOD_GATE1)

        h2 = norm_mod(xs, g_norm2[l], modtab, MOD_SHIFT2, MOD_SCALE2)
        xs = peer_ffn(h2, xs, modtab, peer_wq[l].astype(bf16), peer_keys[l], peer_u[l].astype(bf16),
                      peer_v[l].astype(bf16))
    return rmsnorm_rows(xs, g_final).reshape(BATCH, SEQ, D_MODEL)
```

```python
import functools

import jax
import jax.numpy as jnp
from jax import lax
from jax.experimental import pallas as pl
from jax.experimental.pallas import tpu as pltpu

D_MODEL = 4096
BATCH = 2
SEQ = 4096
DEPTH = 2
CTX_LEN = 256
GRID_W = 64
N_BRANCHES = 4
BRANCH_W = D_MODEL // 4
FOURIER_GROUPS = 4
FOURIER_GW = BRANCH_W // FOURIER_GROUPS
MLSTM_HEADS = 4
MLSTM_DV = BRANCH_W // MLSTM_HEADS
MLSTM_DQK = MLSTM_DV // 2
MLSTM_CHUNK = 128
RG_BLOCKS = 8
RG_BW = BRANCH_W // RG_BLOCKS
RG_C = 8.0
RG_CONV_LEFT = 2
SC_CONV_LEFT = 1
PEER_HEADS = 8
PEER_NKEYS = 128
PEER_DK = 256
PEER_DKH = PEER_DK // 2
PEER_TOPK = 16
TOPK_SHIFT = PEER_TOPK.bit_length() - 1
EPS = 1e-6

N_MLSTM_GATES = 2 * 2 * MLSTM_HEADS
COL_MAIN = 4 * BRANCH_W
COL_GATES = COL_MAIN + N_MLSTM_GATES
COL_MERGE = COL_GATES + 5 * BRANCH_W

LAT_ROWS = BATCH * SEQ
ROWS = LAT_ROWS + BATCH * CTX_LEN
SEQ_BLOCK = CTX_LEN
LAT_BLOCKS = SEQ // SEQ_BLOCK
ROW_BLOCK = 512
SUBLANES = 8
LANES = 128
VMEM_LIMIT_BYTES = 48 * 1024 * 1024

MOD_SHIFT1, MOD_SCALE1, MOD_GATE1, MOD_SHIFT2, MOD_SCALE2, MOD_GATE2 = range(6)


def _round_up(x, m):
    return (x + m - 1) // m * m


def _params(*semantics):
    return pltpu.CompilerParams(dimension_semantics=semantics, vmem_limit_bytes=VMEM_LIMIT_BYTES)


def _segment(row_block, rows_per_block):
    return jnp.minimum(row_block // (SEQ // rows_per_block), BATCH)


def _mm_kernel(a_ref, b_ref, o_ref, acc_ref):
    @pl.when(pl.program_id(2) == 0)
    def _():
        acc_ref[...] = jnp.zeros_like(acc_ref)

    acc_ref[...] += jnp.dot(a_ref[...].astype(jnp.bfloat16), b_ref[...].astype(jnp.bfloat16),
                            preferred_element_type=jnp.float32)

    @pl.when(pl.program_id(2) == pl.num_programs(2) - 1)
    def _():
        o_ref[...] = acc_ref[...].astype(o_ref.dtype)


def _mm_fullk_kernel(a_ref, b_ref, o_ref):
    o_ref[...] = jnp.dot(a_ref[...].astype(jnp.bfloat16), b_ref[...].astype(jnp.bfloat16),
                         preferred_element_type=jnp.float32).astype(o_ref.dtype)


def mm(a, b, out_dtype=jnp.float32, tm=ROW_BLOCK, tn=1024, tk=2048):
    m, k = a.shape
    _, n = b.shape
    tm = min(tm, _round_up(m, 16))
    tn = min(tn, _round_up(n, LANES))
    tk = min(tk, k)
    mp, np_ = _round_up(m, tm), _round_up(n, tn)
    if mp != m:
        a = jnp.pad(a, ((0, mp - m), (0, 0)))
    if np_ != n:
        b = jnp.pad(b, ((0, 0), (0, np_ - n)))
    if tk == k:
        out = pl.pallas_call(
            _mm_fullk_kernel,
            grid=(mp // tm, np_ // tn),
            in_specs=[pl.BlockSpec((tm, k), lambda i, j: (i, 0)),
                      pl.BlockSpec((k, tn), lambda i, j: (0, j))],
            out_specs=pl.BlockSpec((tm, tn), lambda i, j: (i, j)),
            out_shape=jax.ShapeDtypeStruct((mp, np_), out_dtype),
            compiler_params=_params("parallel", "parallel"),
            name="mm_fullk",
        )(a, b)
    else:
        out = pl.pallas_call(
            _mm_kernel,
            grid=(mp // tm, np_ // tn, k // tk),
            in_specs=[pl.BlockSpec((tm, tk), lambda i, j, kk: (i, kk)),
                      pl.BlockSpec((tk, tn), lambda i, j, kk: (kk, j))],
            out_specs=pl.BlockSpec((tm, tn), lambda i, j, kk: (i, j)),
            out_shape=jax.ShapeDtypeStruct((mp, np_), out_dtype),
            scratch_shapes=[pltpu.VMEM((tm, tn), jnp.float32)],
            compiler_params=_params("parallel", "parallel", "arbitrary"),
            name="mm",
        )(a, b)
    if mp != m or np_ != n:
        out = out[:m, :n]
    return out


def _mm_resid_kernel(a_ref, b_ref, x_ref, g_ref, o_ref, acc_ref):
    @pl.when(pl.program_id(2) == 0)
    def _():
        acc_ref[...] = jnp.zeros_like(acc_ref)

    acc_ref[...] += jnp.dot(a_ref[...], b_ref[...], preferred_element_type=jnp.float32)

    @pl.when(pl.program_id(2) == pl.num_programs(2) - 1)
    def _():
        o_ref[...] = x_ref[...] + g_ref[0, 0] * acc_ref[...]


def mm_resid(a, b, x, modtab, which, tn=1024, tk=2048):
    m, k = a.shape
    _, n = b.shape
    tm = _row_tile(m)
    tn, tk = min(tn, n), min(tk, k)
    assert n % tn == 0 and k % tk == 0
    return pl.pallas_call(
        _mm_resid_kernel,
        grid=(m // tm, n // tn, k // tk),
        in_specs=[pl.BlockSpec((tm, tk), lambda i, j, kk: (i, kk)),
                  pl.BlockSpec((tk, tn), lambda i, j, kk: (kk, j)),
                  pl.BlockSpec((tm, tn), lambda i, j, kk: (i, j)),
                  pl.BlockSpec((1, 1, 1, tn), lambda i, j, kk: (_segment(i, tm), which, 0, j))],
        out_specs=pl.BlockSpec((tm, tn), lambda i, j, kk: (i, j)),
        out_shape=jax.ShapeDtypeStruct((m, n), jnp.float32),
        scratch_shapes=[pltpu.VMEM((tm, tn), jnp.float32)],
        compiler_params=_params("parallel", "parallel", "arbitrary"),
        name="mm_resid",
    )(a, b, x, modtab)


def _norm_mod_kernel(x_ref, g_ref, sh_ref, sc_ref, o_ref):
    x = x_ref[...]
    y = x * lax.rsqrt(jnp.mean(x * x, axis=-1, keepdims=True) + EPS)
    o_ref[...] = ((y * g_ref[...]) * (1.0 + sc_ref[0, 0]) + sh_ref[0, 0]).astype(o_ref.dtype)


def norm_mod(x, g, modtab, which_shift, which_scale, tm=SEQ_BLOCK):
    m, d = x.shape
    mod_spec = lambda which: pl.BlockSpec((1, 1, 1, d), lambda i: (_segment(i, tm), which, 0, 0))
    return pl.pallas_call(
        _norm_mod_kernel,
        grid=(m // tm,),
        in_specs=[pl.BlockSpec((tm, d), lambda i: (i, 0)),
                  pl.BlockSpec((1, d), lambda i: (0, 0)),
                  mod_spec(which_shift), mod_spec(which_scale)],
        out_specs=pl.BlockSpec((tm, d), lambda i: (i, 0)),
        out_shape=jax.ShapeDtypeStruct((m, d), jnp.bfloat16),
        compiler_params=_params("parallel"),
        name="norm_mod",
    )(x, g.reshape(1, d), modtab, modtab)


def _rmsnorm_kernel(x_ref, g_ref, o_ref):
    x = x_ref[...]
    o_ref[...] = x * lax.rsqrt(jnp.mean(x * x, axis=-1, keepdims=True) + EPS) * g_ref[...]


def rmsnorm_rows(x, g, tm=SEQ_BLOCK):
    m, d = x.shape
    return pl.pallas_call(
        _rmsnorm_kernel,
        grid=(m // tm,),
        in_specs=[pl.BlockSpec((tm, d), lambda i: (i, 0)), pl.BlockSpec((1, d), lambda i: (0, 0))],
        out_specs=pl.BlockSpec((tm, d), lambda i: (i, 0)),
        out_shape=jax.ShapeDtypeStruct((m, d), jnp.float32),
        compiler_params=_params("parallel"),
        name="rmsnorm",
    )(x, g.reshape(1, d))


def _masked_conv(u, w_ref, pad_l, is_ctx):
    rows = u.shape[0]
    t = lax.broadcasted_iota(jnp.int32, (rows, 1), 0)
    seg = jnp.where(is_ctx, rows, GRID_W)
    pos = t & (seg - 1)
    y = None
    for j in range(w_ref.shape[0]):
        k = j - pad_l
        if k == 0:
            sh = u
        else:
            sh = pltpu.roll(u, (-k) % rows, axis=0)
            sh = jnp.where((pos + k >= 0) & (pos + k < seg), sh, 0.0)
        term = w_ref[j:j + 1, :] * sh
        y = term if y is None else y + term
    return y


def _rglru_kernel(*refs, reverse):
    if reverse:
        (p6_ref, cw_ref, cb_ref, wa_ref, ba_ref, wx_ref, bx_ref, lam_ref, hf_ref, p7_ref,
         o_ref, a_sc, b_sc, h_sc) = refs
    else:
        p6_ref, cw_ref, cb_ref, wa_ref, ba_ref, wx_ref, bx_ref, lam_ref, o_ref, a_sc, b_sc, h_sc = refs
    s = pl.program_id(1)

    @pl.when(s == 0)
    def _():
        h_sc[...] = jnp.zeros_like(h_sc)

    u = _masked_conv(p6_ref[...], cw_ref, RG_CONV_LEFT, s == 0) + cb_ref[...]
    ub = u.astype(jnp.bfloat16)
    for g in range(RG_BLOCKS):
        cols = slice(g * RG_BW, (g + 1) * RG_BW)
        ug = ub[:, cols]
        r = jax.nn.sigmoid(jnp.dot(ug, wa_ref[0, g].astype(jnp.bfloat16), preferred_element_type=jnp.float32)
                           + ba_ref[:, cols])
        i = jax.nn.sigmoid(jnp.dot(ug, wx_ref[0, g].astype(jnp.bfloat16), preferred_element_type=jnp.float32)
                           + bx_ref[:, cols])
        neg_lam = -lam_ref[:, cols]
        softplus = jnp.log1p(jnp.exp(-jnp.abs(neg_lam))) + jnp.maximum(neg_lam, 0.0)
        log_a = (-RG_C * softplus) * r
        a_sc[:, cols] = jnp.exp(log_a)
        b_sc[:, cols] = jnp.sqrt(1.0 - jnp.exp(2.0 * log_a)) * (i * u[:, cols])

    n_groups = a_sc.shape[0] // SUBLANES
    row = lax.broadcasted_iota(jnp.int32, (SUBLANES, a_sc.shape[1]), 0)

    def body(it, h_prev):
        grp = (n_groups - 1 - it) if reverse else it
        off = pl.multiple_of(grp * SUBLANES, SUBLANES)
        a = a_sc[pl.ds(off, SUBLANES), :]
        b = b_sc[pl.ds(off, SUBLANES), :]
        for k in (1, 2, 4):
            shift = (SUBLANES - k) if reverse else k
            inside = (row < SUBLANES - k) if reverse else (row >= k)
            a_s = jnp.where(inside, pltpu.roll(a, shift, axis=0), 1.0)
            b_s = jnp.where(inside, pltpu.roll(b, shift, axis=0), 0.0)
            b = a * b_s + b
            a = a * a_s
        h = b + a * h_prev
        b_sc[pl.ds(off, SUBLANES), :] = h
        return h[0:1] if reverse else h[SUBLANES - 1:SUBLANES]

    h_sc[...] = lax.fori_loop(0, n_groups, body, h_sc[...])
    if reverse:
        o_ref[...] = (jax.nn.gelu(p7_ref[...]) * (hf_ref[...] + b_sc[...])).astype(o_ref.dtype)
    else:
        o_ref[...] = b_sc[...]


def _seq_block(b, s, reverse):
    lat = (LAT_BLOCKS - s) if reverse else (s - 1)
    return jnp.where(s == 0, BATCH * LAT_BLOCKS + b, b * LAT_BLOCKS + lat)


def rglru_mixer(proj, col_in, col_gate, conv_w, conv_b, wa, ba, wx, bx, lam):
    w = BRANCH_W
    row2 = lambda a: a.reshape(1, w)
    outs = None
    for reverse in (False, True):
        d = int(reverse)
        blk = lambda col: (lambda b, s: (_seq_block(b, s, reverse), col))
        const2 = lambda b, s: (0, 0)
        in_specs = [pl.BlockSpec((SEQ_BLOCK, w), blk(col_in)),
                    pl.BlockSpec(conv_w.shape, const2),
                    pl.BlockSpec((1, w), const2),
                    pl.BlockSpec((1,) + wa.shape[1:], lambda b, s: (d, 0, 0, 0)),
                    pl.BlockSpec((1, w), const2),
                    pl.BlockSpec((1,) + wx.shape[1:], lambda b, s: (d, 0, 0, 0)),
                    pl.BlockSpec((1, w), const2),
                    pl.BlockSpec((1, w), const2)]
        args = [proj, conv_w, row2(conv_b), wa, row2(ba[d]), wx, row2(bx[d]), row2(lam[d])]
        if reverse:
            in_specs += [pl.BlockSpec((SEQ_BLOCK, w), blk(0)), pl.BlockSpec((SEQ_BLOCK, w), blk(col_gate))]
            args += [outs, proj]
        outs = pl.pallas_call(
            functools.partial(_rglru_kernel, reverse=reverse),
            grid=(BATCH, LAT_BLOCKS + 1),
            in_specs=in_specs,
            out_specs=pl.BlockSpec((SEQ_BLOCK, w), blk(0)),
            out_shape=jax.ShapeDtypeStruct((ROWS, w), jnp.bfloat16 if reverse else jnp.float32),
            scratch_shapes=[pltpu.VMEM((SEQ_BLOCK, w), jnp.float32), pltpu.VMEM((SEQ_BLOCK, w), jnp.float32),
                            pltpu.VMEM((1, w), jnp.float32)],
            compiler_params=_params("parallel", "arbitrary"),
            name="rglru_bwd" if reverse else "rglru_fwd",
        )(*args)
    return outs


def _sconv_kernel(pb_ref, pc_ref, px_ref, w_ref, o_ref):
    is_ctx = pl.program_id(0) >= BATCH * LAT_BLOCKS
    conv = _masked_conv(pc_ref[...] * px_ref[...], w_ref, SC_CONV_LEFT, is_ctx)
    o_ref[...] = (pb_ref[...] * conv).astype(o_ref.dtype)


def sconv_mixer(proj, col_b, col_c, col_x, conv_w):
    w = BRANCH_W
    spec = lambda col: pl.BlockSpec((SEQ_BLOCK, w), lambda i: (i, col))
    return pl.pallas_call(
        _sconv_kernel,
        grid=(ROWS // SEQ_BLOCK,),
        in_specs=[spec(col_b), spec(col_c), spec(col_x), pl.BlockSpec(conv_w.shape, lambda i: (0, 0))],
        out_specs=pl.BlockSpec((SEQ_BLOCK, w), lambda i: (i, 0)),
        out_shape=jax.ShapeDtypeStruct((ROWS, w), jnp.bfloat16),
        compiler_params=_params("parallel"),
        name="sconv",
    )(proj, proj, proj, conv_w)


def _merge_kernel(h_ref, wg_ref, *rest):
    y_refs, (wb_ref, o_ref, acc_ref) = rest[:N_BRANCHES], rest[N_BRANCHES:]
    b = pl.program_id(2)

    @pl.when(b == 0)
    def _():
        acc_ref[...] = jnp.zeros_like(acc_ref)

    gate = jax.nn.sigmoid(jnp.dot(h_ref[...], wg_ref[...], preferred_element_type=jnp.float32))
    y = y_refs[0][...]
    for branch in range(1, N_BRANCHES):
        y = jnp.where(b == branch, y_refs[branch][...], y)
    acc_ref[...] += gate * jnp.dot(y, wb_ref[0], preferred_element_type=jnp.float32)

    @pl.when(b == N_BRANCHES - 1)
    def _():
        o_ref[...] = acc_ref[...].astype(o_ref.dtype)


def _row_tile(m):
    return 2 * ROW_BLOCK if m % (2 * ROW_BLOCK) == 0 else ROW_BLOCK


def merge_branches(h, w_gate, ys, w_branch, m):
    d = h.shape[1]
    bw = ys[0].shape[1]
    tm = _row_tile(m)
    tn = 1024 * ROW_BLOCK // tm
    n_col = d // tn
    y_spec = pl.BlockSpec((tm, bw), lambda i, j, b: (i, 0))
    return pl.pallas_call(
        _merge_kernel,
        grid=(m // tm, n_col, N_BRANCHES),
        in_specs=[pl.BlockSpec((tm, d), lambda i, j, b: (i, 0)),
                  pl.BlockSpec((d, tn), lambda i, j, b: (0, b * n_col + j))]
                 + [y_spec] * N_BRANCHES
                 + [pl.BlockSpec((1, bw, tn), lambda i, j, b: (b, 0, j))],
        out_specs=pl.BlockSpec((tm, tn), lambda i, j, b: (i, j)),
        out_shape=jax.ShapeDtypeStruct((m, d), jnp.bfloat16),
        scratch_shapes=[pltpu.VMEM((tm, tn), jnp.float32)],
        compiler_params=_params("parallel", "parallel", "arbitrary"),
        name="merge_branches",
    )(h, w_gate, *ys, w_branch)


def _extract_topk(s, n_top, val_ref, idx_ref, slot):
    n_rows = s.shape[0]
    rid = lax.broadcasted_iota(jnp.int32, s.shape, 0).astype(jnp.float32)
    for r in range(n_top):
        m = jnp.max(s, axis=0, keepdims=True)
        am = jnp.min(jnp.where(s == m, rid, float(n_rows)), axis=0, keepdims=True)
        val_ref[slot, r:r + 1, :] = m
        idx_ref[slot, r:r + 1, :] = am
        s = jnp.where(rid == am, -jnp.inf, s)


def _lookup_rows(table, sel):
    out = jnp.zeros(sel.shape, table.dtype)
    for r in range(table.shape[0]):
        out = jnp.where(sel == r, table[r:r + 1, :], out)
    return out


def _peer_topk_kernel(q_ref, keys_ref, i1_ref, i2_ref, w_ref, val_sc, idx_sc, cand_sc, top_sc, pos_sc, ent_sc):
    n_half = keys_ref.shape[0]
    for hp in range(n_half):
        q = q_ref[:, hp * PEER_DKH:(hp + 1) * PEER_DKH]
        s = lax.dot_general(keys_ref[hp], q, (((1,), (1,)), ((), ())), precision=lax.Precision.HIGHEST,
                            preferred_element_type=jnp.float32)
        _extract_topk(s, PEER_TOPK, val_sc, idx_sc, hp)
    for h in range(PEER_HEADS):
        v1, v2 = val_sc[2 * h], val_sc[2 * h + 1]
        for j1 in range(PEER_TOPK):
            cand_sc[j1 * PEER_TOPK:(j1 + 1) * PEER_TOPK, :] = v1[j1:j1 + 1, :] + v2
        _extract_topk(cand_sc[...], PEER_TOPK, top_sc, pos_sc, 0)
        top, pos = top_sc[0], pos_sc[0].astype(jnp.int32)
        e = jnp.exp(top - top[0:1, :])
        rows = slice(h * PEER_TOPK, (h + 1) * PEER_TOPK)
        ent_sc[0, rows, :] = _lookup_rows(idx_sc[2 * h], pos >> TOPK_SHIFT)
        ent_sc[1, rows, :] = _lookup_rows(idx_sc[2 * h + 1], pos & (PEER_TOPK - 1))
        ent_sc[2, rows, :] = e / jnp.sum(e, axis=0, keepdims=True)
    i1_ref[...] = ent_sc[0].T.astype(jnp.int32)
    i2_ref[...] = ent_sc[1].T.astype(jnp.int32)
    w_ref[...] = ent_sc[2].T


def peer_topk(q, keys, tt=LANES):
    n_tok = q.shape[0]
    n_ent = PEER_HEADS * PEER_TOPK
    assert n_ent == tt
    ent_spec = pl.BlockSpec((tt, n_ent), lambda i: (i, 0))
    f32, i32 = jnp.float32, jnp.int32
    return pl.pallas_call(
        _peer_topk_kernel,
        grid=(n_tok // tt,),
        in_specs=[pl.BlockSpec((tt, q.shape[1]), lambda i: (i, 0)),
                  pl.BlockSpec(keys.shape, lambda i: (0, 0, 0))],
        out_specs=[ent_spec, ent_spec, ent_spec],
        out_shape=[jax.ShapeDtypeStruct((n_tok, n_ent), i32), jax.ShapeDtypeStruct((n_tok, n_ent), i32),
                   jax.ShapeDtypeStruct((n_tok, n_ent), f32)],
        scratch_shapes=[pltpu.VMEM((2 * PEER_HEADS, PEER_TOPK, tt), f32), pltpu.VMEM((2 * PEER_HEADS, PEER_TOPK, tt), f32),
                        pltpu.VMEM((PEER_TOPK * PEER_TOPK, tt), f32),
                        pltpu.VMEM((1, PEER_TOPK, tt), f32), pltpu.VMEM((1, PEER_TOPK, tt), f32),
                        pltpu.VMEM((3, n_ent, tt), f32)],
        compiler_params=_params("parallel"),
        name="peer_topk",
    )(q, keys)


def _peer_score_kernel(h_ref, u_ref, i1_ref, i2_ref, o_ref):
    j = pl.program_id(1)

    @pl.when(j == 0)
    def _():
        o_ref[...] = jnp.zeros_like(o_ref)

    s = lax.dot_general(h_ref[...], u_ref[...], (((1,), (1,)), ((), ())), preferred_element_type=jnp.float32)
    i1, i2 = i1_ref[...], i2_ref[...]
    acc = o_ref[...]
    n_chunks = s.shape[1] // PEER_NKEYS
    for c in range(n_chunks):
        picked = jnp.take_along_axis(s[:, c * PEER_NKEYS:(c + 1) * PEER_NKEYS], i2, axis=1)
        acc = jnp.where(i1 == j * n_chunks + c, picked, acc)
    o_ref[...] = acc


def peer_scores(h, u_tab, i1, i2, tn=1024):
    n_tok, d = h.shape
    n_exp = u_tab.shape[0]
    n_ent = i1.shape[1]
    tm = _row_tile(n_tok)
    assert n_exp % tn == 0 and n_ent == PEER_NKEYS
    ent_spec = pl.BlockSpec((tm, n_ent), lambda i, j: (i, 0))
    return pl.pallas_call(
        _peer_score_kernel,
        grid=(n_tok // tm, n_exp // tn),
        in_specs=[pl.BlockSpec((tm, d), lambda i, j: (i, 0)),
                  pl.BlockSpec((tn, d), lambda i, j: (j, 0)),
                  ent_spec, ent_spec],
        out_specs=ent_spec,
        out_shape=jax.ShapeDtypeStruct((n_tok, n_ent), jnp.float32),
        compiler_params=_params("parallel", "arbitrary"),
        name="peer_scores",
    )(h, u_tab, i1, i2)


BF16_SUBLANES = 16


def _peer_coef_kernel(sc_ref, w_ref, i1_ref, i2_ref, o_ref, wa_sc, ct_sc):
    wa_sc[...] = w_ref[...] * jax.nn.gelu(sc_ref[...])
    n_keys = PEER_NKEYS
    n_ent = sc_ref.shape[1]
    group = ct_sc.shape[0]
    key = lax.broadcasted_iota(jnp.int32, (n_keys, n_ent), 0)

    def body(gi, carry):
        t0 = pl.multiple_of(gi * group, group)
        for u in range(group):
            row = lambda ref: jnp.broadcast_to(ref[pl.ds(t0 + u, 1), :], (n_keys, n_ent))
            at = jnp.where(key == row(i1_ref), row(wa_sc), 0.0).astype(jnp.bfloat16)
            bt = jnp.where(key == row(i2_ref), 1.0, 0.0).astype(jnp.bfloat16)
            ct_sc[u] = lax.dot_general(at, bt, (((1,), (1,)), ((), ())), preferred_element_type=jnp.float32)
        by_key = jnp.swapaxes(ct_sc[...], 0, 1)
        for a in range(n_keys):
            o_ref[pl.ds(t0, group), a * n_keys:(a + 1) * n_keys] = by_key[a].astype(o_ref.dtype)
        return carry

    lax.fori_loop(0, sc_ref.shape[0] // group, body, 0)


def peer_coef(sc, wts, i1, i2, tb=128):
    n_tok, n_ent = sc.shape
    tb = min(tb, n_tok)
    assert n_tok % tb == 0 and tb % BF16_SUBLANES == 0
    ent_spec = pl.BlockSpec((tb, n_ent), lambda i: (i, 0))
    n_exp = PEER_NKEYS * PEER_NKEYS
    return pl.pallas_call(
        _peer_coef_kernel,
        grid=(n_tok // tb,),
        in_specs=[ent_spec] * 4,
        out_specs=pl.BlockSpec((tb, n_exp), lambda i: (i, 0)),
        out_shape=jax.ShapeDtypeStruct((n_tok, n_exp), jnp.bfloat16),
        scratch_shapes=[pltpu.VMEM((tb, n_ent), jnp.float32),
                        pltpu.VMEM((BF16_SUBLANES, PEER_NKEYS, PEER_NKEYS), jnp.float32)],
        compiler_params=_params("parallel"),
        name="peer_coef",
    )(sc, wts, i1, i2)


def peer_ffn(h, x, modtab, w_q, keys, u_tab, v_tab):
    q = mm(h, w_q, tk=D_MODEL)
    i1, i2, wts = peer_topk(q, keys.reshape(2 * PEER_HEADS, PEER_NKEYS, PEER_DKH))
    sc = peer_scores(h, u_tab, i1, i2)
    return mm_resid(peer_coef(sc, wts, i1, i2), v_tab, x, modtab, MOD_GATE2)


def _dft_parts(n):
    idx = jnp.arange(n, dtype=jnp.int32)
    ang = ((idx[:, None] * idx[None, :]) % n).astype(jnp.float32) * (2.0 * jnp.pi / n)
    scale = n ** -0.5
    return jnp.cos(ang) * scale, jnp.sin(ang) * scale


def dft_tables():
    cc, sc = _dft_parts(FOURIER_GW)
    ct_l, st_l = _dft_parts(SEQ)
    ct_c, st_c = _dft_parts(CTX_LEN)
    bf16 = jnp.bfloat16
    return (jnp.concatenate([cc, sc], axis=1).astype(bf16),
            jnp.concatenate([ct_l, -st_l], axis=1).astype(bf16),
            jnp.concatenate([ct_c, -st_c], axis=1).astype(bf16))


def _fourier_kernel(p_ref, chan_ref, pos_ref, *rest):
    o_ref, gcs_sc = rest[-2:]
    seq = p_ref.shape[0]
    n_r = seq // o_ref.shape[0]
    r = pl.program_id(2)

    @pl.when(r == 0)
    def _():
        gc = jnp.dot(p_ref[...].astype(jnp.bfloat16), chan_ref[...], preferred_element_type=jnp.float32)
        gcs_sc[0:seq, :] = gc[:, :FOURIER_GW].astype(gcs_sc.dtype)
        gcs_sc[seq:2 * seq, :] = gc[:, FOURIER_GW:].astype(gcs_sc.dtype)

    @pl.when(r < n_r)
    def _():
        o_ref[...] = jnp.dot(pos_ref[...], gcs_sc[...], preferred_element_type=jnp.float32).astype(o_ref.dtype)

    if len(rest) == 3:
        @pl.when(r == n_r)
        def _():
            o_ref[...] = rest[0][...]


def fourier_mixer(proj, tables, tr=ROW_BLOCK):
    chan, pos_lat, pos_ctx = tables
    gw = FOURIER_GW
    ctx_rows = BATCH * CTX_LEN
    assert ctx_rows == tr
    n_r = SEQ // tr
    chan_spec = pl.BlockSpec(chan.shape, lambda b, g, r: (0, 0))
    y_ctx = pl.pallas_call(
        _fourier_kernel,
        grid=(BATCH, FOURIER_GROUPS, 1),
        in_specs=[pl.BlockSpec((CTX_LEN, gw), lambda b, g, r: (LAT_ROWS // CTX_LEN + b, g)), chan_spec,
                  pl.BlockSpec(pos_ctx.shape, lambda b, g, r: (0, 0))],
        out_specs=pl.BlockSpec((CTX_LEN, gw), lambda b, g, r: (b, g)),
        out_shape=jax.ShapeDtypeStruct((ctx_rows, BRANCH_W), jnp.bfloat16),
        scratch_shapes=[pltpu.VMEM((2 * CTX_LEN, gw), jnp.bfloat16)],
        compiler_params=_params("parallel", "parallel", "arbitrary"),
        name="fourier_ctx",
    )(proj, chan, pos_ctx)
    return pl.pallas_call(
        _fourier_kernel,
        grid=(BATCH, FOURIER_GROUPS, n_r + 1),
        in_specs=[pl.BlockSpec((SEQ, gw), lambda b, g, r: (b, g)), chan_spec,
                  pl.BlockSpec((tr, 2 * SEQ), lambda b, g, r: (jnp.minimum(r, n_r - 1), 0)),
                  pl.BlockSpec((ctx_rows, gw), lambda b, g, r: (0, g))],
        out_specs=pl.BlockSpec((tr, gw), lambda b, g, r: (jnp.where(r == n_r, BATCH * n_r, b * n_r + r), g)),
        out_shape=jax.ShapeDtypeStruct((ROWS, BRANCH_W), jnp.bfloat16),
        scratch_shapes=[pltpu.VMEM((2 * SEQ, gw), jnp.bfloat16)],
        compiler_params=_params("arbitrary", "arbitrary", "arbitrary"),
        name="fourier_lat",
    )(proj, chan, pos_lat, y_ctx)


def _mlstm_kernel(q_ref, k_ref, v_ref, gt_ref, o_ref, ct_sc, n_sc, m_sc):
    d = pl.program_id(0)
    head = pl.program_id(1) % MLSTM_HEADS
    f32, bf16 = jnp.float32, jnp.bfloat16
    n_t = q_ref.shape[0]

    @pl.when(pl.program_id(2) == 0)
    def _():
        ct_sc[...] = jnp.zeros_like(ct_sc)
        n_sc[...] = jnp.zeros_like(n_sc)
        m_sc[...] = jnp.zeros_like(m_sc)

    q = q_ref[...] * (MLSTM_DQK ** -0.5)
    k = k_ref[...]
    v = v_ref[...]
    gate0 = d * (2 * MLSTM_HEADS) + head
    li = gt_ref[pl.ds(gate0, 1), :]
    gf = gt_ref[pl.ds(gate0 + MLSTM_HEADS, 1), :]
    lf = -(jnp.log1p(jnp.exp(-jnp.abs(gf))) + jnp.maximum(-gf, 0.0))

    r = lax.broadcasted_iota(jnp.int32, (n_t, n_t), 0)
    c = lax.broadcasted_iota(jnp.int32, (n_t, n_t), 1)
    upto = (c - r) * (1 - 2 * d) <= 0
    eye = r == c
    b_col = jnp.sum(jnp.where(upto, lf, 0.0), axis=1, keepdims=True)
    b_row = jnp.sum(jnp.where(eye, b_col, 0.0), axis=0, keepdims=True)
    m_prev = m_sc[...]
    logw = jnp.where(upto, b_col - b_row + li, -jnp.inf)
    g_col = b_col + m_prev
    mt = jnp.maximum(g_col, jnp.max(logw, axis=1, keepdims=True))
    qb = q.astype(bf16)
    s = lax.dot_general(qb, k.astype(bf16), (((1,), (1,)), ((), ())), preferred_element_type=f32) * jnp.exp(logw - mt)
    w_inter = jnp.exp(g_col - mt)
    num = (jnp.dot(s.astype(bf16), v.astype(bf16), preferred_element_type=f32)
           + w_inter * jnp.dot(qb, ct_sc[...].astype(bf16), preferred_element_type=f32))
    den = jnp.sum(s, axis=1, keepdims=True) + w_inter * jnp.sum(q * n_sc[...], axis=1, keepdims=True)
    o_ref[0] = num / jnp.maximum(jnp.abs(den), jnp.exp(-mt))

    total = jnp.sum(lf, axis=1, keepdims=True)
    logu = total - b_row + li
    m_new = jnp.maximum(total + m_prev, jnp.max(logu, axis=1, keepdims=True))
    ws_row = jnp.exp(logu - m_new)
    wc = jnp.exp(total + m_prev - m_new)
    ws_col = jnp.sum(jnp.where(eye, ws_row, 0.0), axis=1, keepdims=True)
    kv = lax.dot_general(k.astype(bf16), (ws_col * v).astype(bf16), (((0,), (0,)), ((), ())),
                         preferred_element_type=f32)
    ct_sc[...] = wc * ct_sc[...] + kv
    n_sc[...] = wc * n_sc[...] + jnp.sum(ws_col * k, axis=0, keepdims=True)
    m_sc[...] = m_new


def _mlstm_chunk(d, bh, s):
    b = bh // MLSTM_HEADS
    n_ctx, n_lat = CTX_LEN // MLSTM_CHUNK, SEQ // MLSTM_CHUNK
    ctx_j = jnp.where(d == 0, s, n_ctx - 1 - s)
    lat_j = jnp.where(d == 0, s - n_ctx, n_ctx + n_lat - 1 - s)
    return jnp.where(s < n_ctx, BATCH * n_lat + b * n_ctx + ctx_j, b * n_lat + lat_j)


def mlstm_mixer(proj, gates_t):
    t = MLSTM_CHUNK
    q_col0 = BRANCH_W // MLSTM_DQK
    k_col0 = q_col0 + MLSTM_HEADS
    v_col0 = 2 * BRANCH_W // MLSTM_DV
    n_steps = (CTX_LEN + SEQ) // t
    return pl.pallas_call(
        _mlstm_kernel,
        grid=(2, BATCH * MLSTM_HEADS, n_steps),
        in_specs=[pl.BlockSpec((t, MLSTM_DQK), lambda d, bh, s: (_mlstm_chunk(d, bh, s), q_col0 + bh % MLSTM_HEADS)),
                  pl.BlockSpec((t, MLSTM_DQK), lambda d, bh, s: (_mlstm_chunk(d, bh, s), k_col0 + bh % MLSTM_HEADS)),
                  pl.BlockSpec((t, MLSTM_DV), lambda d, bh, s: (_mlstm_chunk(d, bh, s), v_col0 + bh % MLSTM_HEADS)),
                  pl.BlockSpec((N_MLSTM_GATES, t), lambda d, bh, s: (0, _mlstm_chunk(d, bh, s)))],
        out_specs=pl.BlockSpec((1, t, MLSTM_DV), lambda d, bh, s: (d, _mlstm_chunk(d, bh, s), bh % MLSTM_HEADS)),
        out_shape=jax.ShapeDtypeStruct((2, ROWS, BRANCH_W), jnp.float32),
        scratch_shapes=[pltpu.VMEM((MLSTM_DQK, MLSTM_DV), jnp.float32), pltpu.VMEM((1, MLSTM_DQK), jnp.float32),
                        pltpu.VMEM((1, 1), jnp.float32)],
        compiler_params=_params("parallel", "parallel", "arbitrary"),
        name="mlstm",
    )(proj, proj, proj, gates_t)


def _mlstm_out_kernel(hs_ref, og_ref, g_ref, o_ref):
    h = hs_ref[0] + hs_ref[1]
    for head in range(MLSTM_HEADS):
        cols = slice(head * MLSTM_DV, (head + 1) * MLSTM_DV)
        hh = h[:, cols]
        hn = hh * lax.rsqrt(jnp.mean(hh * hh, axis=-1, keepdims=True) + EPS)
        o_ref[:, cols] = (hn * g_ref[:, cols] * jax.nn.sigmoid(og_ref[:, cols])).astype(o_ref.dtype)


def mlstm_out(hs, proj, col_gate, norm_g, tm=SEQ_BLOCK):
    w = BRANCH_W
    return pl.pallas_call(
        _mlstm_out_kernel,
        grid=(ROWS // tm,),
        in_specs=[pl.BlockSpec((2, tm, w), lambda i: (0, i, 0)),
                  pl.BlockSpec((tm, w), lambda i: (i, col_gate)),
                  pl.BlockSpec((1, w), lambda i: (0, 0))],
        out_specs=pl.BlockSpec((tm, w), lambda i: (i, 0)),
        out_shape=jax.ShapeDtypeStruct((ROWS, w), jnp.bfloat16),
        compiler_params=_params("parallel"),
        name="mlstm_out",
    )(hs, proj, norm_g.reshape(1, w))


def kernel(x, c, ctx, c_ctx, w_mod, b_mod, g_norm1, g_norm2, w_in, w_branch, w_out, mlstm_gate_b, mlstm_norm_g,
           rg_conv_w, rg_conv_b, rg_wa, rg_ba, rg_wx, rg_bx, rg_lam, sc_conv_w, peer_wq, peer_keys, peer_u,
           peer_v, g_final):
    bf16 = jnp.bfloat16
    xs = jnp.concatenate([x.reshape(LAT_ROWS, D_MODEL), ctx.reshape(BATCH * CTX_LEN, D_MODEL)], axis=0)
    cond = jax.nn.silu(jnp.concatenate([c, c_ctx[None, :]], axis=0))
    tables = dft_tables()
    for l in range(DEPTH):
        last = l == DEPTH - 1
        modtab = (mm(cond, w_mod[l], tm=16, tn=2048, tk=2048) + b_mod[l]).reshape(BATCH + 1, 6, 1, D_MODEL)
        w_main = jnp.concatenate([w_in[l][:, :COL_MAIN], w_in[l][:, COL_GATES:COL_MERGE]], axis=1).astype(bf16)
        w_gates = jnp.pad(w_in[l][:, COL_MAIN:COL_GATES], ((0, 0), (0, LANES - N_MLSTM_GATES))).astype(bf16)
        w_merge = w_in[l][:, COL_MERGE:].astype(bf16)

        h = norm_mod(xs, g_norm1[l], modtab, MOD_SHIFT1, MOD_SCALE1)
        proj = mm(h, w_main, tk=D_MODEL)
        gates = mm(h, w_gates, tk=D_MODEL)
        y_four = fourier_mixer(proj, tables)
        gates_t = gates[:, :N_MLSTM_GATES].T + mlstm_gate_b[l].reshape(N_MLSTM_GATES, 1)
        y_ml = mlstm_out(mlstm_mixer(proj, gates_t), proj, 3, mlstm_norm_g[l])
        y_rg = rglru_mixer(proj, 4, 5, rg_conv_w[l], rg_conv_b[l], rg_wa[l], rg_ba[l], rg_wx[l], rg_bx[l], rg_lam[l])
        y_sc = sconv_mixer(proj, 6, 7, 8, sc_conv_w[l])
        m = LAT_ROWS if last else ROWS
        merged = merge_branches(h, w_merge, (y_four, y_ml, y_rg, y_sc), w_branch[l].astype(bf16), m)
        xs = mm_resid(merged, w_out[l].astype(bf16), xs, modtab, MOD_GATE1)

        h2 = norm_mod(xs, g_norm2[l], modtab, MOD_SHIFT2, MOD_SCALE2)
        xs = peer_ffn(h2, xs, modtab, peer_wq[l].astype(bf16), peer_keys[l], peer_u[l].astype(bf16),
                      peer_v[l].astype(bf16))
    return rmsnorm_rows(xs, g_final).reshape(BATCH, SEQ, D_MODEL)
```

```python
import functools

import jax
import jax.numpy as jnp
from jax import lax
from jax.experimental import pallas as pl
from jax.experimental.pallas import tpu as pltpu

D_MODEL = 4096
BATCH = 2
SEQ = 4096
DEPTH = 2
CTX_LEN = 256
GRID_W = 64
N_BRANCHES = 4
BRANCH_W = D_MODEL // 4
FOURIER_GROUPS = 4
FOURIER_GW = BRANCH_W // FOURIER_GROUPS
MLSTM_HEADS = 4
MLSTM_DV = BRANCH_W // MLSTM_HEADS
MLSTM_DQK = MLSTM_DV // 2
MLSTM_CHUNK = 128
RG_BLOCKS = 8
RG_BW = BRANCH_W // RG_BLOCKS
RG_C = 8.0
RG_CONV_LEFT = 2
SC_CONV_LEFT = 1
PEER_HEADS = 8
PEER_NKEYS = 128
PEER_DK = 256
PEER_DKH = PEER_DK // 2
PEER_TOPK = 16
TOPK_SHIFT = PEER_TOPK.bit_length() - 1
EPS = 1e-6

N_MLSTM_GATES = 2 * 2 * MLSTM_HEADS
COL_MAIN = 4 * BRANCH_W
COL_GATES = COL_MAIN + N_MLSTM_GATES
COL_MERGE = COL_GATES + 5 * BRANCH_W

LAT_ROWS = BATCH * SEQ
ROWS = LAT_ROWS + BATCH * CTX_LEN
SEQ_BLOCK = CTX_LEN
LAT_BLOCKS = SEQ // SEQ_BLOCK
ROW_BLOCK = 512
SUBLANES = 8
BF16_SUBLANES = 16
LANES = 128
VMEM_LIMIT_BYTES = 48 * 1024 * 1024

MOD_SHIFT1, MOD_SCALE1, MOD_GATE1, MOD_SHIFT2, MOD_SCALE2, MOD_GATE2 = range(6)


def _round_up(x, m):
    return (x + m - 1) // m * m


def _params(*semantics):
    return pltpu.CompilerParams(dimension_semantics=semantics, vmem_limit_bytes=VMEM_LIMIT_BYTES)


def _segment(row_block, rows_per_block):
    return jnp.minimum(row_block // (SEQ // rows_per_block), BATCH)


def _mm_kernel(a_ref, b_ref, o_ref, acc_ref):
    @pl.when(pl.program_id(2) == 0)
    def _():
        acc_ref[...] = jnp.zeros_like(acc_ref)

    acc_ref[...] += jnp.dot(a_ref[...].astype(jnp.bfloat16), b_ref[...].astype(jnp.bfloat16),
                            preferred_element_type=jnp.float32)

    @pl.when(pl.program_id(2) == pl.num_programs(2) - 1)
    def _():
        o_ref[...] = acc_ref[...].astype(o_ref.dtype)


def _mm_fullk_kernel(a_ref, b_ref, o_ref):
    o_ref[...] = jnp.dot(a_ref[...].astype(jnp.bfloat16), b_ref[...].astype(jnp.bfloat16),
                         preferred_element_type=jnp.float32).astype(o_ref.dtype)


def mm(a, b, out_dtype=jnp.float32, tm=ROW_BLOCK, tn=1024, tk=2048):
    m, k = a.shape
    _, n = b.shape
    tm = min(tm, _round_up(m, 16))
    tn = min(tn, _round_up(n, LANES))
    tk = min(tk, k)
    mp, np_ = _round_up(m, tm), _round_up(n, tn)
    if mp != m:
        a = jnp.pad(a, ((0, mp - m), (0, 0)))
    if np_ != n:
        b = jnp.pad(b, ((0, 0), (0, np_ - n)))
    if tk == k:
        out = pl.pallas_call(
            _mm_fullk_kernel,
            grid=(mp // tm, np_ // tn),
            in_specs=[pl.BlockSpec((tm, k), lambda i, j: (i, 0)),
                      pl.BlockSpec((k, tn), lambda i, j: (0, j))],
            out_specs=pl.BlockSpec((tm, tn), lambda i, j: (i, j)),
            out_shape=jax.ShapeDtypeStruct((mp, np_), out_dtype),
            compiler_params=_params("parallel", "parallel"),
            name="mm_fullk",
        )(a, b)
    else:
        out = pl.pallas_call(
            _mm_kernel,
            grid=(mp // tm, np_ // tn, k // tk),
            in_specs=[pl.BlockSpec((tm, tk), lambda i, j, kk: (i, kk)),
                      pl.BlockSpec((tk, tn), lambda i, j, kk: (kk, j))],
            out_specs=pl.BlockSpec((tm, tn), lambda i, j, kk: (i, j)),
            out_shape=jax.ShapeDtypeStruct((mp, np_), out_dtype),
            scratch_shapes=[pltpu.VMEM((tm, tn), jnp.float32)],
            compiler_params=_params("parallel", "parallel", "arbitrary"),
            name="mm",
        )(a, b)
    if mp != m or np_ != n:
        out = out[:m, :n]
    return out


def _mm_resid_kernel(a_ref, b_ref, x_ref, g_ref, o_ref, acc_ref):
    row_block = pl.program_id(0)

    @pl.when(pl.program_id(2) == 0)
    def _():
        acc_ref[...] = jnp.zeros_like(acc_ref)

    acc_ref[...] += jnp.dot(a_ref[...], b_ref[...], preferred_element_type=jnp.float32)

    @pl.when(pl.program_id(2) == pl.num_programs(2) - 1)
    def _():
        tm = o_ref.shape[0]
        row = row_block * tm + lax.broadcasted_iota(jnp.int32, (tm, 1), 0)
        gate = g_ref[0, 0]
        for seg in range(1, BATCH + 1):
            gate = jnp.where(row >= seg * SEQ, g_ref[seg, 0], gate)
        o_ref[...] = x_ref[...] + gate * acc_ref[...]


def mm_resid(a, b, x, modtab, which, tn=1024, tk=2048):
    m, k = a.shape
    _, n = b.shape
    tm = _row_tile(m)
    tn, tk = min(tn, n), min(tk, k)
    assert n % tn == 0 and k % tk == 0
    return pl.pallas_call(
        _mm_resid_kernel,
        grid=(m // tm, n // tn, k // tk),
        in_specs=[pl.BlockSpec((tm, tk), lambda i, j, kk: (i, kk)),
                  pl.BlockSpec((tk, tn), lambda i, j, kk: (kk, j)),
                  pl.BlockSpec((tm, tn), lambda i, j, kk: (i, j)),
                  pl.BlockSpec((BATCH + 1, 1, 1, tn), lambda i, j, kk: (0, which, 0, j))],
        out_specs=pl.BlockSpec((tm, tn), lambda i, j, kk: (i, j)),
        out_shape=jax.ShapeDtypeStruct((m, n), jnp.float32),
        scratch_shapes=[pltpu.VMEM((tm, tn), jnp.float32)],
        compiler_params=_params("parallel", "parallel", "arbitrary"),
        name="mm_resid",
    )(a, b, x, modtab)


def _norm_mod_kernel(x_ref, g_ref, sh_ref, sc_ref, o_ref):
    x = x_ref[...]
    y = x * lax.rsqrt(jnp.mean(x * x, axis=-1, keepdims=True) + EPS)
    o_ref[...] = ((y * g_ref[...]) * (1.0 + sc_ref[0, 0]) + sh_ref[0, 0]).astype(o_ref.dtype)


def norm_mod(x, g, modtab, which_shift, which_scale, tm=SEQ_BLOCK):
    m, d = x.shape
    mod_spec = lambda which: pl.BlockSpec((1, 1, 1, d), lambda i: (_segment(i, tm), which, 0, 0))
    return pl.pallas_call(
        _norm_mod_kernel,
        grid=(m // tm,),
        in_specs=[pl.BlockSpec((tm, d), lambda i: (i, 0)),
                  pl.BlockSpec((1, d), lambda i: (0, 0)),
                  mod_spec(which_shift), mod_spec(which_scale)],
        out_specs=pl.BlockSpec((tm, d), lambda i: (i, 0)),
        out_shape=jax.ShapeDtypeStruct((m, d), jnp.bfloat16),
        compiler_params=_params("parallel"),
        name="norm_mod",
    )(x, g.reshape(1, d), modtab, modtab)


def _rmsnorm_kernel(x_ref, g_ref, o_ref):
    x = x_ref[...]
    o_ref[...] = x * lax.rsqrt(jnp.mean(x * x, axis=-1, keepdims=True) + EPS) * g_ref[...]


def rmsnorm_rows(x, g, tm=SEQ_BLOCK):
    m, d = x.shape
    return pl.pallas_call(
        _rmsnorm_kernel,
        grid=(m // tm,),
        in_specs=[pl.BlockSpec((tm, d), lambda i: (i, 0)), pl.BlockSpec((1, d), lambda i: (0, 0))],
        out_specs=pl.BlockSpec((tm, d), lambda i: (i, 0)),
        out_shape=jax.ShapeDtypeStruct((m, d), jnp.float32),
        compiler_params=_params("parallel"),
        name="rmsnorm",
    )(x, g.reshape(1, d))


def _masked_conv(u, w_ref, pad_l, is_ctx):
    rows = u.shape[0]
    t = lax.broadcasted_iota(jnp.int32, (rows, 1), 0)
    seg = jnp.where(is_ctx, rows, GRID_W)
    pos = t & (seg - 1)
    y = None
    for j in range(w_ref.shape[0]):
        k = j - pad_l
        if k == 0:
            sh = u
        else:
            sh = pltpu.roll(u, (-k) % rows, axis=0)
            sh = jnp.where((pos + k >= 0) & (pos + k < seg), sh, 0.0)
        term = w_ref[j:j + 1, :] * sh
        y = term if y is None else y + term
    return y


def _rglru_kernel(*refs, reverse):
    if reverse:
        (p6_ref, cw_ref, cb_ref, wa_ref, ba_ref, wx_ref, bx_ref, lam_ref, hf_ref, p7_ref,
         o_ref, a_sc, b_sc, h_sc) = refs
    else:
        p6_ref, cw_ref, cb_ref, wa_ref, ba_ref, wx_ref, bx_ref, lam_ref, o_ref, a_sc, b_sc, h_sc = refs
    s = pl.program_id(1)

    @pl.when(s == 0)
    def _():
        h_sc[...] = jnp.zeros_like(h_sc)

    u = _masked_conv(p6_ref[...], cw_ref, RG_CONV_LEFT, s == 0) + cb_ref[...]
    ub = u.astype(jnp.bfloat16)
    for g in range(RG_BLOCKS):
        cols = slice(g * RG_BW, (g + 1) * RG_BW)
        ug = ub[:, cols]
        r = jax.nn.sigmoid(jnp.dot(ug, wa_ref[0, g].astype(jnp.bfloat16), preferred_element_type=jnp.float32)
                           + ba_ref[:, cols])
        i = jax.nn.sigmoid(jnp.dot(ug, wx_ref[0, g].astype(jnp.bfloat16), preferred_element_type=jnp.float32)
                           + bx_ref[:, cols])
        neg_lam = -lam_ref[:, cols]
        softplus = jnp.log1p(jnp.exp(-jnp.abs(neg_lam))) + jnp.maximum(neg_lam, 0.0)
        log_a = (-RG_C * softplus) * r
        a_sc[:, cols] = jnp.exp(log_a)
        b_sc[:, cols] = jnp.sqrt(1.0 - jnp.exp(2.0 * log_a)) * (i * u[:, cols])

    n_groups = a_sc.shape[0] // SUBLANES
    row = lax.broadcasted_iota(jnp.int32, (SUBLANES, a_sc.shape[1]), 0)

    def body(it, h_prev):
        grp = (n_groups - 1 - it) if reverse else it
        off = pl.multiple_of(grp * SUBLANES, SUBLANES)
        a = a_sc[pl.ds(off, SUBLANES), :]
        b = b_sc[pl.ds(off, SUBLANES), :]
        for k in (1, 2, 4):
            shift = (SUBLANES - k) if reverse else k
            inside = (row < SUBLANES - k) if reverse else (row >= k)
            a_s = jnp.where(inside, pltpu.roll(a, shift, axis=0), 1.0)
            b_s = jnp.where(inside, pltpu.roll(b, shift, axis=0), 0.0)
            b = a * b_s + b
            a = a * a_s
        h = b + a * h_prev
        b_sc[pl.ds(off, SUBLANES), :] = h
        return h[0:1] if reverse else h[SUBLANES - 1:SUBLANES]

    h_sc[...] = lax.fori_loop(0, n_groups, body, h_sc[...])
    if reverse:
        o_ref[...] = (jax.nn.gelu(p7_ref[...]) * (hf_ref[...] + b_sc[...])).astype(o_ref.dtype)
    else:
        o_ref[...] = b_sc[...]


def _seq_block(b, s, reverse):
    lat = (LAT_BLOCKS - s) if reverse else (s - 1)
    return jnp.where(s == 0, BATCH * LAT_BLOCKS + b, b * LAT_BLOCKS + lat)


def rglru_mixer(proj, col_in, col_gate, conv_w, conv_b, wa, ba, wx, bx, lam):
    w = BRANCH_W
    row2 = lambda a: a.reshape(1, w)
    outs = None
    for reverse in (False, True):
        d = int(reverse)
        blk = lambda col: (lambda b, s: (_seq_block(b, s, reverse), col))
        const2 = lambda b, s: (0, 0)
        in_specs = [pl.BlockSpec((SEQ_BLOCK, w), blk(col_in)),
                    pl.BlockSpec(conv_w.shape, const2),
                    pl.BlockSpec((1, w), const2),
                    pl.BlockSpec((1,) + wa.shape[1:], lambda b, s: (d, 0, 0, 0)),
                    pl.BlockSpec((1, w), const2),
                    pl.BlockSpec((1,) + wx.shape[1:], lambda b, s: (d, 0, 0, 0)),
                    pl.BlockSpec((1, w), const2),
                    pl.BlockSpec((1, w), const2)]
        args = [proj, conv_w, row2(conv_b), wa, row2(ba[d]), wx, row2(bx[d]), row2(lam[d])]
        if reverse:
            in_specs += [pl.BlockSpec((SEQ_BLOCK, w), blk(0)), pl.BlockSpec((SEQ_BLOCK, w), blk(col_gate))]
            args += [outs, proj]
        outs = pl.pallas_call(
            functools.partial(_rglru_kernel, reverse=reverse),
            grid=(BATCH, LAT_BLOCKS + 1),
            in_specs=in_specs,
            out_specs=pl.BlockSpec((SEQ_BLOCK, w), blk(0)),
            out_shape=jax.ShapeDtypeStruct((ROWS, w), jnp.bfloat16 if reverse else jnp.float32),
            scratch_shapes=[pltpu.VMEM((SEQ_BLOCK, w), jnp.float32), pltpu.VMEM((SEQ_BLOCK, w), jnp.float32),
                            pltpu.VMEM((1, w), jnp.float32)],
            compiler_params=_params("parallel", "arbitrary"),
            name="rglru_bwd" if reverse else "rglru_fwd",
        )(*args)
    return outs


def _sconv_kernel(pb_ref, pc_ref, px_ref, w_ref, o_ref):
    is_ctx = pl.program_id(0) >= BATCH * LAT_BLOCKS
    conv = _masked_conv(pc_ref[...] * px_ref[...], w_ref, SC_CONV_LEFT, is_ctx)
    o_ref[...] = (pb_ref[...] * conv).astype(o_ref.dtype)


def sconv_mixer(proj, col_b, col_c, col_x, conv_w):
    w = BRANCH_W
    spec = lambda col: pl.BlockSpec((SEQ_BLOCK, w), lambda i: (i, col))
    return pl.pallas_call(
        _sconv_kernel,
        grid=(ROWS // SEQ_BLOCK,),
        in_specs=[spec(col_b), spec(col_c), spec(col_x), pl.BlockSpec(conv_w.shape, lambda i: (0, 0))],
        out_specs=pl.BlockSpec((SEQ_BLOCK, w), lambda i: (i, 0)),
        out_shape=jax.ShapeDtypeStruct((ROWS, w), jnp.bfloat16),
        compiler_params=_params("parallel"),
        name="sconv",
    )(proj, proj, proj, conv_w)


def _merge_kernel(h_ref, wg_ref, *rest):
    y_refs, (wb_ref, o_ref, acc_ref) = rest[:N_BRANCHES], rest[N_BRANCHES:]
    b = pl.program_id(2)

    @pl.when(b == 0)
    def _():
        acc_ref[...] = jnp.zeros_like(acc_ref)

    gate = jax.nn.sigmoid(jnp.dot(h_ref[...], wg_ref[...], preferred_element_type=jnp.float32))
    y = y_refs[0][...]
    for branch in range(1, N_BRANCHES):
        y = jnp.where(b == branch, y_refs[branch][...], y)
    acc_ref[...] += gate * jnp.dot(y, wb_ref[0], preferred_element_type=jnp.float32)

    @pl.when(b == N_BRANCHES - 1)
    def _():
        o_ref[...] = acc_ref[...].astype(o_ref.dtype)


MAX_ROW_TILE = 1088


def _row_tile(m):
    return max(t for t in range(BF16_SUBLANES, MAX_ROW_TILE + 1, BF16_SUBLANES) if m % t == 0)


def merge_branches(h, w_gate, ys, w_branch, m):
    d = h.shape[1]
    bw = ys[0].shape[1]
    tm = 2 * ROW_BLOCK if m % (2 * ROW_BLOCK) == 0 else ROW_BLOCK
    tn = 1024 * ROW_BLOCK // tm
    n_col = d // tn
    y_spec = pl.BlockSpec((tm, bw), lambda i, j, b: (i, 0))
    return pl.pallas_call(
        _merge_kernel,
        grid=(m // tm, n_col, N_BRANCHES),
        in_specs=[pl.BlockSpec((tm, d), lambda i, j, b: (i, 0)),
                  pl.BlockSpec((d, tn), lambda i, j, b: (0, b * n_col + j))]
                 + [y_spec] * N_BRANCHES
                 + [pl.BlockSpec((1, bw, tn), lambda i, j, b: (b, 0, j))],
        out_specs=pl.BlockSpec((tm, tn), lambda i, j, b: (i, j)),
        out_shape=jax.ShapeDtypeStruct((m, d), jnp.bfloat16),
        scratch_shapes=[pltpu.VMEM((tm, tn), jnp.float32)],
        compiler_params=_params("parallel", "parallel", "arbitrary"),
        name="merge_branches",
    )(h, w_gate, *ys, w_branch)


def _extract_topk(s, n_top, val_ref, idx_ref, slot, rid=None):
    if rid is None:
        rid = lax.broadcasted_iota(jnp.int32, s.shape, 0).astype(jnp.float32)
    for r in range(n_top):
        m = jnp.max(s, axis=0, keepdims=True)
        am = jnp.min(jnp.where(s == m, rid, jnp.inf), axis=0, keepdims=True)
        val_ref[slot, r:r + 1, :] = m
        idx_ref[slot, r:r + 1, :] = am
        s = jnp.where(rid == am, -jnp.inf, s)


CAND_COUNTS = tuple(PEER_TOPK // (j1 + 1) for j1 in range(PEER_TOPK))
N_CAND = sum(CAND_COUNTS)
N_CAND_ROWS = _round_up(N_CAND, SUBLANES)


def _cand_flat_ids(tt):
    ids = [j1 * PEER_TOPK + j2 for j1, n2 in enumerate(CAND_COUNTS) for j2 in range(n2)]
    ids += [PEER_TOPK * PEER_TOPK + p for p in range(N_CAND_ROWS - N_CAND)]
    return jnp.broadcast_to(jnp.asarray(ids, jnp.float32)[:, None], (N_CAND_ROWS, tt))


def _lookup_rows(table, sel):
    out = jnp.zeros(sel.shape, table.dtype)
    for r in range(table.shape[0]):
        out = jnp.where(sel == r, table[r:r + 1, :], out)
    return out


def _peer_topk_kernel(q_ref, keys_ref, flat_ref, i1_ref, i2_ref, w_ref, val_sc, idx_sc, cand_sc, top_sc, pos_sc,
                      ent_sc):
    n_half = keys_ref.shape[0]
    for hp in range(n_half):
        q = q_ref[:, hp * PEER_DKH:(hp + 1) * PEER_DKH]
        s = lax.dot_general(keys_ref[hp], q, (((1,), (1,)), ((), ())), precision=lax.Precision.HIGHEST,
                            preferred_element_type=jnp.float32)
        _extract_topk(s, PEER_TOPK, val_sc, idx_sc, hp)
    cand_sc[N_CAND_ROWS - SUBLANES:N_CAND_ROWS, :] = jnp.full((SUBLANES, cand_sc.shape[1]), -jnp.inf, jnp.float32)
    for h in range(PEER_HEADS):
        v1, v2 = val_sc[2 * h], val_sc[2 * h + 1]
        row0 = 0
        for j1, n2 in enumerate(CAND_COUNTS):
            cand_sc[row0:row0 + n2, :] = v1[j1:j1 + 1, :] + v2[0:n2, :]
            row0 += n2
        _extract_topk(cand_sc[...], PEER_TOPK, top_sc, pos_sc, 0, rid=flat_ref[...])
        top, pos = top_sc[0], pos_sc[0].astype(jnp.int32)
        e = jnp.exp(top - top[0:1, :])
        rows = slice(h * PEER_TOPK, (h + 1) * PEER_TOPK)
        ent_sc[0, rows, :] = _lookup_rows(idx_sc[2 * h], pos >> TOPK_SHIFT)
        ent_sc[1, rows, :] = _lookup_rows(idx_sc[2 * h + 1], pos & (PEER_TOPK - 1))
        ent_sc[2, rows, :] = e / jnp.sum(e, axis=0, keepdims=True)
    i1_ref[...] = ent_sc[0].T.astype(jnp.int32)
    i2_ref[...] = ent_sc[1].T.astype(jnp.int32)
    w_ref[...] = ent_sc[2].T


def peer_topk(q, keys, tt=LANES):
    n_tok = q.shape[0]
    n_ent = PEER_HEADS * PEER_TOPK
    assert n_ent == tt
    ent_spec = pl.BlockSpec((tt, n_ent), lambda i: (i, 0))
    f32, i32 = jnp.float32, jnp.int32
    return pl.pallas_call(
        _peer_topk_kernel,
        grid=(n_tok // tt,),
        in_specs=[pl.BlockSpec((tt, q.shape[1]), lambda i: (i, 0)),
                  pl.BlockSpec(keys.shape, lambda i: (0, 0, 0)),
                  pl.BlockSpec((N_CAND_ROWS, tt), lambda i: (0, 0))],
        out_specs=[ent_spec, ent_spec, ent_spec],
        out_shape=[jax.ShapeDtypeStruct((n_tok, n_ent), i32), jax.ShapeDtypeStruct((n_tok, n_ent), i32),
                   jax.ShapeDtypeStruct((n_tok, n_ent), f32)],
        scratch_shapes=[pltpu.VMEM((2 * PEER_HEADS, PEER_TOPK, tt), f32), pltpu.VMEM((2 * PEER_HEADS, PEER_TOPK, tt), f32),
                        pltpu.VMEM((N_CAND_ROWS, tt), f32),
                        pltpu.VMEM((1, PEER_TOPK, tt), f32), pltpu.VMEM((1, PEER_TOPK, tt), f32),
                        pltpu.VMEM((3, n_ent, tt), f32)],
        compiler_params=_params("parallel"),
        name="peer_topk",
    )(q, keys, _cand_flat_ids(tt))


def _peer_score_kernel(h_ref, u_ref, i1_ref, i2_ref, o_ref):
    j = pl.program_id(1)

    @pl.when(j == 0)
    def _():
        o_ref[...] = jnp.zeros_like(o_ref)

    s = lax.dot_general(h_ref[...], u_ref[...], (((1,), (1,)), ((), ())), preferred_element_type=jnp.float32)
    i1, i2 = i1_ref[...], i2_ref[...]
    acc = o_ref[...]
    n_chunks = s.shape[1] // PEER_NKEYS
    for c in range(n_chunks):
        picked = jnp.take_along_axis(s[:, c * PEER_NKEYS:(c + 1) * PEER_NKEYS], i2, axis=1)
        acc = jnp.where(i1 == j * n_chunks + c, picked, acc)
    o_ref[...] = acc


def peer_scores(h, u_tab, i1, i2, tn=1024):
    n_tok, d = h.shape
    n_exp = u_tab.shape[0]
    n_ent = i1.shape[1]
    tm = _row_tile(n_tok)
    assert n_exp % tn == 0 and n_ent == PEER_NKEYS
    ent_spec = pl.BlockSpec((tm, n_ent), lambda i, j: (i, 0))
    return pl.pallas_call(
        _peer_score_kernel,
        grid=(n_tok // tm, n_exp // tn),
        in_specs=[pl.BlockSpec((tm, d), lambda i, j: (i, 0)),
                  pl.BlockSpec((tn, d), lambda i, j: (j, 0)),
                  ent_spec, ent_spec],
        out_specs=ent_spec,
        out_shape=jax.ShapeDtypeStruct((n_tok, n_ent), jnp.float32),
        compiler_params=_params("parallel", "arbitrary"),
        name="peer_scores",
    )(h, u_tab, i1, i2)


def _peer_coef_kernel(sc_ref, w_ref, i1_ref, i2_ref, o_ref, wa_sc, ct_sc):
    wa_sc[...] = w_ref[...] * jax.nn.gelu(sc_ref[...])
    n_keys = PEER_NKEYS
    n_ent = sc_ref.shape[1]
    group = ct_sc.shape[0]
    key = lax.broadcasted_iota(jnp.int32, (n_keys, n_ent), 0)

    def body(gi, carry):
        t0 = pl.multiple_of(gi * group, group)
        for u in range(group):
            row = lambda ref: jnp.broadcast_to(ref[pl.ds(t0 + u, 1), :], (n_keys, n_ent))
            at = jnp.where(key == row(i1_ref), row(wa_sc), 0.0).astype(jnp.bfloat16)
            bt = jnp.where(key == row(i2_ref), 1.0, 0.0).astype(jnp.bfloat16)
            ct_sc[u] = lax.dot_general(at, bt, (((1,), (1,)), ((), ())), preferred_element_type=jnp.float32)
        by_key = jnp.swapaxes(ct_sc[...], 0, 1)
        for a in range(n_keys):
            o_ref[pl.ds(t0, group), a * n_keys:(a + 1) * n_keys] = by_key[a].astype(o_ref.dtype)
        return carry

    lax.fori_loop(0, sc_ref.shape[0] // group, body, 0)


def peer_coef(sc, wts, i1, i2, tb=128):
    n_tok, n_ent = sc.shape
    tb = min(tb, n_tok)
    assert n_tok % tb == 0 and tb % BF16_SUBLANES == 0
    ent_spec = pl.BlockSpec((tb, n_ent), lambda i: (i, 0))
    n_exp = PEER_NKEYS * PEER_NKEYS
    return pl.pallas_call(
        _peer_coef_kernel,
        grid=(n_tok // tb,),
        in_specs=[ent_spec] * 4,
        out_specs=pl.BlockSpec((tb, n_exp), lambda i: (i, 0)),
        out_shape=jax.ShapeDtypeStruct((n_tok, n_exp), jnp.bfloat16),
        scratch_shapes=[pltpu.VMEM((tb, n_ent), jnp.float32),
                        pltpu.VMEM((BF16_SUBLANES, PEER_NKEYS, PEER_NKEYS), jnp.float32)],
        compiler_params=_params("parallel"),
        name="peer_coef",
    )(sc, wts, i1, i2)


def peer_ffn(h, x, modtab, w_q, keys, u_tab, v_tab):
    q = mm(h, w_q, tm=_row_tile(h.shape[0]), tn=512, tk=D_MODEL)
    i1, i2, wts = peer_topk(q, keys.reshape(2 * PEER_HEADS, PEER_NKEYS, PEER_DKH))
    sc = peer_scores(h, u_tab, i1, i2)
    return mm_resid(peer_coef(sc, wts, i1, i2), v_tab, x, modtab, MOD_GATE2)


def _dft_parts(n):
    m = int(round(n ** 0.5))
    assert m * m == n
    part = jnp.arange(m, dtype=jnp.int32)[:, None]
    col = jnp.arange(n, dtype=jnp.int32)[None, :]
    ang_a = ((part * col) % m).astype(jnp.float32) * (2.0 * jnp.pi / m)
    ang_b = ((part * col) % n).astype(jnp.float32) * (2.0 * jnp.pi / n)
    ca, sa = jnp.cos(ang_a)[:, None, :], jnp.sin(ang_a)[:, None, :]
    cb, sb = jnp.cos(ang_b)[None, :, :], jnp.sin(ang_b)[None, :, :]
    scale = n ** -0.5
    return ((ca * cb - sa * sb) * scale).reshape(n, n), ((sa * cb + ca * sb) * scale).reshape(n, n)


def dft_tables():
    cc, sc = _dft_parts(FOURIER_GW)
    ct_l, st_l = _dft_parts(SEQ)
    ct_c, st_c = _dft_parts(CTX_LEN)
    bf16 = jnp.bfloat16
    return (jnp.concatenate([cc, sc], axis=1).astype(bf16),
            jnp.concatenate([ct_l, -st_l], axis=1).astype(bf16),
            jnp.concatenate([ct_c, -st_c], axis=1).astype(bf16))


def _fourier_kernel(p_ref, chan_ref, pos_ref, *rest):
    o_ref, gcs_sc = rest[-2:]
    seq = p_ref.shape[0]
    n_r = seq // o_ref.shape[0]
    r = pl.program_id(2)

    @pl.when(r == 0)
    def _():
        gc = jnp.dot(p_ref[...].astype(jnp.bfloat16), chan_ref[...], preferred_element_type=jnp.float32)
        gcs_sc[0:seq, :] = gc[:, :FOURIER_GW].astype(gcs_sc.dtype)
        gcs_sc[seq:2 * seq, :] = gc[:, FOURIER_GW:].astype(gcs_sc.dtype)

    @pl.when(r < n_r)
    def _():
        o_ref[...] = jnp.dot(pos_ref[...], gcs_sc[...], preferred_element_type=jnp.float32).astype(o_ref.dtype)

    if len(rest) == 3:
        @pl.when((r == n_r) & (pl.program_id(0) == 0))
        def _():
            o_ref[...] = rest[0][...]


def fourier_mixer(proj, tables, tr=ROW_BLOCK):
    chan, pos_lat, pos_ctx = tables
    gw = FOURIER_GW
    ctx_rows = BATCH * CTX_LEN
    assert ctx_rows == tr
    n_r = SEQ // tr
    chan_spec = pl.BlockSpec(chan.shape, lambda b, g, r: (0, 0))
    y_ctx = pl.pallas_call(
        _fourier_kernel,
        grid=(BATCH, FOURIER_GROUPS, 1),
        in_specs=[pl.BlockSpec((CTX_LEN, gw), lambda b, g, r: (LAT_ROWS // CTX_LEN + b, g)), chan_spec,
                  pl.BlockSpec(pos_ctx.shape, lambda b, g, r: (0, 0))],
        out_specs=pl.BlockSpec((CTX_LEN, gw), lambda b, g, r: (b, g)),
        out_shape=jax.ShapeDtypeStruct((ctx_rows, BRANCH_W), jnp.bfloat16),
        scratch_shapes=[pltpu.VMEM((2 * CTX_LEN, gw), jnp.bfloat16)],
        compiler_params=_params("parallel", "parallel", "arbitrary"),
        name="fourier_ctx",
    )(proj, chan, pos_ctx)
    return pl.pallas_call(
        _fourier_kernel,
        grid=(BATCH, FOURIER_GROUPS, n_r + 1),
        in_specs=[pl.BlockSpec((SEQ, gw), lambda b, g, r: (b, g)), chan_spec,
                  pl.BlockSpec((tr, 2 * SEQ), lambda b, g, r: (jnp.minimum(r, n_r - 1), 0)),
                  pl.BlockSpec((ctx_rows, gw), lambda b, g, r: (0, g))],
        out_specs=pl.BlockSpec((tr, gw), lambda b, g, r: (jnp.where((r == n_r) & (b == 0), BATCH * n_r,
                                                                    b * n_r + jnp.minimum(r, n_r - 1)), g)),
        out_shape=jax.ShapeDtypeStruct((ROWS, BRANCH_W), jnp.bfloat16),
        scratch_shapes=[pltpu.VMEM((2 * SEQ, gw), jnp.bfloat16)],
        compiler_params=_params("arbitrary", "arbitrary", "arbitrary"),
        name="fourier_lat",
    )(proj, chan, pos_lat, y_ctx)


def _mlstm_kernel(q_ref, k_ref, v_ref, gt_ref, o_ref, ct_sc, n_sc, m_sc):
    d = pl.program_id(0)
    head = pl.program_id(1) % MLSTM_HEADS
    f32, bf16 = jnp.float32, jnp.bfloat16
    n_t = q_ref.shape[0]

    @pl.when(pl.program_id(2) == 0)
    def _():
        ct_sc[...] = jnp.zeros_like(ct_sc)
        n_sc[...] = jnp.zeros_like(n_sc)
        m_sc[...] = jnp.zeros_like(m_sc)

    q = q_ref[...] * (MLSTM_DQK ** -0.5)
    k = k_ref[...]
    v = v_ref[...]
    gate0 = d * (2 * MLSTM_HEADS) + head
    li = gt_ref[pl.ds(gate0, 1), :]
    gf = gt_ref[pl.ds(gate0 + MLSTM_HEADS, 1), :]
    lf = -(jnp.log1p(jnp.exp(-jnp.abs(gf))) + jnp.maximum(-gf, 0.0))

    r = lax.broadcasted_iota(jnp.int32, (n_t, n_t), 0)
    c = lax.broadcasted_iota(jnp.int32, (n_t, n_t), 1)
    upto = (c - r) * (1 - 2 * d) <= 0
    eye = r == c
    b_col = jnp.sum(jnp.where(upto, lf, 0.0), axis=1, keepdims=True)
    b_row = jnp.sum(jnp.where(eye, b_col, 0.0), axis=0, keepdims=True)
    m_prev = m_sc[...]
    logw = jnp.where(upto, b_col - b_row + li, -jnp.inf)
    g_col = b_col + m_prev
    mt = jnp.maximum(g_col, jnp.max(logw, axis=1, keepdims=True))
    qb = q.astype(bf16)
    s = lax.dot_general(qb, k.astype(bf16), (((1,), (1,)), ((), ())), preferred_element_type=f32) * jnp.exp(logw - mt)
    w_inter = jnp.exp(g_col - mt)
    num = (jnp.dot(s.astype(bf16), v.astype(bf16), preferred_element_type=f32)
           + w_inter * jnp.dot(qb, ct_sc[...].astype(bf16), preferred_element_type=f32))
    den = jnp.sum(s, axis=1, keepdims=True) + w_inter * jnp.sum(q * n_sc[...], axis=1, keepdims=True)
    o_ref[0] = num / jnp.maximum(jnp.abs(den), jnp.exp(-mt))

    total = jnp.sum(lf, axis=1, keepdims=True)
    logu = total - b_row + li
    m_new = jnp.maximum(total + m_prev, jnp.max(logu, axis=1, keepdims=True))
    ws_row = jnp.exp(logu - m_new)
    wc = jnp.exp(total + m_prev - m_new)
    ws_col = jnp.sum(jnp.where(eye, ws_row, 0.0), axis=1, keepdims=True)
    kv = lax.dot_general(k.astype(bf16), (ws_col * v).astype(bf16), (((0,), (0,)), ((), ())),
                         preferred_element_type=f32)
    ct_sc[...] = wc * ct_sc[...] + kv
    n_sc[...] = wc * n_sc[...] + jnp.sum(ws_col * k, axis=0, keepdims=True)
    m_sc[...] = m_new


def _mlstm_chunk(d, bh, s):
    b = bh // MLSTM_HEADS
    n_ctx, n_lat = CTX_LEN // MLSTM_CHUNK, SEQ // MLSTM_CHUNK
    ctx_j = jnp.where(d == 0, s, n_ctx - 1 - s)
    lat_j = jnp.where(d == 0, s - n_ctx, n_ctx + n_lat - 1 - s)
    return jnp.where(s < n_ctx, BATCH * n_lat + b * n_ctx + ctx_j, b * n_lat + lat_j)


def mlstm_mixer(proj, gates_t):
    t = MLSTM_CHUNK
    q_col0 = BRANCH_W // MLSTM_DQK
    k_col0 = q_col0 + MLSTM_HEADS
    v_col0 = 2 * BRANCH_W // MLSTM_DV
    n_steps = (CTX_LEN + SEQ) // t
    return pl.pallas_call(
        _mlstm_kernel,
        grid=(2, BATCH * MLSTM_HEADS, n_steps),
        in_specs=[pl.BlockSpec((t, MLSTM_DQK), lambda d, bh, s: (_mlstm_chunk(d, bh, s), q_col0 + bh % MLSTM_HEADS)),
                  pl.BlockSpec((t, MLSTM_DQK), lambda d, bh, s: (_mlstm_chunk(d, bh, s), k_col0 + bh % MLSTM_HEADS)),
                  pl.BlockSpec((t, MLSTM_DV), lambda d, bh, s: (_mlstm_chunk(d, bh, s), v_col0 + bh % MLSTM_HEADS)),
                  pl.BlockSpec((N_MLSTM_GATES, t), lambda d, bh, s: (0, _mlstm_chunk(d, bh, s)))],
        out_specs=pl.BlockSpec((1, t, MLSTM_DV), lambda d, bh, s: (d, _mlstm_chunk(d, bh, s), bh % MLSTM_HEADS)),
        out_shape=jax.ShapeDtypeStruct((2, ROWS, BRANCH_W), jnp.float32),
        scratch_shapes=[pltpu.VMEM((MLSTM_DQK, MLSTM_DV), jnp.float32), pltpu.VMEM((1, MLSTM_DQK), jnp.float32),
                        pltpu.VMEM((1, 1), jnp.float32)],
        compiler_params=_params("parallel", "parallel", "arbitrary"),
        name="mlstm",
    )(proj, proj, proj, gates_t)


def _mlstm_out_kernel(hs_ref, og_ref, g_ref, o_ref):
    h = hs_ref[0] + hs_ref[1]
    for head in range(MLSTM_HEADS):
        cols = slice(head * MLSTM_DV, (head + 1) * MLSTM_DV)
        hh = h[:, cols]
        hn = hh * lax.rsqrt(jnp.mean(hh * hh, axis=-1, keepdims=True) + EPS)
        o_ref[:, cols] = (hn * g_ref[:, cols] * jax.nn.sigmoid(og_ref[:, cols])).astype(o_ref.dtype)


def mlstm_out(hs, proj, col_gate, norm_g, tm=SEQ_BLOCK):
    w = BRANCH_W
    return pl.pallas_call(
        _mlstm_out_kernel,
        grid=(ROWS // tm,),
        in_specs=[pl.BlockSpec((2, tm, w), lambda i: (0, i, 0)),
                  pl.BlockSpec((tm, w), lambda i: (i, col_gate)),
                  pl.BlockSpec((1, w), lambda i: (0, 0))],
        out_specs=pl.BlockSpec((tm, w), lambda i: (i, 0)),
        out_shape=jax.ShapeDtypeStruct((ROWS, w), jnp.bfloat16),
        compiler_params=_params("parallel"),
        name="mlstm_out",
    )(hs, proj, norm_g.reshape(1, w))


def kernel(x, c, ctx, c_ctx, w_mod, b_mod, g_norm1, g_norm2, w_in, w_branch, w_out, mlstm_gate_b, mlstm_norm_g,
           rg_conv_w, rg_conv_b, rg_wa, rg_ba, rg_wx, rg_bx, rg_lam, sc_conv_w, peer_wq, peer_keys, peer_u,
           peer_v, g_final):
    bf16 = jnp.bfloat16
    xs = jnp.concatenate([x.reshape(LAT_ROWS, D_MODEL), ctx.reshape(BATCH * CTX_LEN, D_MODEL)], axis=0)
    cond = jax.nn.silu(jnp.concatenate([c, c_ctx[None, :]], axis=0))
    tables = dft_tables()
    for l in range(DEPTH):
        last = l == DEPTH - 1
        modtab = (mm(cond, w_mod[l], tm=16, tn=2048, tk=2048) + b_mod[l]).reshape(BATCH + 1, 6, 1, D_MODEL)
        w_main = jnp.concatenate([w_in[l][:, :COL_MAIN], w_in[l][:, COL_GATES:COL_MERGE]], axis=1).astype(bf16)
        w_gates = jnp.pad(w_in[l][:, COL_MAIN:COL_GATES], ((0, 0), (0, LANES - N_MLSTM_GATES))).astype(bf16)
        w_merge = w_in[l][:, COL_MERGE:].astype(bf16)

        h = norm_mod(xs, g_norm1[l], modtab, MOD_SHIFT1, MOD_SCALE1)
        proj = mm(h, w_main, tm=_row_tile(ROWS), tn=512, tk=D_MODEL)
        gates = mm(h, w_gates, tm=_row_tile(ROWS), tk=D_MODEL)
        y_four = fourier_mixer(proj, tables)
        gates_t = gates[:, :N_MLSTM_GATES].T + mlstm_gate_b[l].reshape(N_MLSTM_GATES, 1)
        y_ml = mlstm_out(mlstm_mixer(proj, gates_t), proj, 3, mlstm_norm_g[l])
        y_rg = rglru_mixer(proj, 4, 5, rg_conv_w[l], rg_conv_b[l], rg_wa[l], rg_ba[l], rg_wx[l], rg_bx[l], rg_lam[l])
        y_sc = sconv_mixer(proj, 6, 7, 8, sc_conv_w[l])
        m = LAT_ROWS if last else ROWS
        merged = merge_branches(h, w_merge, (y_four, y_ml, y_rg, y_sc), w_branch[l].astype(bf16), m)
        xs = mm_resid(merged, w_out[l].astype(bf16), xs, modtab, MOD_GATE1)

        h2 = norm_mod(xs, g_norm2[l], modtab, MOD_SHIFT2, MOD_SCALE2)
        xs = peer_ffn(h2, xs, modtab, peer_wq[l].astype(bf16), peer_keys[l], peer_u[l].astype(bf16),
                      peer_v[l].astype(bf16))
    return rmsnorm_rows(xs, g_final).reshape(BATCH, SEQ, D_MODEL)
```

```python
import functools

import jax
import jax.numpy as jnp
from jax import lax
from jax.experimental import pallas as pl
from jax.experimental.pallas import tpu as pltpu

D_MODEL = 4096
BATCH = 2
SEQ = 4096
DEPTH = 2
CTX_LEN = 256
GRID_W = 64
N_BRANCHES = 4
BRANCH_W = D_MODEL // 4
FOURIER_GROUPS = 4
FOURIER_GW = BRANCH_W // FOURIER_GROUPS
MLSTM_HEADS = 4
MLSTM_DV = BRANCH_W // MLSTM_HEADS
MLSTM_DQK = MLSTM_DV // 2
MLSTM_CHUNK = 128
RG_BLOCKS = 8
RG_BW = BRANCH_W // RG_BLOCKS
RG_C = 8.0
RG_CONV_LEFT = 2
SC_CONV_LEFT = 1
PEER_HEADS = 8
PEER_NKEYS = 128
PEER_DK = 256
PEER_DKH = PEER_DK // 2
PEER_TOPK = 16
TOPK_SHIFT = PEER_TOPK.bit_length() - 1
EPS = 1e-6

N_MLSTM_GATES = 2 * 2 * MLSTM_HEADS
COL_MAIN = 4 * BRANCH_W
COL_GATES = COL_MAIN + N_MLSTM_GATES
COL_MERGE = COL_GATES + 5 * BRANCH_W

LAT_ROWS = BATCH * SEQ
ROWS = LAT_ROWS + BATCH * CTX_LEN
SEQ_BLOCK = CTX_LEN
LAT_BLOCKS = SEQ // SEQ_BLOCK
ROW_BLOCK = 512
SUBLANES = 8
BF16_SUBLANES = 16
LANES = 128
VMEM_LIMIT_BYTES = 48 * 1024 * 1024

MOD_SHIFT1, MOD_SCALE1, MOD_GATE1, MOD_SHIFT2, MOD_SCALE2, MOD_GATE2 = range(6)


def _round_up(x, m):
    return (x + m - 1) // m * m


def _params(*semantics):
    return pltpu.CompilerParams(dimension_semantics=semantics, vmem_limit_bytes=VMEM_LIMIT_BYTES)


def _segment(row_block, rows_per_block):
    return jnp.minimum(row_block // (SEQ // rows_per_block), BATCH)


def _mm_kernel(a_ref, b_ref, o_ref, acc_ref):
    @pl.when(pl.program_id(2) == 0)
    def _():
        acc_ref[...] = jnp.zeros_like(acc_ref)

    acc_ref[...] += jnp.dot(a_ref[...].astype(jnp.bfloat16), b_ref[...].astype(jnp.bfloat16),
                            preferred_element_type=jnp.float32)

    @pl.when(pl.program_id(2) == pl.num_programs(2) - 1)
    def _():
        o_ref[...] = acc_ref[...].astype(o_ref.dtype)


def _mm_fullk_kernel(a_ref, b_ref, o_ref):
    o_ref[...] = jnp.dot(a_ref[...].astype(jnp.bfloat16), b_ref[...].astype(jnp.bfloat16),
                         preferred_element_type=jnp.float32).astype(o_ref.dtype)


def mm(a, b, out_dtype=jnp.float32, tm=ROW_BLOCK, tn=1024, tk=2048, layer=None):
    m, k = a.shape
    n = b.shape[-1]
    assert (layer is None) == (b.ndim == 2)
    tm = min(tm, _round_up(m, 16))
    tn = min(tn, _round_up(n, LANES))
    tk = min(tk, k)
    mp, np_ = _round_up(m, tm), _round_up(n, tn)
    if mp != m:
        a = jnp.pad(a, ((0, mp - m), (0, 0)))
    if np_ != n:
        b = jnp.pad(b, ((0, 0),) * (b.ndim - 1) + ((0, np_ - n),))
    b_block = lambda rows, index: (pl.BlockSpec((rows, tn), index) if layer is None else
                                   pl.BlockSpec((None, rows, tn), lambda *g: (layer,) + index(*g)))
    if tk == k:
        out = pl.pallas_call(
            _mm_fullk_kernel,
            grid=(mp // tm, np_ // tn),
            in_specs=[pl.BlockSpec((tm, k), lambda i, j: (i, 0)),
                      b_block(k, lambda i, j: (0, j))],
            out_specs=pl.BlockSpec((tm, tn), lambda i, j: (i, j)),
            out_shape=jax.ShapeDtypeStruct((mp, np_), out_dtype),
            compiler_params=_params("parallel", "parallel"),
            name="mm_fullk",
        )(a, b)
    else:
        out = pl.pallas_call(
            _mm_kernel,
            grid=(mp // tm, np_ // tn, k // tk),
            in_specs=[pl.BlockSpec((tm, tk), lambda i, j, kk: (i, kk)),
                      b_block(tk, lambda i, j, kk: (kk, j))],
            out_specs=pl.BlockSpec((tm, tn), lambda i, j, kk: (i, j)),
            out_shape=jax.ShapeDtypeStruct((mp, np_), out_dtype),
            scratch_shapes=[pltpu.VMEM((tm, tn), jnp.float32)],
            compiler_params=_params("parallel", "parallel", "arbitrary"),
            name="mm",
        )(a, b)
    if mp != m or np_ != n:
        out = out[:m, :n]
    return out


def _mm_resid_kernel(a_ref, b_ref, x_ref, g_ref, o_ref, acc_ref):
    row_block = pl.program_id(0)

    @pl.when(pl.program_id(2) == 0)
    def _():
        acc_ref[...] = jnp.zeros_like(acc_ref)

    acc_ref[...] += jnp.dot(a_ref[...], b_ref[...], preferred_element_type=jnp.float32)

    @pl.when(pl.program_id(2) == pl.num_programs(2) - 1)
    def _():
        tm = o_ref.shape[0]
        row = row_block * tm + lax.broadcasted_iota(jnp.int32, (tm, 1), 0)
        gate = g_ref[0, 0]
        for seg in range(1, BATCH + 1):
            gate = jnp.where(row >= seg * SEQ, g_ref[seg, 0], gate)
        o_ref[...] = x_ref[...] + gate * acc_ref[...]


def mm_resid(a, b, x, modtab, which, tn=1024, tk=2048):
    m, k = a.shape
    _, n = b.shape
    tm = _row_tile(m)
    tn, tk = min(tn, n), min(tk, k)
    assert n % tn == 0 and k % tk == 0
    return pl.pallas_call(
        _mm_resid_kernel,
        grid=(m // tm, n // tn, k // tk),
        in_specs=[pl.BlockSpec((tm, tk), lambda i, j, kk: (i, kk)),
                  pl.BlockSpec((tk, tn), lambda i, j, kk: (kk, j)),
                  pl.BlockSpec((tm, tn), lambda i, j, kk: (i, j)),
                  pl.BlockSpec((BATCH + 1, 1, 1, tn), lambda i, j, kk: (0, which, 0, j))],
        out_specs=pl.BlockSpec((tm, tn), lambda i, j, kk: (i, j)),
        out_shape=jax.ShapeDtypeStruct((m, n), jnp.float32),
        scratch_shapes=[pltpu.VMEM((tm, tn), jnp.float32)],
        compiler_params=_params("parallel", "parallel", "arbitrary"),
        name="mm_resid",
    )(a, b, x, modtab)


def _norm_mod_kernel(x_ref, g_ref, sh_ref, sc_ref, o_ref):
    x = x_ref[...]
    y = x * lax.rsqrt(jnp.mean(x * x, axis=-1, keepdims=True) + EPS)
    o_ref[...] = ((y * g_ref[...]) * (1.0 + sc_ref[0, 0]) + sh_ref[0, 0]).astype(o_ref.dtype)


def norm_mod(x, g, modtab, which_shift, which_scale, tm=SEQ_BLOCK):
    m, d = x.shape
    mod_spec = lambda which: pl.BlockSpec((1, 1, 1, d), lambda i: (_segment(i, tm), which, 0, 0))
    return pl.pallas_call(
        _norm_mod_kernel,
        grid=(m // tm,),
        in_specs=[pl.BlockSpec((tm, d), lambda i: (i, 0)),
                  pl.BlockSpec((1, d), lambda i: (0, 0)),
                  mod_spec(which_shift), mod_spec(which_scale)],
        out_specs=pl.BlockSpec((tm, d), lambda i: (i, 0)),
        out_shape=jax.ShapeDtypeStruct((m, d), jnp.bfloat16),
        compiler_params=_params("parallel"),
        name="norm_mod",
    )(x, g.reshape(1, d), modtab, modtab)


def _rmsnorm_kernel(x_ref, g_ref, o_ref):
    x = x_ref[...]
    o_ref[...] = x * lax.rsqrt(jnp.mean(x * x, axis=-1, keepdims=True) + EPS) * g_ref[...]


def rmsnorm_rows(x, g, tm=SEQ_BLOCK):
    m, d = x.shape
    return pl.pallas_call(
        _rmsnorm_kernel,
        grid=(m // tm,),
        in_specs=[pl.BlockSpec((tm, d), lambda i: (i, 0)), pl.BlockSpec((1, d), lambda i: (0, 0))],
        out_specs=pl.BlockSpec((tm, d), lambda i: (i, 0)),
        out_shape=jax.ShapeDtypeStruct((m, d), jnp.float32),
        compiler_params=_params("parallel"),
        name="rmsnorm",
    )(x, g.reshape(1, d))


def _masked_conv(u, w_ref, pad_l, is_ctx):
    rows = u.shape[0]
    t = lax.broadcasted_iota(jnp.int32, (rows, 1), 0)
    seg = jnp.where(is_ctx, rows, GRID_W)
    pos = t & (seg - 1)
    y = None
    for j in range(w_ref.shape[0]):
        k = j - pad_l
        if k == 0:
            sh = u
        else:
            sh = pltpu.roll(u, (-k) % rows, axis=0)
            sh = jnp.where((pos + k >= 0) & (pos + k < seg), sh, 0.0)
        term = w_ref[j:j + 1, :] * sh
        y = term if y is None else y + term
    return y


def _rglru_kernel(*refs, reverse):
    if reverse:
        (p6_ref, cw_ref, cb_ref, wa_ref, ba_ref, wx_ref, bx_ref, lam_ref, hf_ref, p7_ref,
         o_ref, a_sc, b_sc, h_sc) = refs
    else:
        p6_ref, cw_ref, cb_ref, wa_ref, ba_ref, wx_ref, bx_ref, lam_ref, o_ref, a_sc, b_sc, h_sc = refs
    s = pl.program_id(1)

    @pl.when(s == 0)
    def _():
        h_sc[...] = jnp.zeros_like(h_sc)

    u = _masked_conv(p6_ref[...], cw_ref, RG_CONV_LEFT, s == 0) + cb_ref[...]
    ub = u.astype(jnp.bfloat16)
    for g in range(RG_BLOCKS):
        cols = slice(g * RG_BW, (g + 1) * RG_BW)
        ug = ub[:, cols]
        r = jax.nn.sigmoid(jnp.dot(ug, wa_ref[0, g].astype(jnp.bfloat16), preferred_element_type=jnp.float32)
                           + ba_ref[:, cols])
        i = jax.nn.sigmoid(jnp.dot(ug, wx_ref[0, g].astype(jnp.bfloat16), preferred_element_type=jnp.float32)
                           + bx_ref[:, cols])
        neg_lam = -lam_ref[:, cols]
        softplus = jnp.log1p(jnp.exp(-jnp.abs(neg_lam))) + jnp.maximum(neg_lam, 0.0)
        log_a = (-RG_C * softplus) * r
        a_sc[:, cols] = jnp.exp(log_a)
        b_sc[:, cols] = jnp.sqrt(1.0 - jnp.exp(2.0 * log_a)) * (i * u[:, cols])

    n_groups = a_sc.shape[0] // SUBLANES
    row = lax.broadcasted_iota(jnp.int32, (SUBLANES, a_sc.shape[1]), 0)

    def body(it, h_prev):
        grp = (n_groups - 1 - it) if reverse else it
        off = pl.multiple_of(grp * SUBLANES, SUBLANES)
        a = a_sc[pl.ds(off, SUBLANES), :]
        b = b_sc[pl.ds(off, SUBLANES), :]
        for k in (1, 2, 4):
            shift = (SUBLANES - k) if reverse else k
            inside = (row < SUBLANES - k) if reverse else (row >= k)
            a_s = jnp.where(inside, pltpu.roll(a, shift, axis=0), 1.0)
            b_s = jnp.where(inside, pltpu.roll(b, shift, axis=0), 0.0)
            b = a * b_s + b
            a = a * a_s
        h = b + a * h_prev
        b_sc[pl.ds(off, SUBLANES), :] = h
        return h[0:1] if reverse else h[SUBLANES - 1:SUBLANES]

    h_sc[...] = lax.fori_loop(0, n_groups, body, h_sc[...])
    if reverse:
        o_ref[...] = (jax.nn.gelu(p7_ref[...]) * (hf_ref[...] + b_sc[...])).astype(o_ref.dtype)
    else:
        o_ref[...] = b_sc[...]


def _seq_block(b, s, reverse):
    lat = (LAT_BLOCKS - s) if reverse else (s - 1)
    return jnp.where(s == 0, BATCH * LAT_BLOCKS + b, b * LAT_BLOCKS + lat)


def rglru_mixer(proj, col_in, col_gate, conv_w, conv_b, wa, ba, wx, bx, lam):
    w = BRANCH_W
    row2 = lambda a: a.reshape(1, w)
    outs = None
    for reverse in (False, True):
        d = int(reverse)
        blk = lambda col: (lambda b, s: (_seq_block(b, s, reverse), col))
        const2 = lambda b, s: (0, 0)
        in_specs = [pl.BlockSpec((SEQ_BLOCK, w), blk(col_in)),
                    pl.BlockSpec(conv_w.shape, const2),
                    pl.BlockSpec((1, w), const2),
                    pl.BlockSpec((1,) + wa.shape[1:], lambda b, s: (d, 0, 0, 0)),
                    pl.BlockSpec((1, w), const2),
                    pl.BlockSpec((1,) + wx.shape[1:], lambda b, s: (d, 0, 0, 0)),
                    pl.BlockSpec((1, w), const2),
                    pl.BlockSpec((1, w), const2)]
        args = [proj, conv_w, row2(conv_b), wa, row2(ba[d]), wx, row2(bx[d]), row2(lam[d])]
        if reverse:
            in_specs += [pl.BlockSpec((SEQ_BLOCK, w), blk(0)), pl.BlockSpec((SEQ_BLOCK, w), blk(col_gate))]
            args += [outs, proj]
        outs = pl.pallas_call(
            functools.partial(_rglru_kernel, reverse=reverse),
            grid=(BATCH, LAT_BLOCKS + 1),
            in_specs=in_specs,
            out_specs=pl.BlockSpec((SEQ_BLOCK, w), blk(0)),
            out_shape=jax.ShapeDtypeStruct((ROWS, w), jnp.bfloat16 if reverse else jnp.float32),
            scratch_shapes=[pltpu.VMEM((SEQ_BLOCK, w), jnp.float32), pltpu.VMEM((SEQ_BLOCK, w), jnp.float32),
                            pltpu.VMEM((1, w), jnp.float32)],
            compiler_params=_params("parallel", "arbitrary"),
            name="rglru_bwd" if reverse else "rglru_fwd",
        )(*args)
    return outs


def _sconv_kernel(pb_ref, pc_ref, px_ref, w_ref, o_ref):
    is_ctx = pl.program_id(0) >= BATCH * LAT_BLOCKS
    conv = _masked_conv(pc_ref[...] * px_ref[...], w_ref, SC_CONV_LEFT, is_ctx)
    o_ref[...] = (pb_ref[...] * conv).astype(o_ref.dtype)


def sconv_mixer(proj, col_b, col_c, col_x, conv_w):
    w = BRANCH_W
    spec = lambda col: pl.BlockSpec((SEQ_BLOCK, w), lambda i: (i, col))
    return pl.pallas_call(
        _sconv_kernel,
        grid=(ROWS // SEQ_BLOCK,),
        in_specs=[spec(col_b), spec(col_c), spec(col_x), pl.BlockSpec(conv_w.shape, lambda i: (0, 0))],
        out_specs=pl.BlockSpec((SEQ_BLOCK, w), lambda i: (i, 0)),
        out_shape=jax.ShapeDtypeStruct((ROWS, w), jnp.bfloat16),
        compiler_params=_params("parallel"),
        name="sconv",
    )(proj, proj, proj, conv_w)


def _merge_kernel(h_ref, wg_ref, *rest):
    y_refs, (wb_ref, o_ref, acc_ref) = rest[:N_BRANCHES], rest[N_BRANCHES:]
    b = pl.program_id(2)

    @pl.when(b == 0)
    def _():
        acc_ref[...] = jnp.zeros_like(acc_ref)

    gate = jax.nn.sigmoid(jnp.dot(h_ref[...], wg_ref[...], preferred_element_type=jnp.float32))
    y = y_refs[0][...]
    for branch in range(1, N_BRANCHES):
        y = jnp.where(b == branch, y_refs[branch][...], y)
    acc_ref[...] += gate * jnp.dot(y, wb_ref[0], preferred_element_type=jnp.float32)

    @pl.when(b == N_BRANCHES - 1)
    def _():
        o_ref[...] = acc_ref[...].astype(o_ref.dtype)


MAX_ROW_TILE = 1088


def _row_tile(m):
    return max(t for t in range(BF16_SUBLANES, MAX_ROW_TILE + 1, BF16_SUBLANES) if m % t == 0)


def merge_branches(h, w_gate, ys, w_branch, m):
    d = h.shape[1]
    bw = ys[0].shape[1]
    tm = 2 * ROW_BLOCK if m % (2 * ROW_BLOCK) == 0 else ROW_BLOCK
    tn = 1024 * ROW_BLOCK // tm
    n_col = d // tn
    y_spec = pl.BlockSpec((tm, bw), lambda i, j, b: (i, 0))
    return pl.pallas_call(
        _merge_kernel,
        grid=(m // tm, n_col, N_BRANCHES),
        in_specs=[pl.BlockSpec((tm, d), lambda i, j, b: (i, 0)),
                  pl.BlockSpec((d, tn), lambda i, j, b: (0, b * n_col + j))]
                 + [y_spec] * N_BRANCHES
                 + [pl.BlockSpec((1, bw, tn), lambda i, j, b: (b, 0, j))],
        out_specs=pl.BlockSpec((tm, tn), lambda i, j, b: (i, j)),
        out_shape=jax.ShapeDtypeStruct((m, d), jnp.bfloat16),
        scratch_shapes=[pltpu.VMEM((tm, tn), jnp.float32)],
        compiler_params=_params("parallel", "parallel", "arbitrary"),
        name="merge_branches",
    )(h, w_gate, *ys, w_branch)


def _extract_topk(s, n_top, val_ref, idx_ref, slot, rid=None):
    if rid is None:
        rid = lax.broadcasted_iota(jnp.int32, s.shape, 0).astype(jnp.float32)
    for r in range(n_top):
        m = jnp.max(s, axis=0, keepdims=True)
        am = jnp.min(jnp.where(s == m, rid, jnp.inf), axis=0, keepdims=True)
        val_ref[slot, r:r + 1, :] = m
        idx_ref[slot, r:r + 1, :] = am
        s = jnp.where(rid == am, -jnp.inf, s)


CAND_COUNTS = tuple(PEER_TOPK // (j1 + 1) for j1 in range(PEER_TOPK))
N_CAND = sum(CAND_COUNTS)
N_CAND_ROWS = _round_up(N_CAND, SUBLANES)


def _cand_flat_ids(tt):
    ids = [j1 * PEER_TOPK + j2 for j1, n2 in enumerate(CAND_COUNTS) for j2 in range(n2)]
    ids += [PEER_TOPK * PEER_TOPK + p for p in range(N_CAND_ROWS - N_CAND)]
    return jnp.broadcast_to(jnp.asarray(ids, jnp.float32)[:, None], (N_CAND_ROWS, tt))


def _lookup_rows(table, sel):
    out = jnp.zeros(sel.shape, table.dtype)
    for r in range(table.shape[0]):
        out = jnp.where(sel == r, table[r:r + 1, :], out)
    return out


def _peer_topk_kernel(q_ref, keys_ref, flat_ref, i1_ref, i2_ref, w_ref, val_sc, idx_sc, cand_sc, top_sc, pos_sc,
                      ent_sc):
    n_half = keys_ref.shape[0]
    for hp in range(n_half):
        q = q_ref[:, hp * PEER_DKH:(hp + 1) * PEER_DKH]
        s = lax.dot_general(keys_ref[hp], q, (((1,), (1,)), ((), ())), precision=lax.Precision.HIGHEST,
                            preferred_element_type=jnp.float32)
        _extract_topk(s, PEER_TOPK, val_sc, idx_sc, hp)
    cand_sc[N_CAND_ROWS - SUBLANES:N_CAND_ROWS, :] = jnp.full((SUBLANES, cand_sc.shape[1]), -jnp.inf, jnp.float32)
    for h in range(PEER_HEADS):
        v1, v2 = val_sc[2 * h], val_sc[2 * h + 1]
        row0 = 0
        for j1, n2 in enumerate(CAND_COUNTS):
            cand_sc[row0:row0 + n2, :] = v1[j1:j1 + 1, :] + v2[0:n2, :]
            row0 += n2
        _extract_topk(cand_sc[...], PEER_TOPK, top_sc, pos_sc, 0, rid=flat_ref[...])
        top, pos = top_sc[0], pos_sc[0].astype(jnp.int32)
        e = jnp.exp(top - top[0:1, :])
        rows = slice(h * PEER_TOPK, (h + 1) * PEER_TOPK)
        ent_sc[0, rows, :] = _lookup_rows(idx_sc[2 * h], pos >> TOPK_SHIFT)
        ent_sc[1, rows, :] = _lookup_rows(idx_sc[2 * h + 1], pos & (PEER_TOPK - 1))
        ent_sc[2, rows, :] = e / jnp.sum(e, axis=0, keepdims=True)
    i1_ref[...] = ent_sc[0].T.astype(jnp.int32)
    i2_ref[...] = ent_sc[1].T.astype(jnp.int32)
    w_ref[...] = ent_sc[2].T


def peer_topk(q, keys, tt=LANES):
    n_tok = q.shape[0]
    n_ent = PEER_HEADS * PEER_TOPK
    assert n_ent == tt
    ent_spec = pl.BlockSpec((tt, n_ent), lambda i: (i, 0))
    f32, i32 = jnp.float32, jnp.int32
    return pl.pallas_call(
        _peer_topk_kernel,
        grid=(n_tok // tt,),
        in_specs=[pl.BlockSpec((tt, q.shape[1]), lambda i: (i, 0)),
                  pl.BlockSpec(keys.shape, lambda i: (0, 0, 0)),
                  pl.BlockSpec((N_CAND_ROWS, tt), lambda i: (0, 0))],
        out_specs=[ent_spec, ent_spec, ent_spec],
        out_shape=[jax.ShapeDtypeStruct((n_tok, n_ent), i32), jax.ShapeDtypeStruct((n_tok, n_ent), i32),
                   jax.ShapeDtypeStruct((n_tok, n_ent), f32)],
        scratch_shapes=[pltpu.VMEM((2 * PEER_HEADS, PEER_TOPK, tt), f32), pltpu.VMEM((2 * PEER_HEADS, PEER_TOPK, tt), f32),
                        pltpu.VMEM((N_CAND_ROWS, tt), f32),
                        pltpu.VMEM((1, PEER_TOPK, tt), f32), pltpu.VMEM((1, PEER_TOPK, tt), f32),
                        pltpu.VMEM((3, n_ent, tt), f32)],
        compiler_params=_params("parallel"),
        name="peer_topk",
    )(q, keys, _cand_flat_ids(tt))


def _peer_score_kernel(h_ref, u_ref, i1_ref, i2_ref, o_ref):
    j = pl.program_id(1)

    @pl.when(j == 0)
    def _():
        o_ref[...] = jnp.zeros_like(o_ref)

    s = lax.dot_general(h_ref[...], u_ref[...], (((1,), (1,)), ((), ())), preferred_element_type=jnp.float32)
    i1, i2 = i1_ref[...], i2_ref[...]
    acc = o_ref[...]
    n_chunks = s.shape[1] // PEER_NKEYS
    for c in range(n_chunks):
        picked = jnp.take_along_axis(s[:, c * PEER_NKEYS:(c + 1) * PEER_NKEYS], i2, axis=1)
        acc = jnp.where(i1 == j * n_chunks + c, picked, acc)
    o_ref[...] = acc


def peer_scores(h, u_tab, i1, i2, tn=1024):
    n_tok, d = h.shape
    n_exp = u_tab.shape[0]
    n_ent = i1.shape[1]
    tm = _row_tile(n_tok)
    assert n_exp % tn == 0 and n_ent == PEER_NKEYS
    ent_spec = pl.BlockSpec((tm, n_ent), lambda i, j: (i, 0))
    return pl.pallas_call(
        _peer_score_kernel,
        grid=(n_tok // tm, n_exp // tn),
        in_specs=[pl.BlockSpec((tm, d), lambda i, j: (i, 0)),
                  pl.BlockSpec((tn, d), lambda i, j: (j, 0)),
                  ent_spec, ent_spec],
        out_specs=ent_spec,
        out_shape=jax.ShapeDtypeStruct((n_tok, n_ent), jnp.float32),
        compiler_params=_params("parallel", "arbitrary"),
        name="peer_scores",
    )(h, u_tab, i1, i2)


def _peer_coef_kernel(sc_ref, w_ref, i1_ref, i2_ref, o_ref, wa_sc, ct_sc):
    wa_sc[...] = w_ref[...] * jax.nn.gelu(sc_ref[...])
    n_keys = PEER_NKEYS
    n_ent = sc_ref.shape[1]
    group = ct_sc.shape[0]
    key = lax.broadcasted_iota(jnp.int32, (n_keys, n_ent), 0)

    def body(gi, carry):
        t0 = pl.multiple_of(gi * group, group)
        for u in range(group):
            row = lambda ref: jnp.broadcast_to(ref[pl.ds(t0 + u, 1), :], (n_keys, n_ent))
            at = jnp.where(key == row(i1_ref), row(wa_sc), 0.0).astype(jnp.bfloat16)
            bt = jnp.where(key == row(i2_ref), 1.0, 0.0).astype(jnp.bfloat16)
            ct_sc[u] = lax.dot_general(at, bt, (((1,), (1,)), ((), ())), preferred_element_type=jnp.float32)
        by_key = jnp.swapaxes(ct_sc[...], 0, 1)
        for a in range(n_keys):
            o_ref[pl.ds(t0, group), a * n_keys:(a + 1) * n_keys] = by_key[a].astype(o_ref.dtype)
        return carry

    lax.fori_loop(0, sc_ref.shape[0] // group, body, 0)


def peer_coef(sc, wts, i1, i2, tb=128):
    n_tok, n_ent = sc.shape
    tb = min(tb, n_tok)
    assert n_tok % tb == 0 and tb % BF16_SUBLANES == 0
    ent_spec = pl.BlockSpec((tb, n_ent), lambda i: (i, 0))
    n_exp = PEER_NKEYS * PEER_NKEYS
    return pl.pallas_call(
        _peer_coef_kernel,
        grid=(n_tok // tb,),
        in_specs=[ent_spec] * 4,
        out_specs=pl.BlockSpec((tb, n_exp), lambda i: (i, 0)),
        out_shape=jax.ShapeDtypeStruct((n_tok, n_exp), jnp.bfloat16),
        scratch_shapes=[pltpu.VMEM((tb, n_ent), jnp.float32),
                        pltpu.VMEM((BF16_SUBLANES, PEER_NKEYS, PEER_NKEYS), jnp.float32)],
        compiler_params=_params("parallel"),
        name="peer_coef",
    )(sc, wts, i1, i2)


def peer_ffn(h, x, modtab, w_q, keys, u_tab, v_tab):
    q = mm(h, w_q, tm=_row_tile(h.shape[0]), tn=512, tk=D_MODEL)
    i1, i2, wts = peer_topk(q, keys.reshape(2 * PEER_HEADS, PEER_NKEYS, PEER_DKH))
    sc = peer_scores(h, u_tab, i1, i2)
    return mm_resid(peer_coef(sc, wts, i1, i2), v_tab, x, modtab, MOD_GATE2)


def _dft_parts(n):
    m = int(round(n ** 0.5))
    assert m * m == n
    part = jnp.arange(m, dtype=jnp.int32)[:, None]
    col = jnp.arange(n, dtype=jnp.int32)[None, :]
    ang_a = ((part * col) % m).astype(jnp.float32) * (2.0 * jnp.pi / m)
    ang_b = ((part * col) % n).astype(jnp.float32) * (2.0 * jnp.pi / n)
    ca, sa = jnp.cos(ang_a)[:, None, :], jnp.sin(ang_a)[:, None, :]
    cb, sb = jnp.cos(ang_b)[None, :, :], jnp.sin(ang_b)[None, :, :]
    scale = n ** -0.5
    return ((ca * cb - sa * sb) * scale).reshape(n, n), ((sa * cb + ca * sb) * scale).reshape(n, n)


def dft_tables():
    cc, sc = _dft_parts(FOURIER_GW)
    ct_l, st_l = _dft_parts(SEQ)
    ct_c, st_c = _dft_parts(CTX_LEN)
    bf16 = jnp.bfloat16
    return (jnp.concatenate([cc, sc], axis=1).astype(bf16),
            jnp.concatenate([ct_l, -st_l], axis=1).astype(bf16),
            jnp.concatenate([ct_c, -st_c], axis=1).astype(bf16))


def _fourier_kernel(p_ref, chan_ref, pos_ref, *rest):
    o_ref, gcs_sc = rest[-2:]
    seq = p_ref.shape[0]
    n_r = seq // o_ref.shape[0]
    r = pl.program_id(2)

    @pl.when(r == 0)
    def _():
        gc = jnp.dot(p_ref[...].astype(jnp.bfloat16), chan_ref[...], preferred_element_type=jnp.float32)
        gcs_sc[0:seq, :] = gc[:, :FOURIER_GW].astype(gcs_sc.dtype)
        gcs_sc[seq:2 * seq, :] = gc[:, FOURIER_GW:].astype(gcs_sc.dtype)

    @pl.when(r < n_r)
    def _():
        o_ref[...] = jnp.dot(pos_ref[...], gcs_sc[...], preferred_element_type=jnp.float32).astype(o_ref.dtype)

    if len(rest) == 3:
        @pl.when((r == n_r) & (pl.program_id(0) == 0))
        def _():
            o_ref[...] = rest[0][...]


def fourier_mixer(proj, tables, tr=ROW_BLOCK):
    chan, pos_lat, pos_ctx = tables
    gw = FOURIER_GW
    ctx_rows = BATCH * CTX_LEN
    assert ctx_rows == tr
    n_r = SEQ // tr
    chan_spec = pl.BlockSpec(chan.shape, lambda b, g, r: (0, 0))
    y_ctx = pl.pallas_call(
        _fourier_kernel,
        grid=(BATCH, FOURIER_GROUPS, 1),
        in_specs=[pl.BlockSpec((CTX_LEN, gw), lambda b, g, r: (LAT_ROWS // CTX_LEN + b, g)), chan_spec,
                  pl.BlockSpec(pos_ctx.shape, lambda b, g, r: (0, 0))],
        out_specs=pl.BlockSpec((CTX_LEN, gw), lambda b, g, r: (b, g)),
        out_shape=jax.ShapeDtypeStruct((ctx_rows, BRANCH_W), jnp.bfloat16),
        scratch_shapes=[pltpu.VMEM((2 * CTX_LEN, gw), jnp.bfloat16)],
        compiler_params=_params("parallel", "parallel", "arbitrary"),
        name="fourier_ctx",
    )(proj, chan, pos_ctx)
    return pl.pallas_call(
        _fourier_kernel,
        grid=(BATCH, FOURIER_GROUPS, n_r + 1),
        in_specs=[pl.BlockSpec((SEQ, gw), lambda b, g, r: (b, g)), chan_spec,
                  pl.BlockSpec((tr, 2 * SEQ), lambda b, g, r: (jnp.minimum(r, n_r - 1), 0)),
                  pl.BlockSpec((ctx_rows, gw), lambda b, g, r: (0, g))],
        out_specs=pl.BlockSpec((tr, gw), lambda b, g, r: (jnp.where((r == n_r) & (b == 0), BATCH * n_r,
                                                                    b * n_r + jnp.minimum(r, n_r - 1)), g)),
        out_shape=jax.ShapeDtypeStruct((ROWS, BRANCH_W), jnp.bfloat16),
        scratch_shapes=[pltpu.VMEM((2 * SEQ, gw), jnp.bfloat16)],
        compiler_params=_params("arbitrary", "arbitrary", "arbitrary"),
        name="fourier_lat",
    )(proj, chan, pos_lat, y_ctx)


def _mlstm_kernel(q_ref, k_ref, v_ref, gt_ref, o_ref, ct_sc, n_sc, m_sc):
    d = pl.program_id(0)
    f32, bf16 = jnp.float32, jnp.bfloat16
    n_t = q_ref.shape[0]

    @pl.when(pl.program_id(2) == 0)
    def _():
        ct_sc[...] = jnp.zeros_like(ct_sc)
        n_sc[...] = jnp.zeros_like(n_sc)
        m_sc[...] = jnp.zeros_like(m_sc)

    r = lax.broadcasted_iota(jnp.int32, (n_t, n_t), 0)
    c = lax.broadcasted_iota(jnp.int32, (n_t, n_t), 1)
    upto = (c - r) * (1 - 2 * d) <= 0
    eye = r == c
    for head in range(MLSTM_HEADS):
        qk_cols = slice(head * MLSTM_DQK, (head + 1) * MLSTM_DQK)
        v_cols = slice(head * MLSTM_DV, (head + 1) * MLSTM_DV)
        q = q_ref[:, qk_cols] * (MLSTM_DQK ** -0.5)
        k = k_ref[:, qk_cols]
        v = v_ref[:, v_cols]
        gate0 = d * (2 * MLSTM_HEADS) + head
        li = gt_ref[pl.ds(gate0, 1), :]
        gf = gt_ref[pl.ds(gate0 + MLSTM_HEADS, 1), :]
        lf = -(jnp.log1p(jnp.exp(-jnp.abs(gf))) + jnp.maximum(-gf, 0.0))

        b_col = jnp.sum(jnp.where(upto, lf, 0.0), axis=1, keepdims=True)
        b_row = jnp.sum(jnp.where(eye, b_col, 0.0), axis=0, keepdims=True)
        m_prev = m_sc[head]
        logw = jnp.where(upto, b_col - b_row + li, -jnp.inf)
        g_col = b_col + m_prev
        mt = jnp.maximum(g_col, jnp.max(logw, axis=1, keepdims=True))
        qb = q.astype(bf16)
        s = (lax.dot_general(qb, k.astype(bf16), (((1,), (1,)), ((), ())), preferred_element_type=f32)
             * jnp.exp(logw - mt))
        w_inter = jnp.exp(g_col - mt)
        num = (jnp.dot(s.astype(bf16), v.astype(bf16), preferred_element_type=f32)
               + w_inter * jnp.dot(qb, ct_sc[head].astype(bf16), preferred_element_type=f32))
        den = jnp.sum(s, axis=1, keepdims=True) + w_inter * jnp.sum(q * n_sc[head], axis=1, keepdims=True)
        o_ref[0, :, v_cols] = num / jnp.maximum(jnp.abs(den), jnp.exp(-mt))

        total = jnp.sum(lf, axis=1, keepdims=True)
        logu = total - b_row + li
        m_new = jnp.maximum(total + m_prev, jnp.max(logu, axis=1, keepdims=True))
        ws_row = jnp.exp(logu - m_new)
        wc = jnp.exp(total + m_prev - m_new)
        ws_col = jnp.sum(jnp.where(eye, ws_row, 0.0), axis=1, keepdims=True)
        kv = lax.dot_general(k.astype(bf16), (ws_col * v).astype(bf16), (((0,), (0,)), ((), ())),
                             preferred_element_type=f32)
        ct_sc[head] = wc * ct_sc[head] + kv
        n_sc[head] = wc * n_sc[head] + jnp.sum(ws_col * k, axis=0, keepdims=True)
        m_sc[head] = m_new


def _mlstm_chunk(d, b, s):
    n_ctx, n_lat = CTX_LEN // MLSTM_CHUNK, SEQ // MLSTM_CHUNK
    ctx_j = jnp.where(d == 0, s, n_ctx - 1 - s)
    lat_j = jnp.where(d == 0, s - n_ctx, n_ctx + n_lat - 1 - s)
    return jnp.where(s < n_ctx, BATCH * n_lat + b * n_ctx + ctx_j, b * n_lat + lat_j)


def mlstm_mixer(proj, gates_t):
    t = MLSTM_CHUNK
    qk_w = MLSTM_HEADS * MLSTM_DQK
    q_col, k_col, v_col = BRANCH_W // qk_w, BRANCH_W // qk_w + 1, 2
    n_steps = (CTX_LEN + SEQ) // t
    f32 = jnp.float32
    return pl.pallas_call(
        _mlstm_kernel,
        grid=(2, BATCH, n_steps),
        in_specs=[pl.BlockSpec((t, qk_w), lambda d, b, s: (_mlstm_chunk(d, b, s), q_col)),
                  pl.BlockSpec((t, qk_w), lambda d, b, s: (_mlstm_chunk(d, b, s), k_col)),
                  pl.BlockSpec((t, BRANCH_W), lambda d, b, s: (_mlstm_chunk(d, b, s), v_col)),
                  pl.BlockSpec((N_MLSTM_GATES, t), lambda d, b, s: (0, _mlstm_chunk(d, b, s)))],
        out_specs=pl.BlockSpec((1, t, BRANCH_W), lambda d, b, s: (d, _mlstm_chunk(d, b, s), 0)),
        out_shape=jax.ShapeDtypeStruct((2, ROWS, BRANCH_W), f32),
        scratch_shapes=[pltpu.VMEM((MLSTM_HEADS, MLSTM_DQK, MLSTM_DV), f32), pltpu.VMEM((MLSTM_HEADS, 1, MLSTM_DQK), f32),
                        pltpu.VMEM((MLSTM_HEADS, 1, 1), f32)],
        compiler_params=_params("parallel", "parallel", "arbitrary"),
        name="mlstm",
    )(proj, proj, proj, gates_t)


def _mlstm_out_kernel(hs_ref, og_ref, g_ref, o_ref):
    h = hs_ref[0] + hs_ref[1]
    for head in range(MLSTM_HEADS):
        cols = slice(head * MLSTM_DV, (head + 1) * MLSTM_DV)
        hh = h[:, cols]
        hn = hh * lax.rsqrt(jnp.mean(hh * hh, axis=-1, keepdims=True) + EPS)
        o_ref[:, cols] = (hn * g_ref[:, cols] * jax.nn.sigmoid(og_ref[:, cols])).astype(o_ref.dtype)


def mlstm_out(hs, proj, col_gate, norm_g, tm=SEQ_BLOCK):
    w = BRANCH_W
    return pl.pallas_call(
        _mlstm_out_kernel,
        grid=(ROWS // tm,),
        in_specs=[pl.BlockSpec((2, tm, w), lambda i: (0, i, 0)),
                  pl.BlockSpec((tm, w), lambda i: (i, col_gate)),
                  pl.BlockSpec((1, w), lambda i: (0, 0))],
        out_specs=pl.BlockSpec((tm, w), lambda i: (i, 0)),
        out_shape=jax.ShapeDtypeStruct((ROWS, w), jnp.bfloat16),
        compiler_params=_params("parallel"),
        name="mlstm_out",
    )(hs, proj, norm_g.reshape(1, w))


def kernel(x, c, ctx, c_ctx, w_mod, b_mod, g_norm1, g_norm2, w_in, w_branch, w_out, mlstm_gate_b, mlstm_norm_g,
           rg_conv_w, rg_conv_b, rg_wa, rg_ba, rg_wx, rg_bx, rg_lam, sc_conv_w, peer_wq, peer_keys, peer_u,
           peer_v, g_final):
    bf16 = jnp.bfloat16
    xs = jnp.concatenate([x.reshape(LAT_ROWS, D_MODEL), ctx.reshape(BATCH * CTX_LEN, D_MODEL)], axis=0)
    cond = jax.nn.silu(jnp.concatenate([c, c_ctx[None, :]], axis=0))
    tables = dft_tables()
    for l in range(DEPTH):
        last = l == DEPTH - 1
        modtab = (mm(cond, w_mod, tm=16, tn=2048, tk=2048, layer=l) + b_mod[l]).reshape(BATCH + 1, 6, 1, D_MODEL)
        w_main = jnp.concatenate([w_in[l][:, :COL_MAIN], w_in[l][:, COL_GATES:COL_MERGE]], axis=1).astype(bf16)
        w_gates = jnp.pad(w_in[l][:, COL_MAIN:COL_GATES], ((0, 0), (0, LANES - N_MLSTM_GATES))).astype(bf16)
        w_merge = w_in[l][:, COL_MERGE:].astype(bf16)

        h = norm_mod(xs, g_norm1[l], modtab, MOD_SHIFT1, MOD_SCALE1)
        proj = mm(h, w_main, tm=_row_tile(ROWS), tn=512, tk=D_MODEL)
        gates = mm(h, w_gates, tm=_row_tile(ROWS), tk=D_MODEL)
        y_four = fourier_mixer(proj, tables)
        gates_t = gates[:, :N_MLSTM_GATES].T + mlstm_gate_b[l].reshape(N_MLSTM_GATES, 1)
        y_ml = mlstm_out(mlstm_mixer(proj, gates_t), proj, 3, mlstm_norm_g[l])
        y_rg = rglru_mixer(proj, 4, 5, rg_conv_w[l], rg_conv_b[l], rg_wa[l], rg_ba[l], rg_wx[l], rg_bx[l], rg_lam[l])
        y_sc = sconv_mixer(proj, 6, 7, 8, sc_conv_w[l])
        m = LAT_ROWS if last else ROWS
        merged = merge_branches(h, w_merge, (y_four, y_ml, y_rg, y_sc), w_branch[l].astype(bf16), m)
        xs = mm_resid(merged, w_out[l].astype(bf16), xs, modtab, MOD_GATE1)

        h2 = norm_mod(xs, g_norm2[l], modtab, MOD_SHIFT2, MOD_SCALE2)
        xs = peer_ffn(h2, xs, modtab, peer_wq[l].astype(bf16), peer_keys[l], peer_u[l].astype(bf16),
                      peer_v[l].astype(bf16))
    return rmsnorm_rows(xs, g_final).reshape(BATCH, SEQ, D_MODEL)
```

```python
import functools

import jax
import jax.numpy as jnp
from jax import lax
from jax.experimental import pallas as pl
from jax.experimental.pallas import tpu as pltpu

D_MODEL = 4096
BATCH = 2
SEQ = 4096
DEPTH = 2
CTX_LEN = 256
GRID_W = 64
N_BRANCHES = 4
BRANCH_W = D_MODEL // 4
FOURIER_GROUPS = 4
FOURIER_GW = BRANCH_W // FOURIER_GROUPS
MLSTM_HEADS = 4
MLSTM_DV = BRANCH_W // MLSTM_HEADS
MLSTM_DQK = MLSTM_DV // 2
MLSTM_CHUNK = 128
RG_BLOCKS = 8
RG_BW = BRANCH_W // RG_BLOCKS
RG_C = 8.0
RG_CONV_LEFT = 2
SC_CONV_LEFT = 1
PEER_HEADS = 8
PEER_NKEYS = 128
PEER_DK = 256
PEER_DKH = PEER_DK // 2
PEER_TOPK = 16
TOPK_SHIFT = PEER_TOPK.bit_length() - 1
EPS = 1e-6

N_MLSTM_GATES = 2 * 2 * MLSTM_HEADS
COL_MAIN = 4 * BRANCH_W
COL_GATES = COL_MAIN + N_MLSTM_GATES
COL_MERGE = COL_GATES + 5 * BRANCH_W

LAT_ROWS = BATCH * SEQ
ROWS = LAT_ROWS + BATCH * CTX_LEN
SEQ_BLOCK = CTX_LEN
LAT_BLOCKS = SEQ // SEQ_BLOCK
ROW_BLOCK = 512
SUBLANES = 8
BF16_SUBLANES = 16
LANES = 128
VMEM_LIMIT_BYTES = 48 * 1024 * 1024

MOD_SHIFT1, MOD_SCALE1, MOD_GATE1, MOD_SHIFT2, MOD_SCALE2, MOD_GATE2 = range(6)


def _round_up(x, m):
    return (x + m - 1) // m * m


def _params(*semantics):
    return pltpu.CompilerParams(dimension_semantics=semantics, vmem_limit_bytes=VMEM_LIMIT_BYTES)


def _segment(row_block, rows_per_block):
    return jnp.minimum(row_block // (SEQ // rows_per_block), BATCH)


def _mm_kernel(a_ref, b_ref, o_ref, acc_ref):
    @pl.when(pl.program_id(2) == 0)
    def _():
        acc_ref[...] = jnp.zeros_like(acc_ref)

    acc_ref[...] += jnp.dot(a_ref[...].astype(jnp.bfloat16), b_ref[...].astype(jnp.bfloat16),
                            preferred_element_type=jnp.float32)

    @pl.when(pl.program_id(2) == pl.num_programs(2) - 1)
    def _():
        o_ref[...] = acc_ref[...].astype(o_ref.dtype)


def _mm_fullk_kernel(a_ref, b_ref, o_ref):
    o_ref[...] = jnp.dot(a_ref[...].astype(jnp.bfloat16), b_ref[...].astype(jnp.bfloat16),
                         preferred_element_type=jnp.float32).astype(o_ref.dtype)


def mm(a, b, out_dtype=jnp.float32, tm=ROW_BLOCK, tn=1024, tk=2048, layer=None):
    m, k = a.shape
    n = b.shape[-1]
    assert (layer is None) == (b.ndim == 2)
    tm = min(tm, _round_up(m, 16))
    tn = min(tn, _round_up(n, LANES))
    tk = min(tk, k)
    mp, np_ = _round_up(m, tm), _round_up(n, tn)
    if mp != m:
        a = jnp.pad(a, ((0, mp - m), (0, 0)))
    if np_ != n:
        b = jnp.pad(b, ((0, 0),) * (b.ndim - 1) + ((0, np_ - n),))
    b_block = lambda rows, index: (pl.BlockSpec((rows, tn), index) if layer is None else
                                   pl.BlockSpec((None, rows, tn), lambda *g: (layer,) + index(*g)))
    if tk == k:
        out = pl.pallas_call(
            _mm_fullk_kernel,
            grid=(mp // tm, np_ // tn),
            in_specs=[pl.BlockSpec((tm, k), lambda i, j: (i, 0)),
                      b_block(k, lambda i, j: (0, j))],
            out_specs=pl.BlockSpec((tm, tn), lambda i, j: (i, j)),
            out_shape=jax.ShapeDtypeStruct((mp, np_), out_dtype),
            compiler_params=_params("parallel", "parallel"),
            name="mm_fullk",
        )(a, b)
    else:
        out = pl.pallas_call(
            _mm_kernel,
            grid=(mp // tm, np_ // tn, k // tk),
            in_specs=[pl.BlockSpec((tm, tk), lambda i, j, kk: (i, kk)),
                      b_block(tk, lambda i, j, kk: (kk, j))],
            out_specs=pl.BlockSpec((tm, tn), lambda i, j, kk: (i, j)),
            out_shape=jax.ShapeDtypeStruct((mp, np_), out_dtype),
            scratch_shapes=[pltpu.VMEM((tm, tn), jnp.float32)],
            compiler_params=_params("parallel", "parallel", "arbitrary"),
            name="mm",
        )(a, b)
    if mp != m or np_ != n:
        out = out[:m, :n]
    return out


def _mm_resid_kernel(a_ref, b_ref, x_ref, g_ref, o_ref, acc_ref):
    row_block = pl.program_id(0)

    @pl.when(pl.program_id(2) == 0)
    def _():
        acc_ref[...] = jnp.zeros_like(acc_ref)

    acc_ref[...] += jnp.dot(a_ref[...], b_ref[...].astype(jnp.bfloat16), preferred_element_type=jnp.float32)

    @pl.when(pl.program_id(2) == pl.num_programs(2) - 1)
    def _():
        tm = o_ref.shape[0]
        row = row_block * tm + lax.broadcasted_iota(jnp.int32, (tm, 1), 0)
        gate = g_ref[0, 0]
        for seg in range(1, BATCH + 1):
            gate = jnp.where(row >= seg * SEQ, g_ref[seg, 0], gate)
        o_ref[...] = x_ref[...] + gate * acc_ref[...]


def mm_resid(a, b, x, modtab, which, layer=None, tn=1024):
    m, k = a.shape
    n = b.shape[-1]
    assert (layer is None) == (b.ndim == 2)
    tm = _row_tile(m)
    tk = 2048 * 2 // b.dtype.itemsize
    tn, tk = min(tn, n), min(tk, k)
    assert n % tn == 0 and k % tk == 0
    b_index = lambda i, j, kk: (kk, j)
    b_spec = (pl.BlockSpec((tk, tn), b_index) if layer is None else
              pl.BlockSpec((None, tk, tn), lambda i, j, kk: (layer, kk, j)))
    return pl.pallas_call(
        _mm_resid_kernel,
        grid=(m // tm, n // tn, k // tk),
        in_specs=[pl.BlockSpec((tm, tk), lambda i, j, kk: (i, kk)),
                  b_spec,
                  pl.BlockSpec((tm, tn), lambda i, j, kk: (i, j)),
                  pl.BlockSpec((BATCH + 1, 1, 1, tn), lambda i, j, kk: (0, which, 0, j))],
        out_specs=pl.BlockSpec((tm, tn), lambda i, j, kk: (i, j)),
        out_shape=jax.ShapeDtypeStruct((m, n), jnp.float32),
        scratch_shapes=[pltpu.VMEM((tm, tn), jnp.float32)],
        compiler_params=_params("parallel", "parallel", "arbitrary"),
        name="mm_resid",
    )(a, b, x, modtab)


def _norm_mod_kernel(x_ref, g_ref, sh_ref, sc_ref, o_ref):
    x = x_ref[...]
    y = x * lax.rsqrt(jnp.mean(x * x, axis=-1, keepdims=True) + EPS)
    o_ref[...] = ((y * g_ref[...]) * (1.0 + sc_ref[0, 0]) + sh_ref[0, 0]).astype(o_ref.dtype)


def norm_mod(x, g, modtab, which_shift, which_scale, tm=SEQ_BLOCK):
    m, d = x.shape
    mod_spec = lambda which: pl.BlockSpec((1, 1, 1, d), lambda i: (_segment(i, tm), which, 0, 0))
    return pl.pallas_call(
        _norm_mod_kernel,
        grid=(m // tm,),
        in_specs=[pl.BlockSpec((tm, d), lambda i: (i, 0)),
                  pl.BlockSpec((1, d), lambda i: (0, 0)),
                  mod_spec(which_shift), mod_spec(which_scale)],
        out_specs=pl.BlockSpec((tm, d), lambda i: (i, 0)),
        out_shape=jax.ShapeDtypeStruct((m, d), jnp.bfloat16),
        compiler_params=_params("parallel"),
        name="norm_mod",
    )(x, g.reshape(1, d), modtab, modtab)


def _rmsnorm_kernel(x_ref, g_ref, o_ref):
    x = x_ref[...]
    o_ref[...] = x * lax.rsqrt(jnp.mean(x * x, axis=-1, keepdims=True) + EPS) * g_ref[...]


def rmsnorm_rows(x, g, tm=SEQ_BLOCK):
    m, d = x.shape
    return pl.pallas_call(
        _rmsnorm_kernel,
        grid=(m // tm,),
        in_specs=[pl.BlockSpec((tm, d), lambda i: (i, 0)), pl.BlockSpec((1, d), lambda i: (0, 0))],
        out_specs=pl.BlockSpec((tm, d), lambda i: (i, 0)),
        out_shape=jax.ShapeDtypeStruct((m, d), jnp.float32),
        compiler_params=_params("parallel"),
        name="rmsnorm",
    )(x, g.reshape(1, d))


def _masked_conv(u, w_ref, pad_l, is_ctx):
    rows = u.shape[0]
    t = lax.broadcasted_iota(jnp.int32, (rows, 1), 0)
    seg = jnp.where(is_ctx, rows, GRID_W)
    pos = t & (seg - 1)
    y = None
    for j in range(w_ref.shape[0]):
        k = j - pad_l
        if k == 0:
            sh = u
        else:
            sh = pltpu.roll(u, (-k) % rows, axis=0)
            sh = jnp.where((pos + k >= 0) & (pos + k < seg), sh, 0.0)
        term = w_ref[j:j + 1, :] * sh
        y = term if y is None else y + term
    return y


def _rglru_kernel(*refs, reverse):
    if reverse:
        (p6_ref, cw_ref, cb_ref, wa_ref, ba_ref, wx_ref, bx_ref, lam_ref, hf_ref, p7_ref,
         o_ref, a_sc, b_sc, h_sc) = refs
    else:
        p6_ref, cw_ref, cb_ref, wa_ref, ba_ref, wx_ref, bx_ref, lam_ref, o_ref, a_sc, b_sc, h_sc = refs
    s = pl.program_id(1)

    @pl.when(s == 0)
    def _():
        h_sc[...] = jnp.zeros_like(h_sc)

    u = _masked_conv(p6_ref[...], cw_ref, RG_CONV_LEFT, s == 0) + cb_ref[...]
    ub = u.astype(jnp.bfloat16)
    for g in range(RG_BLOCKS):
        cols = slice(g * RG_BW, (g + 1) * RG_BW)
        ug = ub[:, cols]
        r = jax.nn.sigmoid(jnp.dot(ug, wa_ref[0, g].astype(jnp.bfloat16), preferred_element_type=jnp.float32)
                           + ba_ref[:, cols])
        i = jax.nn.sigmoid(jnp.dot(ug, wx_ref[0, g].astype(jnp.bfloat16), preferred_element_type=jnp.float32)
                           + bx_ref[:, cols])
        neg_lam = -lam_ref[:, cols]
        softplus = jnp.log1p(jnp.exp(-jnp.abs(neg_lam))) + jnp.maximum(neg_lam, 0.0)
        log_a = (-RG_C * softplus) * r
        a_sc[:, cols] = jnp.exp(log_a)
        b_sc[:, cols] = jnp.sqrt(1.0 - jnp.exp(2.0 * log_a)) * (i * u[:, cols])

    n_groups = a_sc.shape[0] // SUBLANES
    row = lax.broadcasted_iota(jnp.int32, (SUBLANES, a_sc.shape[1]), 0)

    def body(it, h_prev):
        grp = (n_groups - 1 - it) if reverse else it
        off = pl.multiple_of(grp * SUBLANES, SUBLANES)
        a = a_sc[pl.ds(off, SUBLANES), :]
        b = b_sc[pl.ds(off, SUBLANES), :]
        for k in (1, 2, 4):
            shift = (SUBLANES - k) if reverse else k
            inside = (row < SUBLANES - k) if reverse else (row >= k)
            a_s = jnp.where(inside, pltpu.roll(a, shift, axis=0), 1.0)
            b_s = jnp.where(inside, pltpu.roll(b, shift, axis=0), 0.0)
            b = a * b_s + b
            a = a * a_s
        h = b + a * h_prev
        b_sc[pl.ds(off, SUBLANES), :] = h
        return h[0:1] if reverse else h[SUBLANES - 1:SUBLANES]

    h_sc[...] = lax.fori_loop(0, n_groups, body, h_sc[...])
    if reverse:
        o_ref[...] = (jax.nn.gelu(p7_ref[...]) * (hf_ref[...] + b_sc[...])).astype(o_ref.dtype)
    else:
        o_ref[...] = b_sc[...]


def _seq_block(b, s, reverse):
    lat = (LAT_BLOCKS - s) if reverse else (s - 1)
    return jnp.where(s == 0, BATCH * LAT_BLOCKS + b, b * LAT_BLOCKS + lat)


def rglru_mixer(proj, col_in, col_gate, conv_w, conv_b, wa, ba, wx, bx, lam):
    w = BRANCH_W
    row2 = lambda a: a.reshape(1, w)
    outs = None
    for reverse in (False, True):
        d = int(reverse)
        blk = lambda col: (lambda b, s: (_seq_block(b, s, reverse), col))
        const2 = lambda b, s: (0, 0)
        in_specs = [pl.BlockSpec((SEQ_BLOCK, w), blk(col_in)),
                    pl.BlockSpec(conv_w.shape, const2),
                    pl.BlockSpec((1, w), const2),
                    pl.BlockSpec((1,) + wa.shape[1:], lambda b, s: (d, 0, 0, 0)),
                    pl.BlockSpec((1, w), const2),
                    pl.BlockSpec((1,) + wx.shape[1:], lambda b, s: (d, 0, 0, 0)),
                    pl.BlockSpec((1, w), const2),
                    pl.BlockSpec((1, w), const2)]
        args = [proj, conv_w, row2(conv_b), wa, row2(ba[d]), wx, row2(bx[d]), row2(lam[d])]
        if reverse:
            in_specs += [pl.BlockSpec((SEQ_BLOCK, w), blk(0)), pl.BlockSpec((SEQ_BLOCK, w), blk(col_gate))]
            args += [outs, proj]
        outs = pl.pallas_call(
            functools.partial(_rglru_kernel, reverse=reverse),
            grid=(BATCH, LAT_BLOCKS + 1),
            in_specs=in_specs,
            out_specs=pl.BlockSpec((SEQ_BLOCK, w), blk(0)),
            out_shape=jax.ShapeDtypeStruct((ROWS, w), jnp.bfloat16 if reverse else jnp.float32),
            scratch_shapes=[pltpu.VMEM((SEQ_BLOCK, w), jnp.float32), pltpu.VMEM((SEQ_BLOCK, w), jnp.float32),
                            pltpu.VMEM((1, w), jnp.float32)],
            compiler_params=_params("parallel", "arbitrary"),
            name="rglru_bwd" if reverse else "rglru_fwd",
        )(*args)
    return outs


def _sconv_kernel(pb_ref, pc_ref, px_ref, w_ref, o_ref):
    is_ctx = pl.program_id(0) >= BATCH * LAT_BLOCKS
    conv = _masked_conv(pc_ref[...] * px_ref[...], w_ref, SC_CONV_LEFT, is_ctx)
    o_ref[...] = (pb_ref[...] * conv).astype(o_ref.dtype)


def sconv_mixer(proj, col_b, col_c, col_x, conv_w):
    w = BRANCH_W
    spec = lambda col: pl.BlockSpec((SEQ_BLOCK, w), lambda i: (i, col))
    return pl.pallas_call(
        _sconv_kernel,
        grid=(ROWS // SEQ_BLOCK,),
        in_specs=[spec(col_b), spec(col_c), spec(col_x), pl.BlockSpec(conv_w.shape, lambda i: (0, 0))],
        out_specs=pl.BlockSpec((SEQ_BLOCK, w), lambda i: (i, 0)),
        out_shape=jax.ShapeDtypeStruct((ROWS, w), jnp.bfloat16),
        compiler_params=_params("parallel"),
        name="sconv",
    )(proj, proj, proj, conv_w)


def _merge_kernel(h_ref, wg_ref, *rest):
    y_refs, (wb_ref, o_ref, acc_ref) = rest[:N_BRANCHES], rest[N_BRANCHES:]
    b = pl.program_id(2)

    @pl.when(b == 0)
    def _():
        acc_ref[...] = jnp.zeros_like(acc_ref)

    gate = jax.nn.sigmoid(jnp.dot(h_ref[...], wg_ref[...], preferred_element_type=jnp.float32))
    y = y_refs[0][...]
    for branch in range(1, N_BRANCHES):
        y = jnp.where(b == branch, y_refs[branch][...], y)
    acc_ref[...] += gate * jnp.dot(y, wb_ref[0], preferred_element_type=jnp.float32)

    @pl.when(b == N_BRANCHES - 1)
    def _():
        o_ref[...] = acc_ref[...].astype(o_ref.dtype)


MAX_ROW_TILE = 1088


def _row_tile(m):
    return max(t for t in range(BF16_SUBLANES, MAX_ROW_TILE + 1, BF16_SUBLANES) if m % t == 0)


def merge_branches(h, w_gate, ys, w_branch, m):
    d = h.shape[1]
    bw = ys[0].shape[1]
    tm = 2 * ROW_BLOCK if m % (2 * ROW_BLOCK) == 0 else ROW_BLOCK
    tn = 1024 * ROW_BLOCK // tm
    n_col = d // tn
    y_spec = pl.BlockSpec((tm, bw), lambda i, j, b: (i, 0))
    return pl.pallas_call(
        _merge_kernel,
        grid=(m // tm, n_col, N_BRANCHES),
        in_specs=[pl.BlockSpec((tm, d), lambda i, j, b: (i, 0)),
                  pl.BlockSpec((d, tn), lambda i, j, b: (0, b * n_col + j))]
                 + [y_spec] * N_BRANCHES
                 + [pl.BlockSpec((1, bw, tn), lambda i, j, b: (b, 0, j))],
        out_specs=pl.BlockSpec((tm, tn), lambda i, j, b: (i, j)),
        out_shape=jax.ShapeDtypeStruct((m, d), jnp.bfloat16),
        scratch_shapes=[pltpu.VMEM((tm, tn), jnp.float32)],
        compiler_params=_params("parallel", "parallel", "arbitrary"),
        name="merge_branches",
    )(h, w_gate, *ys, w_branch)


def _extract_topk(s, n_top, val_ref, idx_ref, slot, rid=None):
    if rid is None:
        rid = lax.broadcasted_iota(jnp.int32, s.shape, 0).astype(jnp.float32)
    for r in range(n_top):
        m = jnp.max(s, axis=0, keepdims=True)
        am = jnp.min(jnp.where(s == m, rid, jnp.inf), axis=0, keepdims=True)
        val_ref[slot, r:r + 1, :] = m
        idx_ref[slot, r:r + 1, :] = am
        s = jnp.where(rid == am, -jnp.inf, s)


CAND_COUNTS = tuple(PEER_TOPK // (j1 + 1) for j1 in range(PEER_TOPK))
N_CAND = sum(CAND_COUNTS)
N_CAND_ROWS = _round_up(N_CAND, SUBLANES)


def _cand_flat_ids(tt):
    ids = [j1 * PEER_TOPK + j2 for j1, n2 in enumerate(CAND_COUNTS) for j2 in range(n2)]
    ids += [PEER_TOPK * PEER_TOPK + p for p in range(N_CAND_ROWS - N_CAND)]
    return jnp.broadcast_to(jnp.asarray(ids, jnp.float32)[:, None], (N_CAND_ROWS, tt))


def _lookup_rows(table, sel):
    out = jnp.zeros(sel.shape, table.dtype)
    for r in range(table.shape[0]):
        out = jnp.where(sel == r, table[r:r + 1, :], out)
    return out


def _peer_topk_kernel(q_ref, keys_ref, flat_ref, i1_ref, i2_ref, w_ref, val_sc, idx_sc, cand_sc, top_sc, pos_sc,
                      ent_sc):
    n_half = keys_ref.shape[0]
    for hp in range(n_half):
        q = q_ref[:, hp * PEER_DKH:(hp + 1) * PEER_DKH]
        s = lax.dot_general(keys_ref[hp], q, (((1,), (1,)), ((), ())), precision=lax.Precision.HIGHEST,
                            preferred_element_type=jnp.float32)
        _extract_topk(s, PEER_TOPK, val_sc, idx_sc, hp)
    cand_sc[N_CAND_ROWS - SUBLANES:N_CAND_ROWS, :] = jnp.full((SUBLANES, cand_sc.shape[1]), -jnp.inf, jnp.float32)
    for h in range(PEER_HEADS):
        v1, v2 = val_sc[2 * h], val_sc[2 * h + 1]
        row0 = 0
        for j1, n2 in enumerate(CAND_COUNTS):
            cand_sc[row0:row0 + n2, :] = v1[j1:j1 + 1, :] + v2[0:n2, :]
            row0 += n2
        _extract_topk(cand_sc[...], PEER_TOPK, top_sc, pos_sc, 0, rid=flat_ref[...])
        top, pos = top_sc[0], pos_sc[0].astype(jnp.int32)
        e = jnp.exp(top - top[0:1, :])
        rows = slice(h * PEER_TOPK, (h + 1) * PEER_TOPK)
        ent_sc[0, rows, :] = _lookup_rows(idx_sc[2 * h], pos >> TOPK_SHIFT)
        ent_sc[1, rows, :] = _lookup_rows(idx_sc[2 * h + 1], pos & (PEER_TOPK - 1))
        ent_sc[2, rows, :] = e / jnp.sum(e, axis=0, keepdims=True)
    i1_ref[...] = ent_sc[0].T.astype(jnp.int32)
    i2_ref[...] = ent_sc[1].T.astype(jnp.int32)
    w_ref[...] = ent_sc[2].T


def peer_topk(q, keys, tt=LANES):
    n_tok = q.shape[0]
    n_ent = PEER_HEADS * PEER_TOPK
    assert n_ent == tt
    ent_spec = pl.BlockSpec((tt, n_ent), lambda i: (i, 0))
    f32, i32 = jnp.float32, jnp.int32
    return pl.pallas_call(
        _peer_topk_kernel,
        grid=(n_tok // tt,),
        in_specs=[pl.BlockSpec((tt, q.shape[1]), lambda i: (i, 0)),
                  pl.BlockSpec(keys.shape, lambda i: (0, 0, 0)),
                  pl.BlockSpec((N_CAND_ROWS, tt), lambda i: (0, 0))],
        out_specs=[ent_spec, ent_spec, ent_spec],
        out_shape=[jax.ShapeDtypeStruct((n_tok, n_ent), i32), jax.ShapeDtypeStruct((n_tok, n_ent), i32),
                   jax.ShapeDtypeStruct((n_tok, n_ent), f32)],
        scratch_shapes=[pltpu.VMEM((2 * PEER_HEADS, PEER_TOPK, tt), f32), pltpu.VMEM((2 * PEER_HEADS, PEER_TOPK, tt), f32),
                        pltpu.VMEM((N_CAND_ROWS, tt), f32),
                        pltpu.VMEM((1, PEER_TOPK, tt), f32), pltpu.VMEM((1, PEER_TOPK, tt), f32),
                        pltpu.VMEM((3, n_ent, tt), f32)],
        compiler_params=_params("parallel"),
        name="peer_topk",
    )(q, keys, _cand_flat_ids(tt))


def _peer_score_kernel(h_ref, u_ref, i1_ref, i2_ref, o_ref):
    j = pl.program_id(1)

    @pl.when(j == 0)
    def _():
        o_ref[...] = jnp.zeros_like(o_ref)

    s = lax.dot_general(h_ref[...], u_ref[...].astype(jnp.bfloat16), (((1,), (1,)), ((), ())),
                        preferred_element_type=jnp.float32)
    i1, i2 = i1_ref[...], i2_ref[...]
    acc = o_ref[...]
    n_chunks = s.shape[1] // PEER_NKEYS
    for c in range(n_chunks):
        picked = jnp.take_along_axis(s[:, c * PEER_NKEYS:(c + 1) * PEER_NKEYS], i2, axis=1)
        acc = jnp.where(i1 == j * n_chunks + c, picked, acc)
    o_ref[...] = acc


def peer_scores(h, u_tab, layer, i1, i2):
    n_tok, d = h.shape
    n_exp = u_tab.shape[1]
    n_ent = i1.shape[1]
    tm = _row_tile(n_tok)
    tn = 1024 * 2 // u_tab.dtype.itemsize
    assert n_exp % tn == 0 and n_ent == PEER_NKEYS
    ent_spec = pl.BlockSpec((tm, n_ent), lambda i, j: (i, 0))
    return pl.pallas_call(
        _peer_score_kernel,
        grid=(n_tok // tm, n_exp // tn),
        in_specs=[pl.BlockSpec((tm, d), lambda i, j: (i, 0)),
                  pl.BlockSpec((None, tn, d), lambda i, j: (layer, j, 0)),
                  ent_spec, ent_spec],
        out_specs=ent_spec,
        out_shape=jax.ShapeDtypeStruct((n_tok, n_ent), jnp.float32),
        compiler_params=_params("parallel", "arbitrary"),
        name="peer_scores",
    )(h, u_tab, i1, i2)


def _peer_coef_kernel(sc_ref, w_ref, i1_ref, i2_ref, o_ref, wa_sc, ct_sc):
    wa_sc[...] = w_ref[...] * jax.nn.gelu(sc_ref[...])
    n_keys = PEER_NKEYS
    n_ent = sc_ref.shape[1]
    group = ct_sc.shape[0]
    key = lax.broadcasted_iota(jnp.int32, (n_keys, n_ent), 0)

    def body(gi, carry):
        t0 = pl.multiple_of(gi * group, group)
        for u in range(group):
            row = lambda ref: jnp.broadcast_to(ref[pl.ds(t0 + u, 1), :], (n_keys, n_ent))
            at = jnp.where(key == row(i1_ref), row(wa_sc), 0.0).astype(jnp.bfloat16)
            bt = jnp.where(key == row(i2_ref), 1.0, 0.0).astype(jnp.bfloat16)
            ct_sc[u] = lax.dot_general(at, bt, (((1,), (1,)), ((), ())), preferred_element_type=jnp.float32)
        by_key = jnp.swapaxes(ct_sc[...], 0, 1)
        for a in range(n_keys):
            o_ref[pl.ds(t0, group), a * n_keys:(a + 1) * n_keys] = by_key[a].astype(o_ref.dtype)
        return carry

    lax.fori_loop(0, sc_ref.shape[0] // group, body, 0)


def peer_coef(sc, wts, i1, i2, tb=128):
    n_tok, n_ent = sc.shape
    tb = min(tb, n_tok)
    assert n_tok % tb == 0 and tb % BF16_SUBLANES == 0
    ent_spec = pl.BlockSpec((tb, n_ent), lambda i: (i, 0))
    n_exp = PEER_NKEYS * PEER_NKEYS
    return pl.pallas_call(
        _peer_coef_kernel,
        grid=(n_tok // tb,),
        in_specs=[ent_spec] * 4,
        out_specs=pl.BlockSpec((tb, n_exp), lambda i: (i, 0)),
        out_shape=jax.ShapeDtypeStruct((n_tok, n_exp), jnp.bfloat16),
        scratch_shapes=[pltpu.VMEM((tb, n_ent), jnp.float32),
                        pltpu.VMEM((BF16_SUBLANES, PEER_NKEYS, PEER_NKEYS), jnp.float32)],
        compiler_params=_params("parallel"),
        name="peer_coef",
    )(sc, wts, i1, i2)


def peer_ffn(h, x, modtab, layer, w_q, keys, u_tab, v_tab):
    q = mm(h, w_q, tm=_row_tile(h.shape[0]), tn=512, tk=D_MODEL, layer=layer)
    i1, i2, wts = peer_topk(q, keys.reshape(2 * PEER_HEADS, PEER_NKEYS, PEER_DKH))
    sc = peer_scores(h, u_tab, layer, i1, i2)
    return mm_resid(peer_coef(sc, wts, i1, i2), v_tab, x, modtab, MOD_GATE2, layer=layer)


def _dft_parts(n):
    m = int(round(n ** 0.5))
    assert m * m == n
    part = jnp.arange(m, dtype=jnp.int32)[:, None]
    col = jnp.arange(n, dtype=jnp.int32)[None, :]
    ang_a = ((part * col) % m).astype(jnp.float32) * (2.0 * jnp.pi / m)
    ang_b = ((part * col) % n).astype(jnp.float32) * (2.0 * jnp.pi / n)
    ca, sa = jnp.cos(ang_a)[:, None, :], jnp.sin(ang_a)[:, None, :]
    cb, sb = jnp.cos(ang_b)[None, :, :], jnp.sin(ang_b)[None, :, :]
    scale = n ** -0.5
    return ((ca * cb - sa * sb) * scale).reshape(n, n), ((sa * cb + ca * sb) * scale).reshape(n, n)


def dft_tables():
    cc, sc = _dft_parts(FOURIER_GW)
    ct_l, st_l = _dft_parts(SEQ)
    ct_c, st_c = _dft_parts(CTX_LEN)
    bf16 = jnp.bfloat16
    return (jnp.concatenate([cc, sc], axis=1).astype(bf16),
            jnp.concatenate([ct_l, -st_l], axis=1).astype(bf16),
            jnp.concatenate([ct_c, -st_c], axis=1).astype(bf16))


def _fourier_kernel(p_ref, chan_ref, pos_ref, *rest):
    o_ref, gcs_sc = rest[-2:]
    seq = p_ref.shape[0]
    n_r = seq // o_ref.shape[0]
    r = pl.program_id(2)

    @pl.when(r == 0)
    def _():
        gc = jnp.dot(p_ref[...].astype(jnp.bfloat16), chan_ref[...], preferred_element_type=jnp.float32)
        gcs_sc[0:seq, :] = gc[:, :FOURIER_GW].astype(gcs_sc.dtype)
        gcs_sc[seq:2 * seq, :] = gc[:, FOURIER_GW:].astype(gcs_sc.dtype)

    @pl.when(r < n_r)
    def _():
        o_ref[...] = jnp.dot(pos_ref[...], gcs_sc[...], preferred_element_type=jnp.float32).astype(o_ref.dtype)

    if len(rest) == 3:
        @pl.when((r == n_r) & (pl.program_id(0) == 0))
        def _():
            o_ref[...] = rest[0][...]


def fourier_mixer(proj, tables, tr=ROW_BLOCK):
    chan, pos_lat, pos_ctx = tables
    gw = FOURIER_GW
    ctx_rows = BATCH * CTX_LEN
    assert ctx_rows == tr
    n_r = SEQ // tr
    chan_spec = pl.BlockSpec(chan.shape, lambda b, g, r: (0, 0))
    y_ctx = pl.pallas_call(
        _fourier_kernel,
        grid=(BATCH, FOURIER_GROUPS, 1),
        in_specs=[pl.BlockSpec((CTX_LEN, gw), lambda b, g, r: (LAT_ROWS // CTX_LEN + b, g)), chan_spec,
                  pl.BlockSpec(pos_ctx.shape, lambda b, g, r: (0, 0))],
        out_specs=pl.BlockSpec((CTX_LEN, gw), lambda b, g, r: (b, g)),
        out_shape=jax.ShapeDtypeStruct((ctx_rows, BRANCH_W), jnp.bfloat16),
        scratch_shapes=[pltpu.VMEM((2 * CTX_LEN, gw), jnp.bfloat16)],
        compiler_params=_params("parallel", "parallel", "arbitrary"),
        name="fourier_ctx",
    )(proj, chan, pos_ctx)
    return pl.pallas_call(
        _fourier_kernel,
        grid=(BATCH, FOURIER_GROUPS, n_r + 1),
        in_specs=[pl.BlockSpec((SEQ, gw), lambda b, g, r: (b, g)), chan_spec,
                  pl.BlockSpec((tr, 2 * SEQ), lambda b, g, r: (jnp.minimum(r, n_r - 1), 0)),
                  pl.BlockSpec((ctx_rows, gw), lambda b, g, r: (0, g))],
        out_specs=pl.BlockSpec((tr, gw), lambda b, g, r: (jnp.where((r == n_r) & (b == 0), BATCH * n_r,
                                                                    b * n_r + jnp.minimum(r, n_r - 1)), g)),
        out_shape=jax.ShapeDtypeStruct((ROWS, BRANCH_W), jnp.bfloat16),
        scratch_shapes=[pltpu.VMEM((2 * SEQ, gw), jnp.bfloat16)],
        compiler_params=_params("arbitrary", "arbitrary", "arbitrary"),
        name="fourier_lat",
    )(proj, chan, pos_lat, y_ctx)


def _mlstm_kernel(q_ref, k_ref, v_ref, gt_ref, o_ref, ct_sc, n_sc, m_sc):
    d = pl.program_id(0)
    f32, bf16 = jnp.float32, jnp.bfloat16
    n_t = q_ref.shape[0]

    @pl.when(pl.program_id(2) == 0)
    def _():
        ct_sc[...] = jnp.zeros_like(ct_sc)
        n_sc[...] = jnp.zeros_like(n_sc)
        m_sc[...] = jnp.zeros_like(m_sc)

    r = lax.broadcasted_iota(jnp.int32, (n_t, n_t), 0)
    c = lax.broadcasted_iota(jnp.int32, (n_t, n_t), 1)
    upto = (c - r) * (1 - 2 * d) <= 0
    eye = r == c
    for head in range(MLSTM_HEADS):
        qk_cols = slice(head * MLSTM_DQK, (head + 1) * MLSTM_DQK)
        v_cols = slice(head * MLSTM_DV, (head + 1) * MLSTM_DV)
        q = q_ref[:, qk_cols] * (MLSTM_DQK ** -0.5)
        k = k_ref[:, qk_cols]
        v = v_ref[:, v_cols]
        gate0 = d * (2 * MLSTM_HEADS) + head
        li = gt_ref[pl.ds(gate0, 1), :]
        gf = gt_ref[pl.ds(gate0 + MLSTM_HEADS, 1), :]
        lf = -(jnp.log1p(jnp.exp(-jnp.abs(gf))) + jnp.maximum(-gf, 0.0))

        b_col = jnp.sum(jnp.where(upto, lf, 0.0), axis=1, keepdims=True)
        b_row = jnp.sum(jnp.where(eye, b_col, 0.0), axis=0, keepdims=True)
        m_prev = m_sc[head]
        logw = jnp.where(upto, b_col - b_row + li, -jnp.inf)
        g_col = b_col + m_prev
        mt = jnp.maximum(g_col, jnp.max(logw, axis=1, keepdims=True))
        qb = q.astype(bf16)
        s = (lax.dot_general(qb, k.astype(bf16), (((1,), (1,)), ((), ())), preferred_element_type=f32)
             * jnp.exp(logw - mt))
        w_inter = jnp.exp(g_col - mt)
        num = (jnp.dot(s.astype(bf16), v.astype(bf16), preferred_element_type=f32)
               + w_inter * jnp.dot(qb, ct_sc[head].astype(bf16), preferred_element_type=f32))
        den = jnp.sum(s, axis=1, keepdims=True) + w_inter * jnp.sum(q * n_sc[head], axis=1, keepdims=True)
        o_ref[0, :, v_cols] = num / jnp.maximum(jnp.abs(den), jnp.exp(-mt))

        total = jnp.sum(lf, axis=1, keepdims=True)
        logu = total - b_row + li
        m_new = jnp.maximum(total + m_prev, jnp.max(logu, axis=1, keepdims=True))
        ws_row = jnp.exp(logu - m_new)
        wc = jnp.exp(total + m_prev - m_new)
        ws_col = jnp.sum(jnp.where(eye, ws_row, 0.0), axis=1, keepdims=True)
        kv = lax.dot_general(k.astype(bf16), (ws_col * v).astype(bf16), (((0,), (0,)), ((), ())),
                             preferred_element_type=f32)
        ct_sc[head] = wc * ct_sc[head] + kv
        n_sc[head] = wc * n_sc[head] + jnp.sum(ws_col * k, axis=0, keepdims=True)
        m_sc[head] = m_new


def _mlstm_chunk(d, b, s):
    n_ctx, n_lat = CTX_LEN // MLSTM_CHUNK, SEQ // MLSTM_CHUNK
    ctx_j = jnp.where(d == 0, s, n_ctx - 1 - s)
    lat_j = jnp.where(d == 0, s - n_ctx, n_ctx + n_lat - 1 - s)
    return jnp.where(s < n_ctx, BATCH * n_lat + b * n_ctx + ctx_j, b * n_lat + lat_j)


def mlstm_mixer(proj, gates_t):
    t = MLSTM_CHUNK
    qk_w = MLSTM_HEADS * MLSTM_DQK
    q_col, k_col, v_col = BRANCH_W // qk_w, BRANCH_W // qk_w + 1, 2
    n_steps = (CTX_LEN + SEQ) // t
    f32 = jnp.float32
    return pl.pallas_call(
        _mlstm_kernel,
        grid=(2, BATCH, n_steps),
        in_specs=[pl.BlockSpec((t, qk_w), lambda d, b, s: (_mlstm_chunk(d, b, s), q_col)),
                  pl.BlockSpec((t, qk_w), lambda d, b, s: (_mlstm_chunk(d, b, s), k_col)),
                  pl.BlockSpec((t, BRANCH_W), lambda d, b, s: (_mlstm_chunk(d, b, s), v_col)),
                  pl.BlockSpec((N_MLSTM_GATES, t), lambda d, b, s: (0, _mlstm_chunk(d, b, s)))],
        out_specs=pl.BlockSpec((1, t, BRANCH_W), lambda d, b, s: (d, _mlstm_chunk(d, b, s), 0)),
        out_shape=jax.ShapeDtypeStruct((2, ROWS, BRANCH_W), f32),
        scratch_shapes=[pltpu.VMEM((MLSTM_HEADS, MLSTM_DQK, MLSTM_DV), f32), pltpu.VMEM((MLSTM_HEADS, 1, MLSTM_DQK), f32),
                        pltpu.VMEM((MLSTM_HEADS, 1, 1), f32)],
        compiler_params=_params("parallel", "parallel", "arbitrary"),
        name="mlstm",
    )(proj, proj, proj, gates_t)


def _mlstm_out_kernel(hs_ref, og_ref, g_ref, o_ref):
    h = hs_ref[0] + hs_ref[1]
    for head in range(MLSTM_HEADS):
        cols = slice(head * MLSTM_DV, (head + 1) * MLSTM_DV)
        hh = h[:, cols]
        hn = hh * lax.rsqrt(jnp.mean(hh * hh, axis=-1, keepdims=True) + EPS)
        o_ref[:, cols] = (hn * g_ref[:, cols] * jax.nn.sigmoid(og_ref[:, cols])).astype(o_ref.dtype)


def mlstm_out(hs, proj, col_gate, norm_g, tm=SEQ_BLOCK):
    w = BRANCH_W
    return pl.pallas_call(
        _mlstm_out_kernel,
        grid=(ROWS // tm,),
        in_specs=[pl.BlockSpec((2, tm, w), lambda i: (0, i, 0)),
                  pl.BlockSpec((tm, w), lambda i: (i, col_gate)),
                  pl.BlockSpec((1, w), lambda i: (0, 0))],
        out_specs=pl.BlockSpec((tm, w), lambda i: (i, 0)),
        out_shape=jax.ShapeDtypeStruct((ROWS, w), jnp.bfloat16),
        compiler_params=_params("parallel"),
        name="mlstm_out",
    )(hs, proj, norm_g.reshape(1, w))


def kernel(x, c, ctx, c_ctx, w_mod, b_mod, g_norm1, g_norm2, w_in, w_branch, w_out, mlstm_gate_b, mlstm_norm_g,
           rg_conv_w, rg_conv_b, rg_wa, rg_ba, rg_wx, rg_bx, rg_lam, sc_conv_w, peer_wq, peer_keys, peer_u,
           peer_v, g_final):
    bf16 = jnp.bfloat16
    xs = jnp.concatenate([x.reshape(LAT_ROWS, D_MODEL), ctx.reshape(BATCH * CTX_LEN, D_MODEL)], axis=0)
    cond = jax.nn.silu(jnp.concatenate([c, c_ctx[None, :]], axis=0))
    tables = dft_tables()
    w_out_b, peer_v_b = w_out.astype(bf16), peer_v.astype(bf16)
    for l in range(DEPTH):
        last = l == DEPTH - 1
        modtab = (mm(cond, w_mod, tm=16, tn=2048, tk=2048, layer=l) + b_mod[l]).reshape(BATCH + 1, 6, 1, D_MODEL)
        w_main = jnp.concatenate([w_in[l][:, :COL_MAIN], w_in[l][:, COL_GATES:COL_MERGE]], axis=1).astype(bf16)
        w_gates = jnp.pad(w_in[l][:, COL_MAIN:COL_GATES], ((0, 0), (0, LANES - N_MLSTM_GATES))).astype(bf16)
        w_merge = w_in[l][:, COL_MERGE:].astype(bf16)

        h = norm_mod(xs, g_norm1[l], modtab, MOD_SHIFT1, MOD_SCALE1)
        proj = mm(h, w_main, tm=_row_tile(ROWS), tn=512, tk=D_MODEL)
        gates = mm(h, w_gates, tm=_row_tile(ROWS), tk=D_MODEL)
        y_four = fourier_mixer(proj, tables)
        gates_t = gates[:, :N_MLSTM_GATES].T + mlstm_gate_b[l].reshape(N_MLSTM_GATES, 1)
        y_ml = mlstm_out(mlstm_mixer(proj, gates_t), proj, 3, mlstm_norm_g[l])
        y_rg = rglru_mixer(proj, 4, 5, rg_conv_w[l], rg_conv_b[l], rg_wa[l], rg_ba[l], rg_wx[l], rg_bx[l], rg_lam[l])
        y_sc = sconv_mixer(proj, 6, 7, 8, sc_conv_w[l])
        m = LAT_ROWS if last else ROWS
        merged = merge_branches(h, w_merge, (y_four, y_ml, y_rg, y_sc), w_branch[l].astype(bf16), m)
        xs = mm_resid(merged, w_out_b, xs, modtab, MOD_GATE1, layer=l)

        h2 = norm_mod(xs, g_norm2[l], modtab, MOD_SHIFT2, MOD_SCALE2)
        xs = peer_ffn(h2, xs, modtab, l, peer_wq, peer_keys[l], peer_u, peer_v_b)
    return rmsnorm_rows(xs, g_final).reshape(BATCH, SEQ, D_MODEL)
```

```python
import functools

import jax
import jax.numpy as jnp
from jax import lax
from jax.experimental import pallas as pl
from jax.experimental.pallas import tpu as pltpu

D_MODEL = 4096
BATCH = 2
SEQ = 4096
DEPTH = 2
CTX_LEN = 256
GRID_W = 64
N_BRANCHES = 4
BRANCH_W = D_MODEL // 4
FOURIER_GROUPS = 4
FOURIER_GW = BRANCH_W // FOURIER_GROUPS
MLSTM_HEADS = 4
MLSTM_DV = BRANCH_W // MLSTM_HEADS
MLSTM_DQK = MLSTM_DV // 2
MLSTM_CHUNK = 128
RG_BLOCKS = 8
RG_BW = BRANCH_W // RG_BLOCKS
RG_C = 8.0
RG_CONV_LEFT = 2
SC_CONV_LEFT = 1
PEER_HEADS = 8
PEER_NKEYS = 128
PEER_DK = 256
PEER_DKH = PEER_DK // 2
PEER_TOPK = 16
TOPK_SHIFT = PEER_TOPK.bit_length() - 1
EPS = 1e-6

N_MLSTM_GATES = 2 * 2 * MLSTM_HEADS
COL_MAIN = 4 * BRANCH_W
COL_GATES = COL_MAIN + N_MLSTM_GATES
COL_MERGE = COL_GATES + 5 * BRANCH_W

LAT_ROWS = BATCH * SEQ
ROWS = LAT_ROWS + BATCH * CTX_LEN
SEQ_BLOCK = CTX_LEN
LAT_BLOCKS = SEQ // SEQ_BLOCK
ROW_BLOCK = 512
SUBLANES = 8
BF16_SUBLANES = 16
LANES = 128
VMEM_LIMIT_BYTES = 48 * 1024 * 1024

MOD_SHIFT1, MOD_SCALE1, MOD_GATE1, MOD_SHIFT2, MOD_SCALE2, MOD_GATE2 = range(6)


def _round_up(x, m):
    return (x + m - 1) // m * m


def _params(*semantics):
    return pltpu.CompilerParams(dimension_semantics=semantics, vmem_limit_bytes=VMEM_LIMIT_BYTES)


def _segment(row_block, rows_per_block):
    return jnp.minimum(row_block // (SEQ // rows_per_block), BATCH)


def _mm_kernel(a_ref, b_ref, o_ref, acc_ref):
    @pl.when(pl.program_id(2) == 0)
    def _():
        acc_ref[...] = jnp.zeros_like(acc_ref)

    acc_ref[...] += jnp.dot(a_ref[...].astype(jnp.bfloat16), b_ref[...].astype(jnp.bfloat16),
                            preferred_element_type=jnp.float32)

    @pl.when(pl.program_id(2) == pl.num_programs(2) - 1)
    def _():
        o_ref[...] = acc_ref[...].astype(o_ref.dtype)


def _mm_fullk_kernel(a_ref, b_ref, o_ref):
    o_ref[...] = jnp.dot(a_ref[...].astype(jnp.bfloat16), b_ref[...].astype(jnp.bfloat16),
                         preferred_element_type=jnp.float32).astype(o_ref.dtype)


def mm(a, b, out_dtype=jnp.float32, tm=ROW_BLOCK, tn=1024, tk=2048, layer=None, n_cols=None):
    m, k = a.shape
    n = b.shape[-1] if n_cols is None else n_cols
    assert (layer is None) == (b.ndim == 2) and (n_cols is None or n_cols % tn == 0)
    tm = min(tm, _round_up(m, 16))
    tn = min(tn, _round_up(n, LANES))
    tk = min(tk, k)
    mp, np_ = _round_up(m, tm), _round_up(n, tn)
    if mp != m:
        a = jnp.pad(a, ((0, mp - m), (0, 0)))
    if np_ != n:
        b = jnp.pad(b, ((0, 0),) * (b.ndim - 1) + ((0, np_ - n),))
    b_block = lambda rows, index: (pl.BlockSpec((rows, tn), index) if layer is None else
                                   pl.BlockSpec((None, rows, tn), lambda *g: (layer,) + index(*g)))
    if tk == k:
        out = pl.pallas_call(
            _mm_fullk_kernel,
            grid=(mp // tm, np_ // tn),
            in_specs=[pl.BlockSpec((tm, k), lambda i, j: (i, 0)),
                      b_block(k, lambda i, j: (0, j))],
            out_specs=pl.BlockSpec((tm, tn), lambda i, j: (i, j)),
            out_shape=jax.ShapeDtypeStruct((mp, np_), out_dtype),
            compiler_params=_params("parallel", "parallel"),
            name="mm_fullk",
        )(a, b)
    else:
        out = pl.pallas_call(
            _mm_kernel,
            grid=(mp // tm, np_ // tn, k // tk),
            in_specs=[pl.BlockSpec((tm, tk), lambda i, j, kk: (i, kk)),
                      b_block(tk, lambda i, j, kk: (kk, j))],
            out_specs=pl.BlockSpec((tm, tn), lambda i, j, kk: (i, j)),
            out_shape=jax.ShapeDtypeStruct((mp, np_), out_dtype),
            scratch_shapes=[pltpu.VMEM((tm, tn), jnp.float32)],
            compiler_params=_params("parallel", "parallel", "arbitrary"),
            name="mm",
        )(a, b)
    if mp != m or np_ != n:
        out = out[:m, :n]
    return out


def _mm_resid_kernel(a_ref, b_ref, x_ref, g_ref, o_ref, acc_ref):
    row_block = pl.program_id(0)

    @pl.when(pl.program_id(2) == 0)
    def _():
        acc_ref[...] = jnp.zeros_like(acc_ref)

    acc_ref[...] += jnp.dot(a_ref[...], b_ref[...].astype(jnp.bfloat16), preferred_element_type=jnp.float32)

    @pl.when(pl.program_id(2) == pl.num_programs(2) - 1)
    def _():
        tm = o_ref.shape[0]
        row = row_block * tm + lax.broadcasted_iota(jnp.int32, (tm, 1), 0)
        gate = g_ref[0, 0]
        for seg in range(1, BATCH + 1):
            gate = jnp.where(row >= seg * SEQ, g_ref[seg, 0], gate)
        o_ref[...] = x_ref[...] + gate * acc_ref[...]


def mm_resid(a, b, x, modtab, which, layer=None, tn=1024):
    m, k = a.shape
    n = b.shape[-1]
    assert (layer is None) == (b.ndim == 2)
    tm = _row_tile(m)
    tk = 2048 * 2 // b.dtype.itemsize
    tn, tk = min(tn, n), min(tk, k)
    assert n % tn == 0 and k % tk == 0
    b_index = lambda i, j, kk: (kk, j)
    b_spec = (pl.BlockSpec((tk, tn), b_index) if layer is None else
              pl.BlockSpec((None, tk, tn), lambda i, j, kk: (layer, kk, j)))
    return pl.pallas_call(
        _mm_resid_kernel,
        grid=(m // tm, n // tn, k // tk),
        in_specs=[pl.BlockSpec((tm, tk), lambda i, j, kk: (i, kk)),
                  b_spec,
                  pl.BlockSpec((tm, tn), lambda i, j, kk: (i, j)),
                  pl.BlockSpec((BATCH + 1, 1, 1, tn), lambda i, j, kk: (0, which, 0, j))],
        out_specs=pl.BlockSpec((tm, tn), lambda i, j, kk: (i, j)),
        out_shape=jax.ShapeDtypeStruct((m, n), jnp.float32),
        scratch_shapes=[pltpu.VMEM((tm, tn), jnp.float32)],
        compiler_params=_params("parallel", "parallel", "arbitrary"),
        name="mm_resid",
    )(a, b, x, modtab)


def _norm_mod_kernel(x_ref, g_ref, sh_ref, sc_ref, o_ref):
    x = x_ref[...]
    y = x * lax.rsqrt(jnp.mean(x * x, axis=-1, keepdims=True) + EPS)
    o_ref[...] = ((y * g_ref[...]) * (1.0 + sc_ref[0, 0]) + sh_ref[0, 0]).astype(o_ref.dtype)


def norm_mod(x, g, modtab, which_shift, which_scale, tm=SEQ_BLOCK):
    m, d = x.shape
    mod_spec = lambda which: pl.BlockSpec((1, 1, 1, d), lambda i: (_segment(i, tm), which, 0, 0))
    return pl.pallas_call(
        _norm_mod_kernel,
        grid=(m // tm,),
        in_specs=[pl.BlockSpec((tm, d), lambda i: (i, 0)),
                  pl.BlockSpec((1, d), lambda i: (0, 0)),
                  mod_spec(which_shift), mod_spec(which_scale)],
        out_specs=pl.BlockSpec((tm, d), lambda i: (i, 0)),
        out_shape=jax.ShapeDtypeStruct((m, d), jnp.bfloat16),
        compiler_params=_params("parallel"),
        name="norm_mod",
    )(x, g.reshape(1, d), modtab, modtab)


def _rmsnorm_kernel(x_ref, g_ref, o_ref):
    x = x_ref[...]
    o_ref[...] = x * lax.rsqrt(jnp.mean(x * x, axis=-1, keepdims=True) + EPS) * g_ref[...]


def rmsnorm_rows(x, g, tm=SEQ_BLOCK):
    m, d = x.shape
    return pl.pallas_call(
        _rmsnorm_kernel,
        grid=(m // tm,),
        in_specs=[pl.BlockSpec((tm, d), lambda i: (i, 0)), pl.BlockSpec((1, d), lambda i: (0, 0))],
        out_specs=pl.BlockSpec((tm, d), lambda i: (i, 0)),
        out_shape=jax.ShapeDtypeStruct((m, d), jnp.float32),
        compiler_params=_params("parallel"),
        name="rmsnorm",
    )(x, g.reshape(1, d))


def _masked_conv(u, w_ref, pad_l, is_ctx):
    rows = u.shape[0]
    t = lax.broadcasted_iota(jnp.int32, (rows, 1), 0)
    seg = jnp.where(is_ctx, rows, GRID_W)
    pos = t & (seg - 1)
    y = None
    for j in range(w_ref.shape[0]):
        k = j - pad_l
        if k == 0:
            sh = u
        else:
            sh = pltpu.roll(u, (-k) % rows, axis=0)
            sh = jnp.where((pos + k >= 0) & (pos + k < seg), sh, 0.0)
        term = w_ref[j:j + 1, :] * sh
        y = term if y is None else y + term
    return y


def _rglru_kernel(*refs, reverse):
    if reverse:
        (p6_ref, cw_ref, cb_ref, wa_ref, ba_ref, wx_ref, bx_ref, lam_ref, hf_ref, p7_ref,
         o_ref, a_sc, b_sc, h_sc) = refs
    else:
        p6_ref, cw_ref, cb_ref, wa_ref, ba_ref, wx_ref, bx_ref, lam_ref, o_ref, a_sc, b_sc, h_sc = refs
    s = pl.program_id(1)

    @pl.when(s == 0)
    def _():
        h_sc[...] = jnp.zeros_like(h_sc)

    u = _masked_conv(p6_ref[...], cw_ref, RG_CONV_LEFT, s == 0) + cb_ref[...]
    ub = u.astype(jnp.bfloat16)
    for g in range(RG_BLOCKS):
        cols = slice(g * RG_BW, (g + 1) * RG_BW)
        ug = ub[:, cols]
        r = jax.nn.sigmoid(jnp.dot(ug, wa_ref[0, g].astype(jnp.bfloat16), preferred_element_type=jnp.float32)
                           + ba_ref[:, cols])
        i = jax.nn.sigmoid(jnp.dot(ug, wx_ref[0, g].astype(jnp.bfloat16), preferred_element_type=jnp.float32)
                           + bx_ref[:, cols])
        neg_lam = -lam_ref[:, cols]
        softplus = jnp.log1p(jnp.exp(-jnp.abs(neg_lam))) + jnp.maximum(neg_lam, 0.0)
        log_a = (-RG_C * softplus) * r
        a_sc[:, cols] = jnp.exp(log_a)
        b_sc[:, cols] = jnp.sqrt(1.0 - jnp.exp(2.0 * log_a)) * (i * u[:, cols])

    n_groups = a_sc.shape[0] // SUBLANES
    row = lax.broadcasted_iota(jnp.int32, (SUBLANES, a_sc.shape[1]), 0)

    def body(it, h_prev):
        grp = (n_groups - 1 - it) if reverse else it
        off = pl.multiple_of(grp * SUBLANES, SUBLANES)
        a = a_sc[pl.ds(off, SUBLANES), :]
        b = b_sc[pl.ds(off, SUBLANES), :]
        for k in (1, 2, 4):
            shift = (SUBLANES - k) if reverse else k
            inside = (row < SUBLANES - k) if reverse else (row >= k)
            a_s = jnp.where(inside, pltpu.roll(a, shift, axis=0), 1.0)
            b_s = jnp.where(inside, pltpu.roll(b, shift, axis=0), 0.0)
            b = a * b_s + b
            a = a * a_s
        h = b + a * h_prev
        b_sc[pl.ds(off, SUBLANES), :] = h
        return h[0:1] if reverse else h[SUBLANES - 1:SUBLANES]

    h_sc[...] = lax.fori_loop(0, n_groups, body, h_sc[...])
    if reverse:
        o_ref[...] = (jax.nn.gelu(p7_ref[...]) * (hf_ref[...] + b_sc[...])).astype(o_ref.dtype)
    else:
        o_ref[...] = b_sc[...]


def _seq_block(b, s, reverse):
    lat = (LAT_BLOCKS - s) if reverse else (s - 1)
    return jnp.where(s == 0, BATCH * LAT_BLOCKS + b, b * LAT_BLOCKS + lat)


def rglru_mixer(proj, col_in, col_gate, conv_w, conv_b, wa, ba, wx, bx, lam):
    w = BRANCH_W
    row2 = lambda a: a.reshape(1, w)
    outs = None
    for reverse in (False, True):
        d = int(reverse)
        blk = lambda col: (lambda b, s: (_seq_block(b, s, reverse), col))
        const2 = lambda b, s: (0, 0)
        in_specs = [pl.BlockSpec((SEQ_BLOCK, w), blk(col_in)),
                    pl.BlockSpec(conv_w.shape, const2),
                    pl.BlockSpec((1, w), const2),
                    pl.BlockSpec((1,) + wa.shape[1:], lambda b, s: (d, 0, 0, 0)),
                    pl.BlockSpec((1, w), const2),
                    pl.BlockSpec((1,) + wx.shape[1:], lambda b, s: (d, 0, 0, 0)),
                    pl.BlockSpec((1, w), const2),
                    pl.BlockSpec((1, w), const2)]
        args = [proj, conv_w, row2(conv_b), wa, row2(ba[d]), wx, row2(bx[d]), row2(lam[d])]
        if reverse:
            in_specs += [pl.BlockSpec((SEQ_BLOCK, w), blk(0)), pl.BlockSpec((SEQ_BLOCK, w), blk(col_gate))]
            args += [outs, proj]
        outs = pl.pallas_call(
            functools.partial(_rglru_kernel, reverse=reverse),
            grid=(BATCH, LAT_BLOCKS + 1),
            in_specs=in_specs,
            out_specs=pl.BlockSpec((SEQ_BLOCK, w), blk(0)),
            out_shape=jax.ShapeDtypeStruct((ROWS, w), jnp.bfloat16 if reverse else jnp.float32),
            scratch_shapes=[pltpu.VMEM((SEQ_BLOCK, w), jnp.float32), pltpu.VMEM((SEQ_BLOCK, w), jnp.float32),
                            pltpu.VMEM((1, w), jnp.float32)],
            compiler_params=_params("parallel", "arbitrary"),
            name="rglru_bwd" if reverse else "rglru_fwd",
        )(*args)
    return outs


def _sconv_kernel(pb_ref, pc_ref, px_ref, w_ref, o_ref):
    is_ctx = pl.program_id(0) >= BATCH * LAT_BLOCKS
    conv = _masked_conv(pc_ref[...] * px_ref[...], w_ref, SC_CONV_LEFT, is_ctx)
    o_ref[...] = (pb_ref[...] * conv).astype(o_ref.dtype)


def sconv_mixer(proj, col_b, col_c, col_x, conv_w):
    w = BRANCH_W
    spec = lambda col: pl.BlockSpec((SEQ_BLOCK, w), lambda i: (i, col))
    return pl.pallas_call(
        _sconv_kernel,
        grid=(ROWS // SEQ_BLOCK,),
        in_specs=[spec(col_b), spec(col_c), spec(col_x), pl.BlockSpec(conv_w.shape, lambda i: (0, 0))],
        out_specs=pl.BlockSpec((SEQ_BLOCK, w), lambda i: (i, 0)),
        out_shape=jax.ShapeDtypeStruct((ROWS, w), jnp.bfloat16),
        compiler_params=_params("parallel"),
        name="sconv",
    )(proj, proj, proj, conv_w)


W_ALL_MERGE = COL_MERGE - N_MLSTM_GATES
W_ALL_COLS = W_ALL_MERGE + N_BRANCHES * D_MODEL
PACK_TN = 1024


def _pack_w_in_kernel(main_ref, next_ref, o_ref):
    j = pl.program_id(2)
    x = jnp.concatenate([main_ref[...], next_ref[...]], axis=1)
    shifted = x[:, N_MLSTM_GATES:N_MLSTM_GATES + PACK_TN]
    o_ref[...] = jnp.where(j < COL_MAIN // PACK_TN, main_ref[...], shifted).astype(o_ref.dtype)


def pack_w_in(w_in, tk=1024):
    depth, k, _ = w_in.shape
    lane_blocks = PACK_TN // LANES
    return pl.pallas_call(
        _pack_w_in_kernel,
        grid=(depth, k // tk, W_ALL_COLS // PACK_TN),
        in_specs=[pl.BlockSpec((None, tk, PACK_TN), lambda l, i, j: (l, i, j)),
                  pl.BlockSpec((None, tk, LANES), lambda l, i, j: (l, i, lane_blocks * (j + 1)))],
        out_specs=pl.BlockSpec((None, tk, PACK_TN), lambda l, i, j: (l, i, j)),
        out_shape=jax.ShapeDtypeStruct((depth, k, W_ALL_COLS), jnp.bfloat16),
        compiler_params=_params("parallel", "parallel", "parallel"),
        name="pack_w_in",
    )(w_in, w_in)


def _merge_kernel(h_ref, wg_ref, *rest):
    y_refs, (wb_ref, o_ref, acc_ref) = rest[:N_BRANCHES], rest[N_BRANCHES:]
    b = pl.program_id(2)

    @pl.when(b == 0)
    def _():
        acc_ref[...] = jnp.zeros_like(acc_ref)

    gate = jax.nn.sigmoid(jnp.dot(h_ref[...], wg_ref[...], preferred_element_type=jnp.float32))
    y = y_refs[0][...]
    for branch in range(1, N_BRANCHES):
        y = jnp.where(b == branch, y_refs[branch][...], y)
    acc_ref[...] += gate * jnp.dot(y, wb_ref[...], preferred_element_type=jnp.float32)

    @pl.when(b == N_BRANCHES - 1)
    def _():
        o_ref[...] = acc_ref[...].astype(o_ref.dtype)


MAX_ROW_TILE = 1088


def _row_tile(m):
    return max(t for t in range(BF16_SUBLANES, MAX_ROW_TILE + 1, BF16_SUBLANES) if m % t == 0)


def merge_branches(h, w_all, layer, ys, w_branch, m):
    d = h.shape[1]
    bw = ys[0].shape[1]
    tm = 2 * ROW_BLOCK if m % (2 * ROW_BLOCK) == 0 else ROW_BLOCK
    tn = 1024 * ROW_BLOCK // tm
    n_col = d // tn
    col0 = W_ALL_MERGE // tn
    y_spec = pl.BlockSpec((tm, bw), lambda i, j, b: (i, 0))
    return pl.pallas_call(
        _merge_kernel,
        grid=(m // tm, n_col, N_BRANCHES),
        in_specs=[pl.BlockSpec((tm, d), lambda i, j, b: (i, 0)),
                  pl.BlockSpec((None, d, tn), lambda i, j, b: (layer, 0, col0 + b * n_col + j))]
                 + [y_spec] * N_BRANCHES
                 + [pl.BlockSpec((None, None, bw, tn), lambda i, j, b: (layer, b, 0, j))],
        out_specs=pl.BlockSpec((tm, tn), lambda i, j, b: (i, j)),
        out_shape=jax.ShapeDtypeStruct((m, d), jnp.bfloat16),
        scratch_shapes=[pltpu.VMEM((tm, tn), jnp.float32)],
        compiler_params=_params("parallel", "parallel", "arbitrary"),
        name="merge_branches",
    )(h, w_all, *ys, w_branch)


def _extract_topk(s, n_top, val_ref, idx_ref, slot, rid=None):
    if rid is None:
        rid = lax.broadcasted_iota(jnp.int32, s.shape, 0).astype(jnp.float32)
    for r in range(n_top):
        m = jnp.max(s, axis=0, keepdims=True)
        am = jnp.min(jnp.where(s == m, rid, jnp.inf), axis=0, keepdims=True)
        val_ref[slot, r:r + 1, :] = m
        idx_ref[slot, r:r + 1, :] = am
        s = jnp.where(rid == am, -jnp.inf, s)


CAND_COUNTS = tuple(PEER_TOPK // (j1 + 1) for j1 in range(PEER_TOPK))
N_CAND = sum(CAND_COUNTS)
N_CAND_ROWS = _round_up(N_CAND, SUBLANES)


def _cand_flat_ids(tt):
    ids = [j1 * PEER_TOPK + j2 for j1, n2 in enumerate(CAND_COUNTS) for j2 in range(n2)]
    ids += [PEER_TOPK * PEER_TOPK + p for p in range(N_CAND_ROWS - N_CAND)]
    return jnp.broadcast_to(jnp.asarray(ids, jnp.float32)[:, None], (N_CAND_ROWS, tt))


def _lookup_rows(table, sel):
    out = jnp.zeros(sel.shape, table.dtype)
    for r in range(table.shape[0]):
        out = jnp.where(sel == r, table[r:r + 1, :], out)
    return out


def _peer_topk_kernel(q_ref, keys_ref, flat_ref, i1_ref, i2_ref, w_ref, val_sc, idx_sc, cand_sc, top_sc, pos_sc,
                      ent_sc):
    n_half = keys_ref.shape[0]
    for hp in range(n_half):
        q = q_ref[:, hp * PEER_DKH:(hp + 1) * PEER_DKH]
        s = lax.dot_general(keys_ref[hp], q, (((1,), (1,)), ((), ())), precision=lax.Precision.HIGHEST,
                            preferred_element_type=jnp.float32)
        _extract_topk(s, PEER_TOPK, val_sc, idx_sc, hp)
    cand_sc[N_CAND_ROWS - SUBLANES:N_CAND_ROWS, :] = jnp.full((SUBLANES, cand_sc.shape[1]), -jnp.inf, jnp.float32)
    for h in range(PEER_HEADS):
        v1, v2 = val_sc[2 * h], val_sc[2 * h + 1]
        row0 = 0
        for j1, n2 in enumerate(CAND_COUNTS):
            cand_sc[row0:row0 + n2, :] = v1[j1:j1 + 1, :] + v2[0:n2, :]
            row0 += n2
        _extract_topk(cand_sc[...], PEER_TOPK, top_sc, pos_sc, 0, rid=flat_ref[...])
        top, pos = top_sc[0], pos_sc[0].astype(jnp.int32)
        e = jnp.exp(top - top[0:1, :])
        rows = slice(h * PEER_TOPK, (h + 1) * PEER_TOPK)
        ent_sc[0, rows, :] = _lookup_rows(idx_sc[2 * h], pos >> TOPK_SHIFT)
        ent_sc[1, rows, :] = _lookup_rows(idx_sc[2 * h + 1], pos & (PEER_TOPK - 1))
        ent_sc[2, rows, :] = e / jnp.sum(e, axis=0, keepdims=True)
    i1_ref[...] = ent_sc[0].T.astype(jnp.int32)
    i2_ref[...] = ent_sc[1].T.astype(jnp.int32)
    w_ref[...] = ent_sc[2].T


def peer_topk(q, keys, tt=LANES):
    n_tok = q.shape[0]
    n_ent = PEER_HEADS * PEER_TOPK
    assert n_ent == tt
    ent_spec = pl.BlockSpec((tt, n_ent), lambda i: (i, 0))
    f32, i32 = jnp.float32, jnp.int32
    return pl.pallas_call(
        _peer_topk_kernel,
        grid=(n_tok // tt,),
        in_specs=[pl.BlockSpec((tt, q.shape[1]), lambda i: (i, 0)),
                  pl.BlockSpec(keys.shape, lambda i: (0, 0, 0)),
                  pl.BlockSpec((N_CAND_ROWS, tt), lambda i: (0, 0))],
        out_specs=[ent_spec, ent_spec, ent_spec],
        out_shape=[jax.ShapeDtypeStruct((n_tok, n_ent), i32), jax.ShapeDtypeStruct((n_tok, n_ent), i32),
                   jax.ShapeDtypeStruct((n_tok, n_ent), f32)],
        scratch_shapes=[pltpu.VMEM((2 * PEER_HEADS, PEER_TOPK, tt), f32), pltpu.VMEM((2 * PEER_HEADS, PEER_TOPK, tt), f32),
                        pltpu.VMEM((N_CAND_ROWS, tt), f32),
                        pltpu.VMEM((1, PEER_TOPK, tt), f32), pltpu.VMEM((1, PEER_TOPK, tt), f32),
                        pltpu.VMEM((3, n_ent, tt), f32)],
        compiler_params=_params("parallel"),
        name="peer_topk",
    )(q, keys, _cand_flat_ids(tt))


def _peer_score_kernel(h_ref, u_ref, i1_ref, i2_ref, o_ref):
    j = pl.program_id(1)

    @pl.when(j == 0)
    def _():
        o_ref[...] = jnp.zeros_like(o_ref)

    s = lax.dot_general(h_ref[...], u_ref[...].astype(jnp.bfloat16), (((1,), (1,)), ((), ())),
                        preferred_element_type=jnp.float32)
    i1, i2 = i1_ref[...], i2_ref[...]
    acc = o_ref[...]
    n_chunks = s.shape[1] // PEER_NKEYS
    for c in range(n_chunks):
        picked = jnp.take_along_axis(s[:, c * PEER_NKEYS:(c + 1) * PEER_NKEYS], i2, axis=1)
        acc = jnp.where(i1 == j * n_chunks + c, picked, acc)
    o_ref[...] = acc


def peer_scores(h, u_tab, layer, i1, i2):
    n_tok, d = h.shape
    n_exp = u_tab.shape[1]
    n_ent = i1.shape[1]
    tm = _row_tile(n_tok)
    tn = 1024 * 2 // u_tab.dtype.itemsize
    assert n_exp % tn == 0 and n_ent == PEER_NKEYS
    ent_spec = pl.BlockSpec((tm, n_ent), lambda i, j: (i, 0))
    return pl.pallas_call(
        _peer_score_kernel,
        grid=(n_tok // tm, n_exp // tn),
        in_specs=[pl.BlockSpec((tm, d), lambda i, j: (i, 0)),
                  pl.BlockSpec((None, tn, d), lambda i, j: (layer, j, 0)),
                  ent_spec, ent_spec],
        out_specs=ent_spec,
        out_shape=jax.ShapeDtypeStruct((n_tok, n_ent), jnp.float32),
        compiler_params=_params("parallel", "arbitrary"),
        name="peer_scores",
    )(h, u_tab, i1, i2)


def _peer_coef_kernel(sc_ref, w_ref, i1_ref, i2_ref, o_ref, wa_sc, ct_sc):
    wa_sc[...] = w_ref[...] * jax.nn.gelu(sc_ref[...])
    n_keys = PEER_NKEYS
    n_ent = sc_ref.shape[1]
    group = ct_sc.shape[0]
    key = lax.broadcasted_iota(jnp.int32, (n_keys, n_ent), 0)

    def body(gi, carry):
        t0 = pl.multiple_of(gi * group, group)
        for u in range(group):
            row = lambda ref: jnp.broadcast_to(ref[pl.ds(t0 + u, 1), :], (n_keys, n_ent))
            at = jnp.where(key == row(i1_ref), row(wa_sc), 0.0).astype(jnp.bfloat16)
            bt = jnp.where(key == row(i2_ref), 1.0, 0.0).astype(jnp.bfloat16)
            ct_sc[u] = lax.dot_general(at, bt, (((1,), (1,)), ((), ())), preferred_element_type=jnp.float32)
        by_key = jnp.swapaxes(ct_sc[...], 0, 1)
        for a in range(n_keys):
            o_ref[pl.ds(t0, group), a * n_keys:(a + 1) * n_keys] = by_key[a].astype(o_ref.dtype)
        return carry

    lax.fori_loop(0, sc_ref.shape[0] // group, body, 0)


def peer_coef(sc, wts, i1, i2, tb=128):
    n_tok, n_ent = sc.shape
    tb = min(tb, n_tok)
    assert n_tok % tb == 0 and tb % BF16_SUBLANES == 0
    ent_spec = pl.BlockSpec((tb, n_ent), lambda i: (i, 0))
    n_exp = PEER_NKEYS * PEER_NKEYS
    return pl.pallas_call(
        _peer_coef_kernel,
        grid=(n_tok // tb,),
        in_specs=[ent_spec] * 4,
        out_specs=pl.BlockSpec((tb, n_exp), lambda i: (i, 0)),
        out_shape=jax.ShapeDtypeStruct((n_tok, n_exp), jnp.bfloat16),
        scratch_shapes=[pltpu.VMEM((tb, n_ent), jnp.float32),
                        pltpu.VMEM((BF16_SUBLANES, PEER_NKEYS, PEER_NKEYS), jnp.float32)],
        compiler_params=_params("parallel"),
        name="peer_coef",
    )(sc, wts, i1, i2)


def peer_ffn(h, x, modtab, layer, w_q, keys, u_tab, v_tab):
    q = mm(h, w_q, tm=_row_tile(h.shape[0]), tn=512, tk=D_MODEL, layer=layer)
    i1, i2, wts = peer_topk(q, keys.reshape(2 * PEER_HEADS, PEER_NKEYS, PEER_DKH))
    sc = peer_scores(h, u_tab, layer, i1, i2)
    return mm_resid(peer_coef(sc, wts, i1, i2), v_tab, x, modtab, MOD_GATE2, layer=layer)


def _dft_parts(n):
    m = int(round(n ** 0.5))
    assert m * m == n
    part = jnp.arange(m, dtype=jnp.int32)[:, None]
    col = jnp.arange(n, dtype=jnp.int32)[None, :]
    ang_a = ((part * col) % m).astype(jnp.float32) * (2.0 * jnp.pi / m)
    ang_b = ((part * col) % n).astype(jnp.float32) * (2.0 * jnp.pi / n)
    ca, sa = jnp.cos(ang_a)[:, None, :], jnp.sin(ang_a)[:, None, :]
    cb, sb = jnp.cos(ang_b)[None, :, :], jnp.sin(ang_b)[None, :, :]
    scale = n ** -0.5
    return ((ca * cb - sa * sb) * scale).reshape(n, n), ((sa * cb + ca * sb) * scale).reshape(n, n)


def dft_tables():
    cc, sc = _dft_parts(FOURIER_GW)
    ct_l, st_l = _dft_parts(SEQ)
    ct_c, st_c = _dft_parts(CTX_LEN)
    bf16 = jnp.bfloat16
    return (jnp.concatenate([cc, sc], axis=1).astype(bf16),
            jnp.concatenate([ct_l, -st_l], axis=1).astype(bf16),
            jnp.concatenate([ct_c, -st_c], axis=1).astype(bf16))


def _fourier_kernel(p_ref, chan_ref, pos_ref, *rest):
    o_ref, gcs_sc = rest[-2:]
    seq = p_ref.shape[0]
    n_r = seq // o_ref.shape[0]
    r = pl.program_id(2)

    @pl.when(r == 0)
    def _():
        gc = jnp.dot(p_ref[...].astype(jnp.bfloat16), chan_ref[...], preferred_element_type=jnp.float32)
        gcs_sc[0:seq, :] = gc[:, :FOURIER_GW].astype(gcs_sc.dtype)
        gcs_sc[seq:2 * seq, :] = gc[:, FOURIER_GW:].astype(gcs_sc.dtype)

    @pl.when(r < n_r)
    def _():
        o_ref[...] = jnp.dot(pos_ref[...], gcs_sc[...], preferred_element_type=jnp.float32).astype(o_ref.dtype)

    if len(rest) == 3:
        @pl.when((r == n_r) & (pl.program_id(0) == 0))
        def _():
            o_ref[...] = rest[0][...]


def fourier_mixer(proj, tables, tr=ROW_BLOCK):
    chan, pos_lat, pos_ctx = tables
    gw = FOURIER_GW
    ctx_rows = BATCH * CTX_LEN
    assert ctx_rows == tr
    n_r = SEQ // tr
    chan_spec = pl.BlockSpec(chan.shape, lambda b, g, r: (0, 0))
    y_ctx = pl.pallas_call(
        _fourier_kernel,
        grid=(BATCH, FOURIER_GROUPS, 1),
        in_specs=[pl.BlockSpec((CTX_LEN, gw), lambda b, g, r: (LAT_ROWS // CTX_LEN + b, g)), chan_spec,
                  pl.BlockSpec(pos_ctx.shape, lambda b, g, r: (0, 0))],
        out_specs=pl.BlockSpec((CTX_LEN, gw), lambda b, g, r: (b, g)),
        out_shape=jax.ShapeDtypeStruct((ctx_rows, BRANCH_W), jnp.bfloat16),
        scratch_shapes=[pltpu.VMEM((2 * CTX_LEN, gw), jnp.bfloat16)],
        compiler_params=_params("parallel", "parallel", "arbitrary"),
        name="fourier_ctx",
    )(proj, chan, pos_ctx)
    return pl.pallas_call(
        _fourier_kernel,
        grid=(BATCH, FOURIER_GROUPS, n_r + 1),
        in_specs=[pl.BlockSpec((SEQ, gw), lambda b, g, r: (b, g)), chan_spec,
                  pl.BlockSpec((tr, 2 * SEQ), lambda b, g, r: (jnp.minimum(r, n_r - 1), 0)),
                  pl.BlockSpec((ctx_rows, gw), lambda b, g, r: (0, g))],
        out_specs=pl.BlockSpec((tr, gw), lambda b, g, r: (jnp.where((r == n_r) & (b == 0), BATCH * n_r,
                                                                    b * n_r + jnp.minimum(r, n_r - 1)), g)),
        out_shape=jax.ShapeDtypeStruct((ROWS, BRANCH_W), jnp.bfloat16),
        scratch_shapes=[pltpu.VMEM((2 * SEQ, gw), jnp.bfloat16)],
        compiler_params=_params("arbitrary", "arbitrary", "arbitrary"),
        name="fourier_lat",
    )(proj, chan, pos_lat, y_ctx)


def _mlstm_kernel(q_ref, k_ref, v_ref, gt_ref, o_ref, ct_sc, n_sc, m_sc):
    d = pl.program_id(0)
    f32, bf16 = jnp.float32, jnp.bfloat16
    n_t = q_ref.shape[0]

    @pl.when(pl.program_id(2) == 0)
    def _():
        ct_sc[...] = jnp.zeros_like(ct_sc)
        n_sc[...] = jnp.zeros_like(n_sc)
        m_sc[...] = jnp.zeros_like(m_sc)

    r = lax.broadcasted_iota(jnp.int32, (n_t, n_t), 0)
    c = lax.broadcasted_iota(jnp.int32, (n_t, n_t), 1)
    upto = (c - r) * (1 - 2 * d) <= 0
    eye = r == c
    for head in range(MLSTM_HEADS):
        qk_cols = slice(head * MLSTM_DQK, (head + 1) * MLSTM_DQK)
        v_cols = slice(head * MLSTM_DV, (head + 1) * MLSTM_DV)
        q = q_ref[:, qk_cols] * (MLSTM_DQK ** -0.5)
        k = k_ref[:, qk_cols]
        v = v_ref[:, v_cols]
        gate0 = d * (2 * MLSTM_HEADS) + head
        li = gt_ref[pl.ds(gate0, 1), :]
        gf = gt_ref[pl.ds(gate0 + MLSTM_HEADS, 1), :]
        lf = -(jnp.log1p(jnp.exp(-jnp.abs(gf))) + jnp.maximum(-gf, 0.0))

        b_col = jnp.sum(jnp.where(upto, lf, 0.0), axis=1, keepdims=True)
        b_row = jnp.sum(jnp.where(eye, b_col, 0.0), axis=0, keepdims=True)
        m_prev = m_sc[head]
        logw = jnp.where(upto, b_col - b_row + li, -jnp.inf)
        g_col = b_col + m_prev
        mt = jnp.maximum(g_col, jnp.max(logw, axis=1, keepdims=True))
        qb = q.astype(bf16)
        s = (lax.dot_general(qb, k.astype(bf16), (((1,), (1,)), ((), ())), preferred_element_type=f32)
             * jnp.exp(logw - mt))
        w_inter = jnp.exp(g_col - mt)
        num = (jnp.dot(s.astype(bf16), v.astype(bf16), preferred_element_type=f32)
               + w_inter * jnp.dot(qb, ct_sc[head].astype(bf16), preferred_element_type=f32))
        den = jnp.sum(s, axis=1, keepdims=True) + w_inter * jnp.sum(q * n_sc[head], axis=1, keepdims=True)
        o_ref[0, :, v_cols] = num / jnp.maximum(jnp.abs(den), jnp.exp(-mt))

        total = jnp.sum(lf, axis=1, keepdims=True)
        logu = total - b_row + li
        m_new = jnp.maximum(total + m_prev, jnp.max(logu, axis=1, keepdims=True))
        ws_row = jnp.exp(logu - m_new)
        wc = jnp.exp(total + m_prev - m_new)
        ws_col = jnp.sum(jnp.where(eye, ws_row, 0.0), axis=1, keepdims=True)
        kv = lax.dot_general(k.astype(bf16), (ws_col * v).astype(bf16), (((0,), (0,)), ((), ())),
                             preferred_element_type=f32)
        ct_sc[head] = wc * ct_sc[head] + kv
        n_sc[head] = wc * n_sc[head] + jnp.sum(ws_col * k, axis=0, keepdims=True)
        m_sc[head] = m_new


def _mlstm_chunk(d, b, s):
    n_ctx, n_lat = CTX_LEN // MLSTM_CHUNK, SEQ // MLSTM_CHUNK
    ctx_j = jnp.where(d == 0, s, n_ctx - 1 - s)
    lat_j = jnp.where(d == 0, s - n_ctx, n_ctx + n_lat - 1 - s)
    return jnp.where(s < n_ctx, BATCH * n_lat + b * n_ctx + ctx_j, b * n_lat + lat_j)


def mlstm_mixer(proj, gates_t):
    t = MLSTM_CHUNK
    qk_w = MLSTM_HEADS * MLSTM_DQK
    q_col, k_col, v_col = BRANCH_W // qk_w, BRANCH_W // qk_w + 1, 2
    n_steps = (CTX_LEN + SEQ) // t
    f32 = jnp.float32
    return pl.pallas_call(
        _mlstm_kernel,
        grid=(2, BATCH, n_steps),
        in_specs=[pl.BlockSpec((t, qk_w), lambda d, b, s: (_mlstm_chunk(d, b, s), q_col)),
                  pl.BlockSpec((t, qk_w), lambda d, b, s: (_mlstm_chunk(d, b, s), k_col)),
                  pl.BlockSpec((t, BRANCH_W), lambda d, b, s: (_mlstm_chunk(d, b, s), v_col)),
                  pl.BlockSpec((N_MLSTM_GATES, t), lambda d, b, s: (0, _mlstm_chunk(d, b, s)))],
        out_specs=pl.BlockSpec((1, t, BRANCH_W), lambda d, b, s: (d, _mlstm_chunk(d, b, s), 0)),
        out_shape=jax.ShapeDtypeStruct((2, ROWS, BRANCH_W), f32),
        scratch_shapes=[pltpu.VMEM((MLSTM_HEADS, MLSTM_DQK, MLSTM_DV), f32), pltpu.VMEM((MLSTM_HEADS, 1, MLSTM_DQK), f32),
                        pltpu.VMEM((MLSTM_HEADS, 1, 1), f32)],
        compiler_params=_params("parallel", "parallel", "arbitrary"),
        name="mlstm",
    )(proj, proj, proj, gates_t)


def _mlstm_out_kernel(hs_ref, og_ref, g_ref, o_ref):
    h = hs_ref[0] + hs_ref[1]
    for head in range(MLSTM_HEADS):
        cols = slice(head * MLSTM_DV, (head + 1) * MLSTM_DV)
        hh = h[:, cols]
        hn = hh * lax.rsqrt(jnp.mean(hh * hh, axis=-1, keepdims=True) + EPS)
        o_ref[:, cols] = (hn * g_ref[:, cols] * jax.nn.sigmoid(og_ref[:, cols])).astype(o_ref.dtype)


def mlstm_out(hs, proj, col_gate, norm_g, tm=SEQ_BLOCK):
    w = BRANCH_W
    return pl.pallas_call(
        _mlstm_out_kernel,
        grid=(ROWS // tm,),
        in_specs=[pl.BlockSpec((2, tm, w), lambda i: (0, i, 0)),
                  pl.BlockSpec((tm, w), lambda i: (i, col_gate)),
                  pl.BlockSpec((1, w), lambda i: (0, 0))],
        out_specs=pl.BlockSpec((tm, w), lambda i: (i, 0)),
        out_shape=jax.ShapeDtypeStruct((ROWS, w), jnp.bfloat16),
        compiler_params=_params("parallel"),
        name="mlstm_out",
    )(hs, proj, norm_g.reshape(1, w))


def kernel(x, c, ctx, c_ctx, w_mod, b_mod, g_norm1, g_norm2, w_in, w_branch, w_out, mlstm_gate_b, mlstm_norm_g,
           rg_conv_w, rg_conv_b, rg_wa, rg_ba, rg_wx, rg_bx, rg_lam, sc_conv_w, peer_wq, peer_keys, peer_u,
           peer_v, g_final):
    bf16 = jnp.bfloat16
    xs = jnp.concatenate([x.reshape(LAT_ROWS, D_MODEL), ctx.reshape(BATCH * CTX_LEN, D_MODEL)], axis=0)
    cond = jax.nn.silu(jnp.concatenate([c, c_ctx[None, :]], axis=0))
    tables = dft_tables()
    w_out_b, peer_v_b, w_branch_b = w_out.astype(bf16), peer_v.astype(bf16), w_branch.astype(bf16)
    w_all = pack_w_in(w_in)
    w_gates = jnp.pad(w_in[:, :, COL_MAIN:COL_GATES], ((0, 0), (0, 0), (0, LANES - N_MLSTM_GATES))).astype(bf16)
    for l in range(DEPTH):
        last = l == DEPTH - 1
        modtab = (mm(cond, w_mod, tm=16, tn=2048, tk=2048, layer=l) + b_mod[l]).reshape(BATCH + 1, 6, 1, D_MODEL)

        h = norm_mod(xs, g_norm1[l], modtab, MOD_SHIFT1, MOD_SCALE1)
        proj = mm(h, w_all, tm=_row_tile(ROWS), tn=512, tk=D_MODEL, layer=l, n_cols=W_ALL_MERGE)
        gates = mm(h, w_gates, tm=_row_tile(ROWS), tk=D_MODEL, layer=l)
        y_four = fourier_mixer(proj, tables)
        gates_t = gates[:, :N_MLSTM_GATES].T + mlstm_gate_b[l].reshape(N_MLSTM_GATES, 1)
        y_ml = mlstm_out(mlstm_mixer(proj, gates_t), proj, 3, mlstm_norm_g[l])
        y_rg = rglru_mixer(proj, 4, 5, rg_conv_w[l], rg_conv_b[l], rg_wa[l], rg_ba[l], rg_wx[l], rg_bx[l], rg_lam[l])
        y_sc = sconv_mixer(proj, 6, 7, 8, sc_conv_w[l])
        m = LAT_ROWS if last else ROWS
        merged = merge_branches(h, w_all, l, (y_four, y_ml, y_rg, y_sc), w_branch_b, m)
        xs = mm_resid(merged, w_out_b, xs, modtab, MOD_GATE1, layer=l)

        h2 = norm_mod(xs, g_norm2[l], modtab, MOD_SHIFT2, MOD_SCALE2)
        xs = peer_ffn(h2, xs, modtab, l, peer_wq, peer_keys[l], peer_u, peer_v_b)
    return rmsnorm_rows(xs, g_final).reshape(BATCH, SEQ, D_MODEL)
```

```python
import functools

import jax
import jax.numpy as jnp
from jax import lax
from jax.experimental import pallas as pl
from jax.experimental.pallas import tpu as pltpu

D_MODEL = 4096
BATCH = 2
SEQ = 4096
DEPTH = 2
CTX_LEN = 256
GRID_W = 64
N_BRANCHES = 4
BRANCH_W = D_MODEL // 4
FOURIER_GROUPS = 4
FOURIER_GW = BRANCH_W // FOURIER_GROUPS
MLSTM_HEADS = 4
MLSTM_DV = BRANCH_W // MLSTM_HEADS
MLSTM_DQK = MLSTM_DV // 2
MLSTM_CHUNK = 128
RG_BLOCKS = 8
RG_BW = BRANCH_W // RG_BLOCKS
RG_C = 8.0
RG_CONV_LEFT = 2
SC_CONV_LEFT = 1
PEER_HEADS = 8
PEER_NKEYS = 128
PEER_DK = 256
PEER_DKH = PEER_DK // 2
PEER_TOPK = 16
TOPK_SHIFT = PEER_TOPK.bit_length() - 1
EPS = 1e-6

N_MLSTM_GATES = 2 * 2 * MLSTM_HEADS
COL_MAIN = 4 * BRANCH_W
COL_GATES = COL_MAIN + N_MLSTM_GATES
COL_MERGE = COL_GATES + 5 * BRANCH_W

LAT_ROWS = BATCH * SEQ
ROWS = LAT_ROWS + BATCH * CTX_LEN
SEQ_BLOCK = CTX_LEN
LAT_BLOCKS = SEQ // SEQ_BLOCK
ROW_BLOCK = 512
SUBLANES = 8
BF16_SUBLANES = 16
LANES = 128
VMEM_LIMIT_BYTES = 48 * 1024 * 1024

MOD_SHIFT1, MOD_SCALE1, MOD_GATE1, MOD_SHIFT2, MOD_SCALE2, MOD_GATE2 = range(6)


def _round_up(x, m):
    return (x + m - 1) // m * m


def _params(*semantics):
    return pltpu.CompilerParams(dimension_semantics=semantics, vmem_limit_bytes=VMEM_LIMIT_BYTES)


def _segment(row_block, rows_per_block):
    return jnp.minimum(row_block // (SEQ // rows_per_block), BATCH)


def _mm_kernel(a_ref, b_ref, o_ref, acc_ref):
    @pl.when(pl.program_id(2) == 0)
    def _():
        acc_ref[...] = jnp.zeros_like(acc_ref)

    acc_ref[...] += jnp.dot(a_ref[...].astype(jnp.bfloat16), b_ref[...].astype(jnp.bfloat16),
                            preferred_element_type=jnp.float32)

    @pl.when(pl.program_id(2) == pl.num_programs(2) - 1)
    def _():
        o_ref[...] = acc_ref[...].astype(o_ref.dtype)


def _mm_fullk_kernel(a_ref, b_ref, o_ref):
    o_ref[...] = jnp.dot(a_ref[...].astype(jnp.bfloat16), b_ref[...].astype(jnp.bfloat16),
                         preferred_element_type=jnp.float32).astype(o_ref.dtype)


def mm(a, b, out_dtype=jnp.float32, tm=ROW_BLOCK, tn=1024, tk=2048, layer=None):
    m, k = a.shape
    n = b.shape[-1]
    assert (layer is None) == (b.ndim == 2)
    tm = min(tm, _round_up(m, 16))
    tn = min(tn, _round_up(n, LANES))
    tk = min(tk, k)
    mp, np_ = _round_up(m, tm), _round_up(n, tn)
    if mp != m:
        a = jnp.pad(a, ((0, mp - m), (0, 0)))
    if np_ != n:
        b = jnp.pad(b, ((0, 0),) * (b.ndim - 1) + ((0, np_ - n),))
    b_block = lambda rows, index: (pl.BlockSpec((rows, tn), index) if layer is None else
                                   pl.BlockSpec((None, rows, tn), lambda *g: (layer,) + index(*g)))
    if tk == k:
        out = pl.pallas_call(
            _mm_fullk_kernel,
            grid=(mp // tm, np_ // tn),
            in_specs=[pl.BlockSpec((tm, k), lambda i, j: (i, 0)),
                      b_block(k, lambda i, j: (0, j))],
            out_specs=pl.BlockSpec((tm, tn), lambda i, j: (i, j)),
            out_shape=jax.ShapeDtypeStruct((mp, np_), out_dtype),
            compiler_params=_params("parallel", "parallel"),
            name="mm_fullk",
        )(a, b)
    else:
        out = pl.pallas_call(
            _mm_kernel,
            grid=(mp // tm, np_ // tn, k // tk),
            in_specs=[pl.BlockSpec((tm, tk), lambda i, j, kk: (i, kk)),
                      b_block(tk, lambda i, j, kk: (kk, j))],
            out_specs=pl.BlockSpec((tm, tn), lambda i, j, kk: (i, j)),
            out_shape=jax.ShapeDtypeStruct((mp, np_), out_dtype),
            scratch_shapes=[pltpu.VMEM((tm, tn), jnp.float32)],
            compiler_params=_params("parallel", "parallel", "arbitrary"),
            name="mm",
        )(a, b)
    if mp != m or np_ != n:
        out = out[:m, :n]
    return out


def _mm_resid_kernel(a_ref, b_ref, x_ref, g_ref, o_ref, acc_ref):
    row_block = pl.program_id(0)

    @pl.when(pl.program_id(2) == 0)
    def _():
        acc_ref[...] = jnp.zeros_like(acc_ref)

    acc_ref[...] += jnp.dot(a_ref[...], b_ref[...].astype(jnp.bfloat16), preferred_element_type=jnp.float32)

    @pl.when(pl.program_id(2) == pl.num_programs(2) - 1)
    def _():
        tm = o_ref.shape[0]
        row = row_block * tm + lax.broadcasted_iota(jnp.int32, (tm, 1), 0)
        gate = g_ref[0, 0]
        for seg in range(1, BATCH + 1):
            gate = jnp.where(row >= seg * SEQ, g_ref[seg, 0], gate)
        o_ref[...] = x_ref[...] + gate * acc_ref[...]


def mm_resid(a, b, x, modtab, which, layer=None, tn=1024):
    m, k = a.shape
    n = b.shape[-1]
    assert (layer is None) == (b.ndim == 2)
    tm = _row_tile(m)
    tk = 2048 * 2 // b.dtype.itemsize
    tn, tk = min(tn, n), min(tk, k)
    assert n % tn == 0 and k % tk == 0
    b_index = lambda i, j, kk: (kk, j)
    b_spec = (pl.BlockSpec((tk, tn), b_index) if layer is None else
              pl.BlockSpec((None, tk, tn), lambda i, j, kk: (layer, kk, j)))
    return pl.pallas_call(
        _mm_resid_kernel,
        grid=(m // tm, n // tn, k // tk),
        in_specs=[pl.BlockSpec((tm, tk), lambda i, j, kk: (i, kk)),
                  b_spec,
                  pl.BlockSpec((tm, tn), lambda i, j, kk: (i, j)),
                  pl.BlockSpec((BATCH + 1, 1, 1, tn), lambda i, j, kk: (0, which, 0, j))],
        out_specs=pl.BlockSpec((tm, tn), lambda i, j, kk: (i, j)),
        out_shape=jax.ShapeDtypeStruct((m, n), jnp.float32),
        scratch_shapes=[pltpu.VMEM((tm, tn), jnp.float32)],
        compiler_params=_params("parallel", "parallel", "arbitrary"),
        name="mm_resid",
    )(a, b, x, modtab)


def _norm_mod_kernel(x_ref, g_ref, sh_ref, sc_ref, o_ref):
    x = x_ref[...]
    y = x * lax.rsqrt(jnp.mean(x * x, axis=-1, keepdims=True) + EPS)
    o_ref[...] = ((y * g_ref[...]) * (1.0 + sc_ref[0, 0]) + sh_ref[0, 0]).astype(o_ref.dtype)


def norm_mod(x, g, modtab, which_shift, which_scale, tm=SEQ_BLOCK):
    m, d = x.shape
    mod_spec = lambda which: pl.BlockSpec((1, 1, 1, d), lambda i: (_segment(i, tm), which, 0, 0))
    return pl.pallas_call(
        _norm_mod_kernel,
        grid=(m // tm,),
        in_specs=[pl.BlockSpec((tm, d), lambda i: (i, 0)),
                  pl.BlockSpec((1, d), lambda i: (0, 0)),
                  mod_spec(which_shift), mod_spec(which_scale)],
        out_specs=pl.BlockSpec((tm, d), lambda i: (i, 0)),
        out_shape=jax.ShapeDtypeStruct((m, d), jnp.bfloat16),
        compiler_params=_params("parallel"),
        name="norm_mod",
    )(x, g.reshape(1, d), modtab, modtab)


def _rmsnorm_kernel(x_ref, g_ref, o_ref):
    x = x_ref[...]
    o_ref[...] = x * lax.rsqrt(jnp.mean(x * x, axis=-1, keepdims=True) + EPS) * g_ref[...]


def rmsnorm_rows(x, g, tm=SEQ_BLOCK):
    m, d = x.shape
    return pl.pallas_call(
        _rmsnorm_kernel,
        grid=(m // tm,),
        in_specs=[pl.BlockSpec((tm, d), lambda i: (i, 0)), pl.BlockSpec((1, d), lambda i: (0, 0))],
        out_specs=pl.BlockSpec((tm, d), lambda i: (i, 0)),
        out_shape=jax.ShapeDtypeStruct((m, d), jnp.float32),
        compiler_params=_params("parallel"),
        name="rmsnorm",
    )(x, g.reshape(1, d))


def _masked_conv(u, w_ref, pad_l, is_ctx):
    rows = u.shape[0]
    t = lax.broadcasted_iota(jnp.int32, (rows, 1), 0)
    seg = jnp.where(is_ctx, rows, GRID_W)
    pos = t & (seg - 1)
    y = None
    for j in range(w_ref.shape[0]):
        k = j - pad_l
        if k == 0:
            sh = u
        else:
            sh = pltpu.roll(u, (-k) % rows, axis=0)
            sh = jnp.where((pos + k >= 0) & (pos + k < seg), sh, 0.0)
        term = w_ref[j:j + 1, :] * sh
        y = term if y is None else y + term
    return y


def _rglru_kernel(*refs, reverse):
    if reverse:
        (p6_ref, cw_ref, cb_ref, wa_ref, ba_ref, wx_ref, bx_ref, lam_ref, hf_ref, p7_ref,
         o_ref, a_sc, b_sc, h_sc) = refs
    else:
        p6_ref, cw_ref, cb_ref, wa_ref, ba_ref, wx_ref, bx_ref, lam_ref, o_ref, a_sc, b_sc, h_sc = refs
    s = pl.program_id(1)

    @pl.when(s == 0)
    def _():
        h_sc[...] = jnp.zeros_like(h_sc)

    u = _masked_conv(p6_ref[...], cw_ref, RG_CONV_LEFT, s == 0) + cb_ref[...]
    ub = u.astype(jnp.bfloat16)
    for g in range(RG_BLOCKS):
        cols = slice(g * RG_BW, (g + 1) * RG_BW)
        ug = ub[:, cols]
        r = jax.nn.sigmoid(jnp.dot(ug, wa_ref[0, g].astype(jnp.bfloat16), preferred_element_type=jnp.float32)
                           + ba_ref[:, cols])
        i = jax.nn.sigmoid(jnp.dot(ug, wx_ref[0, g].astype(jnp.bfloat16), preferred_element_type=jnp.float32)
                           + bx_ref[:, cols])
        neg_lam = -lam_ref[:, cols]
        softplus = jnp.log1p(jnp.exp(-jnp.abs(neg_lam))) + jnp.maximum(neg_lam, 0.0)
        log_a = (-RG_C * softplus) * r
        a_sc[:, cols] = jnp.exp(log_a)
        b_sc[:, cols] = jnp.sqrt(1.0 - jnp.exp(2.0 * log_a)) * (i * u[:, cols])

    n_groups = a_sc.shape[0] // SUBLANES
    row = lax.broadcasted_iota(jnp.int32, (SUBLANES, a_sc.shape[1]), 0)

    def body(it, h_prev):
        grp = (n_groups - 1 - it) if reverse else it
        off = pl.multiple_of(grp * SUBLANES, SUBLANES)
        a = a_sc[pl.ds(off, SUBLANES), :]
        b = b_sc[pl.ds(off, SUBLANES), :]
        for k in (1, 2, 4):
            shift = (SUBLANES - k) if reverse else k
            inside = (row < SUBLANES - k) if reverse else (row >= k)
            a_s = jnp.where(inside, pltpu.roll(a, shift, axis=0), 1.0)
            b_s = jnp.where(inside, pltpu.roll(b, shift, axis=0), 0.0)
            b = a * b_s + b
            a = a * a_s
        h = b + a * h_prev
        b_sc[pl.ds(off, SUBLANES), :] = h
        return h[0:1] if reverse else h[SUBLANES - 1:SUBLANES]

    h_sc[...] = lax.fori_loop(0, n_groups, body, h_sc[...])
    if reverse:
        o_ref[...] = (jax.nn.gelu(p7_ref[...]) * (hf_ref[...] + b_sc[...])).astype(o_ref.dtype)
    else:
        o_ref[...] = b_sc[...]


def _seq_block(b, s, reverse):
    lat = (LAT_BLOCKS - s) if reverse else (s - 1)
    return jnp.where(s == 0, BATCH * LAT_BLOCKS + b, b * LAT_BLOCKS + lat)


def rglru_mixer(proj, col_in, col_gate, conv_w, conv_b, wa, ba, wx, bx, lam):
    w = BRANCH_W
    row2 = lambda a: a.reshape(1, w)
    outs = None
    for reverse in (False, True):
        d = int(reverse)
        blk = lambda col: (lambda b, s: (_seq_block(b, s, reverse), col))
        const2 = lambda b, s: (0, 0)
        in_specs = [pl.BlockSpec((SEQ_BLOCK, w), blk(col_in)),
                    pl.BlockSpec(conv_w.shape, const2),
                    pl.BlockSpec((1, w), const2),
                    pl.BlockSpec((1,) + wa.shape[1:], lambda b, s: (d, 0, 0, 0)),
                    pl.BlockSpec((1, w), const2),
                    pl.BlockSpec((1,) + wx.shape[1:], lambda b, s: (d, 0, 0, 0)),
                    pl.BlockSpec((1, w), const2),
                    pl.BlockSpec((1, w), const2)]
        args = [proj, conv_w, row2(conv_b), wa, row2(ba[d]), wx, row2(bx[d]), row2(lam[d])]
        if reverse:
            in_specs += [pl.BlockSpec((SEQ_BLOCK, w), blk(0)), pl.BlockSpec((SEQ_BLOCK, w), blk(col_gate))]
            args += [outs, proj]
        outs = pl.pallas_call(
            functools.partial(_rglru_kernel, reverse=reverse),
            grid=(BATCH, LAT_BLOCKS + 1),
            in_specs=in_specs,
            out_specs=pl.BlockSpec((SEQ_BLOCK, w), blk(0)),
            out_shape=jax.ShapeDtypeStruct((ROWS, w), jnp.bfloat16 if reverse else jnp.float32),
            scratch_shapes=[pltpu.VMEM((SEQ_BLOCK, w), jnp.float32), pltpu.VMEM((SEQ_BLOCK, w), jnp.float32),
                            pltpu.VMEM((1, w), jnp.float32)],
            compiler_params=_params("parallel", "arbitrary"),
            name="rglru_bwd" if reverse else "rglru_fwd",
        )(*args)
    return outs


def _sconv_kernel(pb_ref, pc_ref, px_ref, w_ref, o_ref):
    is_ctx = pl.program_id(0) >= BATCH * LAT_BLOCKS
    conv = _masked_conv(pc_ref[...] * px_ref[...], w_ref, SC_CONV_LEFT, is_ctx)
    o_ref[...] = (pb_ref[...] * conv).astype(o_ref.dtype)


def sconv_mixer(proj, col_b, col_c, col_x, conv_w):
    w = BRANCH_W
    spec = lambda col: pl.BlockSpec((SEQ_BLOCK, w), lambda i: (i, col))
    return pl.pallas_call(
        _sconv_kernel,
        grid=(ROWS // SEQ_BLOCK,),
        in_specs=[spec(col_b), spec(col_c), spec(col_x), pl.BlockSpec(conv_w.shape, lambda i: (0, 0))],
        out_specs=pl.BlockSpec((SEQ_BLOCK, w), lambda i: (i, 0)),
        out_shape=jax.ShapeDtypeStruct((ROWS, w), jnp.bfloat16),
        compiler_params=_params("parallel"),
        name="sconv",
    )(proj, proj, proj, conv_w)


def _merge_kernel(h_ref, wg_ref, *rest):
    y_refs, (wb_ref, o_ref, acc_ref) = rest[:N_BRANCHES], rest[N_BRANCHES:]
    b = pl.program_id(2)

    @pl.when(b == 0)
    def _():
        acc_ref[...] = jnp.zeros_like(acc_ref)

    gate = jax.nn.sigmoid(jnp.dot(h_ref[...], wg_ref[...], preferred_element_type=jnp.float32))
    y = y_refs[0][...]
    for branch in range(1, N_BRANCHES):
        y = jnp.where(b == branch, y_refs[branch][...], y)
    acc_ref[...] += gate * jnp.dot(y, wb_ref[0], preferred_element_type=jnp.float32)

    @pl.when(b == N_BRANCHES - 1)
    def _():
        o_ref[...] = acc_ref[...].astype(o_ref.dtype)


MAX_ROW_TILE = 1088


def _row_tile(m):
    return max(t for t in range(BF16_SUBLANES, MAX_ROW_TILE + 1, BF16_SUBLANES) if m % t == 0)


def merge_branches(h, w_gate, ys, w_branch, m):
    d = h.shape[1]
    bw = ys[0].shape[1]
    tm = 2 * ROW_BLOCK if m % (2 * ROW_BLOCK) == 0 else ROW_BLOCK
    tn = 1024 * ROW_BLOCK // tm
    n_col = d // tn
    y_spec = pl.BlockSpec((tm, bw), lambda i, j, b: (i, 0))
    return pl.pallas_call(
        _merge_kernel,
        grid=(m // tm, n_col, N_BRANCHES),
        in_specs=[pl.BlockSpec((tm, d), lambda i, j, b: (i, 0)),
                  pl.BlockSpec((d, tn), lambda i, j, b: (0, b * n_col + j))]
                 + [y_spec] * N_BRANCHES
                 + [pl.BlockSpec((1, bw, tn), lambda i, j, b: (b, 0, j))],
        out_specs=pl.BlockSpec((tm, tn), lambda i, j, b: (i, j)),
        out_shape=jax.ShapeDtypeStruct((m, d), jnp.bfloat16),
        scratch_shapes=[pltpu.VMEM((tm, tn), jnp.float32)],
        compiler_params=_params("parallel", "parallel", "arbitrary"),
        name="merge_branches",
    )(h, w_gate, *ys, w_branch)


def _extract_topk(s, n_top, val_ref, idx_ref, slot, rid=None):
    if rid is None:
        rid = lax.broadcasted_iota(jnp.int32, s.shape, 0).astype(jnp.float32)
    for r in range(n_top):
        m = jnp.max(s, axis=0, keepdims=True)
        am = jnp.min(jnp.where(s == m, rid, jnp.inf), axis=0, keepdims=True)
        val_ref[slot, r:r + 1, :] = m
        idx_ref[slot, r:r + 1, :] = am
        s = jnp.where(rid == am, -jnp.inf, s)


CAND_COUNTS = tuple(PEER_TOPK // (j1 + 1) for j1 in range(PEER_TOPK))
N_CAND = sum(CAND_COUNTS)
N_CAND_ROWS = _round_up(N_CAND, SUBLANES)


def _cand_flat_ids(tt):
    ids = [j1 * PEER_TOPK + j2 for j1, n2 in enumerate(CAND_COUNTS) for j2 in range(n2)]
    ids += [PEER_TOPK * PEER_TOPK + p for p in range(N_CAND_ROWS - N_CAND)]
    return jnp.broadcast_to(jnp.asarray(ids, jnp.float32)[:, None], (N_CAND_ROWS, tt))


def _lookup_rows(table, sel):
    out = jnp.zeros(sel.shape, table.dtype)
    for r in range(table.shape[0]):
        out = jnp.where(sel == r, table[r:r + 1, :], out)
    return out


def _peer_topk_kernel(q_ref, keys_ref, flat_ref, i1_ref, i2_ref, w_ref, val_sc, idx_sc, cand_sc, top_sc, pos_sc,
                      ent_sc):
    n_half = keys_ref.shape[0]
    for hp in range(n_half):
        q = q_ref[:, hp * PEER_DKH:(hp + 1) * PEER_DKH]
        s = lax.dot_general(keys_ref[hp], q, (((1,), (1,)), ((), ())), precision=lax.Precision.HIGHEST,
                            preferred_element_type=jnp.float32)
        _extract_topk(s, PEER_TOPK, val_sc, idx_sc, hp)
    cand_sc[N_CAND_ROWS - SUBLANES:N_CAND_ROWS, :] = jnp.full((SUBLANES, cand_sc.shape[1]), -jnp.inf, jnp.float32)
    for h in range(PEER_HEADS):
        v1, v2 = val_sc[2 * h], val_sc[2 * h + 1]
        row0 = 0
        for j1, n2 in enumerate(CAND_COUNTS):
            cand_sc[row0:row0 + n2, :] = v1[j1:j1 + 1, :] + v2[0:n2, :]
            row0 += n2
        _extract_topk(cand_sc[...], PEER_TOPK, top_sc, pos_sc, 0, rid=flat_ref[...])
        top, pos = top_sc[0], pos_sc[0].astype(jnp.int32)
        e = jnp.exp(top - top[0:1, :])
        rows = slice(h * PEER_TOPK, (h + 1) * PEER_TOPK)
        ent_sc[0, rows, :] = _lookup_rows(idx_sc[2 * h], pos >> TOPK_SHIFT)
        ent_sc[1, rows, :] = _lookup_rows(idx_sc[2 * h + 1], pos & (PEER_TOPK - 1))
        ent_sc[2, rows, :] = e / jnp.sum(e, axis=0, keepdims=True)
    i1_ref[...] = ent_sc[0].T.astype(jnp.int32)
    i2_ref[...] = ent_sc[1].T.astype(jnp.int32)
    w_ref[...] = ent_sc[2].T


def peer_topk(q, keys, tt=LANES):
    n_tok = q.shape[0]
    n_ent = PEER_HEADS * PEER_TOPK
    assert n_ent == tt
    ent_spec = pl.BlockSpec((tt, n_ent), lambda i: (i, 0))
    f32, i32 = jnp.float32, jnp.int32
    return pl.pallas_call(
        _peer_topk_kernel,
        grid=(n_tok // tt,),
        in_specs=[pl.BlockSpec((tt, q.shape[1]), lambda i: (i, 0)),
                  pl.BlockSpec(keys.shape, lambda i: (0, 0, 0)),
                  pl.BlockSpec((N_CAND_ROWS, tt), lambda i: (0, 0))],
        out_specs=[ent_spec, ent_spec, ent_spec],
        out_shape=[jax.ShapeDtypeStruct((n_tok, n_ent), i32), jax.ShapeDtypeStruct((n_tok, n_ent), i32),
                   jax.ShapeDtypeStruct((n_tok, n_ent), f32)],
        scratch_shapes=[pltpu.VMEM((2 * PEER_HEADS, PEER_TOPK, tt), f32), pltpu.VMEM((2 * PEER_HEADS, PEER_TOPK, tt), f32),
                        pltpu.VMEM((N_CAND_ROWS, tt), f32),
                        pltpu.VMEM((1, PEER_TOPK, tt), f32), pltpu.VMEM((1, PEER_TOPK, tt), f32),
                        pltpu.VMEM((3, n_ent, tt), f32)],
        compiler_params=_params("parallel"),
        name="peer_topk",
    )(q, keys, _cand_flat_ids(tt))


def _peer_score_kernel(h_ref, u_ref, i1_ref, i2_ref, o_ref):
    j = pl.program_id(1)

    @pl.when(j == 0)
    def _():
        o_ref[...] = jnp.zeros_like(o_ref)

    s = lax.dot_general(h_ref[...], u_ref[...].astype(jnp.bfloat16), (((1,), (1,)), ((), ())),
                        preferred_element_type=jnp.float32)
    i1, i2 = i1_ref[...], i2_ref[...]
    acc = o_ref[...]
    n_chunks = s.shape[1] // PEER_NKEYS
    for c in range(n_chunks):
        picked = jnp.take_along_axis(s[:, c * PEER_NKEYS:(c + 1) * PEER_NKEYS], i2, axis=1)
        acc = jnp.where(i1 == j * n_chunks + c, picked, acc)
    o_ref[...] = acc


def peer_scores(h, u_tab, layer, i1, i2):
    n_tok, d = h.shape
    n_exp = u_tab.shape[1]
    n_ent = i1.shape[1]
    tm = _row_tile(n_tok)
    tn = 1024 * 2 // u_tab.dtype.itemsize
    assert n_exp % tn == 0 and n_ent == PEER_NKEYS
    ent_spec = pl.BlockSpec((tm, n_ent), lambda i, j: (i, 0))
    return pl.pallas_call(
        _peer_score_kernel,
        grid=(n_tok // tm, n_exp // tn),
        in_specs=[pl.BlockSpec((tm, d), lambda i, j: (i, 0)),
                  pl.BlockSpec((None, tn, d), lambda i, j: (layer, j, 0)),
                  ent_spec, ent_spec],
        out_specs=ent_spec,
        out_shape=jax.ShapeDtypeStruct((n_tok, n_ent), jnp.float32),
        compiler_params=_params("parallel", "arbitrary"),
        name="peer_scores",
    )(h, u_tab, i1, i2)


def _peer_coef_kernel(sc_ref, w_ref, i1_ref, i2_ref, o_ref, wa_sc, ct_sc):
    wa_sc[...] = w_ref[...] * jax.nn.gelu(sc_ref[...])
    n_keys = PEER_NKEYS
    n_ent = sc_ref.shape[1]
    group = ct_sc.shape[0]
    key = lax.broadcasted_iota(jnp.int32, (n_keys, n_ent), 0)

    def body(gi, carry):
        t0 = pl.multiple_of(gi * group, group)
        for u in range(group):
            row = lambda ref: jnp.broadcast_to(ref[pl.ds(t0 + u, 1), :], (n_keys, n_ent))
            at = jnp.where(key == row(i1_ref), row(wa_sc), 0.0).astype(jnp.bfloat16)
            bt = jnp.where(key == row(i2_ref), 1.0, 0.0).astype(jnp.bfloat16)
            ct_sc[u] = lax.dot_general(at, bt, (((1,), (1,)), ((), ())), preferred_element_type=jnp.float32)
        by_key = jnp.swapaxes(ct_sc[...], 0, 1)
        for a in range(n_keys):
            o_ref[pl.ds(t0, group), a * n_keys:(a + 1) * n_keys] = by_key[a].astype(o_ref.dtype)
        return carry

    lax.fori_loop(0, sc_ref.shape[0] // group, body, 0)


def peer_coef(sc, wts, i1, i2, tb=128):
    n_tok, n_ent = sc.shape
    tb = min(tb, n_tok)
    assert n_tok % tb == 0 and tb % BF16_SUBLANES == 0
    ent_spec = pl.BlockSpec((tb, n_ent), lambda i: (i, 0))
    n_exp = PEER_NKEYS * PEER_NKEYS
    return pl.pallas_call(
        _peer_coef_kernel,
        grid=(n_tok // tb,),
        in_specs=[ent_spec] * 4,
        out_specs=pl.BlockSpec((tb, n_exp), lambda i: (i, 0)),
        out_shape=jax.ShapeDtypeStruct((n_tok, n_exp), jnp.bfloat16),
        scratch_shapes=[pltpu.VMEM((tb, n_ent), jnp.float32),
                        pltpu.VMEM((BF16_SUBLANES, PEER_NKEYS, PEER_NKEYS), jnp.float32)],
        compiler_params=_params("parallel"),
        name="peer_coef",
    )(sc, wts, i1, i2)


def peer_ffn(h, x, modtab, layer, w_q, keys, u_tab, v_tab):
    q = mm(h, w_q, tm=_row_tile(h.shape[0]), tn=512, tk=D_MODEL, layer=layer)
    i1, i2, wts = peer_topk(q, keys.reshape(2 * PEER_HEADS, PEER_NKEYS, PEER_DKH))
    sc = peer_scores(h, u_tab, layer, i1, i2)
    return mm_resid(peer_coef(sc, wts, i1, i2), v_tab, x, modtab, MOD_GATE2, layer=layer)


def _dft_parts(n):
    m = int(round(n ** 0.5))
    assert m * m == n
    part = jnp.arange(m, dtype=jnp.int32)[:, None]
    col = jnp.arange(n, dtype=jnp.int32)[None, :]
    ang_a = ((part * col) % m).astype(jnp.float32) * (2.0 * jnp.pi / m)
    ang_b = ((part * col) % n).astype(jnp.float32) * (2.0 * jnp.pi / n)
    ca, sa = jnp.cos(ang_a)[:, None, :], jnp.sin(ang_a)[:, None, :]
    cb, sb = jnp.cos(ang_b)[None, :, :], jnp.sin(ang_b)[None, :, :]
    scale = n ** -0.5
    return ((ca * cb - sa * sb) * scale).reshape(n, n), ((sa * cb + ca * sb) * scale).reshape(n, n)


def dft_tables():
    cc, sc = _dft_parts(FOURIER_GW)
    ct_l, st_l = _dft_parts(SEQ)
    ct_c, st_c = _dft_parts(CTX_LEN)
    bf16 = jnp.bfloat16
    return (jnp.concatenate([cc, sc], axis=1).astype(bf16),
            jnp.concatenate([ct_l, -st_l], axis=1).astype(bf16),
            jnp.concatenate([ct_c, -st_c], axis=1).astype(bf16))


def _fourier_kernel(p_ref, chan_ref, pos_ref, *rest):
    o_ref, gcs_sc = rest[-2:]
    seq = p_ref.shape[0]
    n_r = seq // o_ref.shape[0]
    r = pl.program_id(2)

    @pl.when(r == 0)
    def _():
        gc = jnp.dot(p_ref[...].astype(jnp.bfloat16), chan_ref[...], preferred_element_type=jnp.float32)
        gcs_sc[0:seq, :] = gc[:, :FOURIER_GW].astype(gcs_sc.dtype)
        gcs_sc[seq:2 * seq, :] = gc[:, FOURIER_GW:].astype(gcs_sc.dtype)

    @pl.when(r < n_r)
    def _():
        o_ref[...] = jnp.dot(pos_ref[...], gcs_sc[...], preferred_element_type=jnp.float32).astype(o_ref.dtype)

    if len(rest) == 3:
        @pl.when((r == n_r) & (pl.program_id(0) == 0))
        def _():
            o_ref[...] = rest[0][...]


def fourier_mixer(proj, tables, tr=ROW_BLOCK):
    chan, pos_lat, pos_ctx = tables
    gw = FOURIER_GW
    ctx_rows = BATCH * CTX_LEN
    assert ctx_rows == tr
    n_r = SEQ // tr
    chan_spec = pl.BlockSpec(chan.shape, lambda b, g, r: (0, 0))
    y_ctx = pl.pallas_call(
        _fourier_kernel,
        grid=(BATCH, FOURIER_GROUPS, 1),
        in_specs=[pl.BlockSpec((CTX_LEN, gw), lambda b, g, r: (LAT_ROWS // CTX_LEN + b, g)), chan_spec,
                  pl.BlockSpec(pos_ctx.shape, lambda b, g, r: (0, 0))],
        out_specs=pl.BlockSpec((CTX_LEN, gw), lambda b, g, r: (b, g)),
        out_shape=jax.ShapeDtypeStruct((ctx_rows, BRANCH_W), jnp.bfloat16),
        scratch_shapes=[pltpu.VMEM((2 * CTX_LEN, gw), jnp.bfloat16)],
        compiler_params=_params("parallel", "parallel", "arbitrary"),
        name="fourier_ctx",
    )(proj, chan, pos_ctx)
    return pl.pallas_call(
        _fourier_kernel,
        grid=(BATCH, FOURIER_GROUPS, n_r + 1),
        in_specs=[pl.BlockSpec((SEQ, gw), lambda b, g, r: (b, g)), chan_spec,
                  pl.BlockSpec((tr, 2 * SEQ), lambda b, g, r: (jnp.minimum(r, n_r - 1), 0)),
                  pl.BlockSpec((ctx_rows, gw), lambda b, g, r: (0, g))],
        out_specs=pl.BlockSpec((tr, gw), lambda b, g, r: (jnp.where((r == n_r) & (b == 0), BATCH * n_r,
                                                                    b * n_r + jnp.minimum(r, n_r - 1)), g)),
        out_shape=jax.ShapeDtypeStruct((ROWS, BRANCH_W), jnp.bfloat16),
        scratch_shapes=[pltpu.VMEM((2 * SEQ, gw), jnp.bfloat16)],
        compiler_params=_params("arbitrary", "arbitrary", "arbitrary"),
        name="fourier_lat",
    )(proj, chan, pos_lat, y_ctx)


def _mlstm_kernel(q_ref, k_ref, v_ref, gt_ref, o_ref, ct_sc, n_sc, m_sc):
    d = pl.program_id(0)
    f32, bf16 = jnp.float32, jnp.bfloat16
    n_t = q_ref.shape[0]

    @pl.when(pl.program_id(2) == 0)
    def _():
        ct_sc[...] = jnp.zeros_like(ct_sc)
        n_sc[...] = jnp.zeros_like(n_sc)
        m_sc[...] = jnp.zeros_like(m_sc)

    r = lax.broadcasted_iota(jnp.int32, (n_t, n_t), 0)
    c = lax.broadcasted_iota(jnp.int32, (n_t, n_t), 1)
    upto = (c - r) * (1 - 2 * d) <= 0
    eye = r == c
    for head in range(MLSTM_HEADS):
        qk_cols = slice(head * MLSTM_DQK, (head + 1) * MLSTM_DQK)
        v_cols = slice(head * MLSTM_DV, (head + 1) * MLSTM_DV)
        q = q_ref[:, qk_cols] * (MLSTM_DQK ** -0.5)
        k = k_ref[:, qk_cols]
        v = v_ref[:, v_cols]
        gate0 = d * (2 * MLSTM_HEADS) + head
        li = gt_ref[pl.ds(gate0, 1), :]
        gf = gt_ref[pl.ds(gate0 + MLSTM_HEADS, 1), :]
        lf = -(jnp.log1p(jnp.exp(-jnp.abs(gf))) + jnp.maximum(-gf, 0.0))

        b_col = jnp.sum(jnp.where(upto, lf, 0.0), axis=1, keepdims=True)
        b_row = jnp.sum(jnp.where(eye, b_col, 0.0), axis=0, keepdims=True)
        m_prev = m_sc[head]
        logw = jnp.where(upto, b_col - b_row + li, -jnp.inf)
        g_col = b_col + m_prev
        mt = jnp.maximum(g_col, jnp.max(logw, axis=1, keepdims=True))
        qb = q.astype(bf16)
        s = (lax.dot_general(qb, k.astype(bf16), (((1,), (1,)), ((), ())), preferred_element_type=f32)
             * jnp.exp(logw - mt))
        w_inter = jnp.exp(g_col - mt)
        num = (jnp.dot(s.astype(bf16), v.astype(bf16), preferred_element_type=f32)
               + w_inter * jnp.dot(qb, ct_sc[head].astype(bf16), preferred_element_type=f32))
        den = jnp.sum(s, axis=1, keepdims=True) + w_inter * jnp.sum(q * n_sc[head], axis=1, keepdims=True)
        o_ref[0, :, v_cols] = num / jnp.maximum(jnp.abs(den), jnp.exp(-mt))

        total = jnp.sum(lf, axis=1, keepdims=True)
        logu = total - b_row + li
        m_new = jnp.maximum(total + m_prev, jnp.max(logu, axis=1, keepdims=True))
        ws_row = jnp.exp(logu - m_new)
        wc = jnp.exp(total + m_prev - m_new)
        ws_col = jnp.sum(jnp.where(eye, ws_row, 0.0), axis=1, keepdims=True)
        kv = lax.dot_general(k.astype(bf16), (ws_col * v).astype(bf16), (((0,), (0,)), ((), ())),
                             preferred_element_type=f32)
        ct_sc[head] = wc * ct_sc[head] + kv
        n_sc[head] = wc * n_sc[head] + jnp.sum(ws_col * k, axis=0, keepdims=True)
        m_sc[head] = m_new


def _mlstm_chunk(d, b, s):
    n_ctx, n_lat = CTX_LEN // MLSTM_CHUNK, SEQ // MLSTM_CHUNK
    ctx_j = jnp.where(d == 0, s, n_ctx - 1 - s)
    lat_j = jnp.where(d == 0, s - n_ctx, n_ctx + n_lat - 1 - s)
    return jnp.where(s < n_ctx, BATCH * n_lat + b * n_ctx + ctx_j, b * n_lat + lat_j)


def mlstm_mixer(proj, gates_t):
    t = MLSTM_CHUNK
    qk_w = MLSTM_HEADS * MLSTM_DQK
    q_col, k_col, v_col = BRANCH_W // qk_w, BRANCH_W // qk_w + 1, 2
    n_steps = (CTX_LEN + SEQ) // t
    f32 = jnp.float32
    return pl.pallas_call(
        _mlstm_kernel,
        grid=(2, BATCH, n_steps),
        in_specs=[pl.BlockSpec((t, qk_w), lambda d, b, s: (_mlstm_chunk(d, b, s), q_col)),
                  pl.BlockSpec((t, qk_w), lambda d, b, s: (_mlstm_chunk(d, b, s), k_col)),
                  pl.BlockSpec((t, BRANCH_W), lambda d, b, s: (_mlstm_chunk(d, b, s), v_col)),
                  pl.BlockSpec((N_MLSTM_GATES, t), lambda d, b, s: (0, _mlstm_chunk(d, b, s)))],
        out_specs=pl.BlockSpec((1, t, BRANCH_W), lambda d, b, s: (d, _mlstm_chunk(d, b, s), 0)),
        out_shape=jax.ShapeDtypeStruct((2, ROWS, BRANCH_W), f32),
        scratch_shapes=[pltpu.VMEM((MLSTM_HEADS, MLSTM_DQK, MLSTM_DV), f32), pltpu.VMEM((MLSTM_HEADS, 1, MLSTM_DQK), f32),
                        pltpu.VMEM((MLSTM_HEADS, 1, 1), f32)],
        compiler_params=_params("parallel", "parallel", "arbitrary"),
        name="mlstm",
    )(proj, proj, proj, gates_t)


def _mlstm_out_kernel(hs_ref, og_ref, g_ref, o_ref):
    h = hs_ref[0] + hs_ref[1]
    for head in range(MLSTM_HEADS):
        cols = slice(head * MLSTM_DV, (head + 1) * MLSTM_DV)
        hh = h[:, cols]
        hn = hh * lax.rsqrt(jnp.mean(hh * hh, axis=-1, keepdims=True) + EPS)
        o_ref[:, cols] = (hn * g_ref[:, cols] * jax.nn.sigmoid(og_ref[:, cols])).astype(o_ref.dtype)


def mlstm_out(hs, proj, col_gate, norm_g, tm=SEQ_BLOCK):
    w = BRANCH_W
    return pl.pallas_call(
        _mlstm_out_kernel,
        grid=(ROWS // tm,),
        in_specs=[pl.BlockSpec((2, tm, w), lambda i: (0, i, 0)),
                  pl.BlockSpec((tm, w), lambda i: (i, col_gate)),
                  pl.BlockSpec((1, w), lambda i: (0, 0))],
        out_specs=pl.BlockSpec((tm, w), lambda i: (i, 0)),
        out_shape=jax.ShapeDtypeStruct((ROWS, w), jnp.bfloat16),
        compiler_params=_params("parallel"),
        name="mlstm_out",
    )(hs, proj, norm_g.reshape(1, w))


def kernel(x, c, ctx, c_ctx, w_mod, b_mod, g_norm1, g_norm2, w_in, w_branch, w_out, mlstm_gate_b, mlstm_norm_g,
           rg_conv_w, rg_conv_b, rg_wa, rg_ba, rg_wx, rg_bx, rg_lam, sc_conv_w, peer_wq, peer_keys, peer_u,
           peer_v, g_final):
    bf16 = jnp.bfloat16
    xs = jnp.concatenate([x.reshape(LAT_ROWS, D_MODEL), ctx.reshape(BATCH * CTX_LEN, D_MODEL)], axis=0)
    cond = jax.nn.silu(jnp.concatenate([c, c_ctx[None, :]], axis=0))
    tables = dft_tables()
    w_out_b, peer_u_b, peer_v_b = w_out.astype(bf16), peer_u.astype(bf16), peer_v.astype(bf16)
    for l in range(DEPTH):
        last = l == DEPTH - 1
        modtab = (mm(cond, w_mod, tm=16, tn=2048, tk=2048, layer=l) + b_mod[l]).reshape(BATCH + 1, 6, 1, D_MODEL)
        w_main = jnp.concatenate([w_in[l][:, :COL_MAIN], w_in[l][:, COL_GATES:COL_MERGE]], axis=1).astype(bf16)
        w_gates = jnp.pad(w_in[l][:, COL_MAIN:COL_GATES], ((0, 0), (0, LANES - N_MLSTM_GATES))).astype(bf16)
        w_merge = w_in[l][:, COL_MERGE:].astype(bf16)

        h = norm_mod(xs, g_norm1[l], modtab, MOD_SHIFT1, MOD_SCALE1)
        proj = mm(h, w_main, tm=_row_tile(ROWS), tn=512, tk=D_MODEL)
        gates = mm(h, w_gates, tm=_row_tile(ROWS), tk=D_MODEL)
        y_four = fourier_mixer(proj, tables)
        gates_t = gates[:, :N_MLSTM_GATES].T + mlstm_gate_b[l].reshape(N_MLSTM_GATES, 1)
        y_ml = mlstm_out(mlstm_mixer(proj, gates_t), proj, 3, mlstm_norm_g[l])
        y_rg = rglru_mixer(proj, 4, 5, rg_conv_w[l], rg_conv_b[l], rg_wa[l], rg_ba[l], rg_wx[l], rg_bx[l], rg_lam[l])
        y_sc = sconv_mixer(proj, 6, 7, 8, sc_conv_w[l])
        m = LAT_ROWS if last else ROWS
        merged = merge_branches(h, w_merge, (y_four, y_ml, y_rg, y_sc), w_branch[l].astype(bf16), m)
        xs = mm_resid(merged, w_out_b, xs, modtab, MOD_GATE1, layer=l)

        h2 = norm_mod(xs, g_norm2[l], modtab, MOD_SHIFT2, MOD_SCALE2)
        xs = peer_ffn(h2, xs, modtab, l, peer_wq, peer_keys[l], peer_u_b, peer_v_b)
    return rmsnorm_rows(xs, g_final).reshape(BATCH, SEQ, D_MODEL)
```

```python
import functools

import jax
import jax.numpy as jnp
from jax import lax
from jax.experimental import pallas as pl
from jax.experimental.pallas import tpu as pltpu

D_MODEL = 4096
BATCH = 2
SEQ = 4096
DEPTH = 2
CTX_LEN = 256
GRID_W = 64
N_BRANCHES = 4
BRANCH_W = D_MODEL // 4
FOURIER_GROUPS = 4
FOURIER_GW = BRANCH_W // FOURIER_GROUPS
MLSTM_HEADS = 4
MLSTM_DV = BRANCH_W // MLSTM_HEADS
MLSTM_DQK = MLSTM_DV // 2
MLSTM_CHUNK = 128
RG_BLOCKS = 8
RG_BW = BRANCH_W // RG_BLOCKS
RG_C = 8.0
RG_CONV_LEFT = 2
SC_CONV_LEFT = 1
PEER_HEADS = 8
PEER_NKEYS = 128
PEER_DK = 256
PEER_DKH = PEER_DK // 2
PEER_TOPK = 16
TOPK_SHIFT = PEER_TOPK.bit_length() - 1
EPS = 1e-6

N_MLSTM_GATES = 2 * 2 * MLSTM_HEADS
COL_MAIN = 4 * BRANCH_W
COL_GATES = COL_MAIN + N_MLSTM_GATES
COL_MERGE = COL_GATES + 5 * BRANCH_W

LAT_ROWS = BATCH * SEQ
ROWS = LAT_ROWS + BATCH * CTX_LEN
SEQ_BLOCK = CTX_LEN
LAT_BLOCKS = SEQ // SEQ_BLOCK
ROW_BLOCK = 512
SUBLANES = 8
BF16_SUBLANES = 16
LANES = 128
VMEM_LIMIT_BYTES = 48 * 1024 * 1024

MOD_SHIFT1, MOD_SCALE1, MOD_GATE1, MOD_SHIFT2, MOD_SCALE2, MOD_GATE2 = range(6)


def _round_up(x, m):
    return (x + m - 1) // m * m


def _params(*semantics):
    return pltpu.CompilerParams(dimension_semantics=semantics, vmem_limit_bytes=VMEM_LIMIT_BYTES)


def _segment(row_block, rows_per_block):
    return jnp.minimum(row_block // (SEQ // rows_per_block), BATCH)


def _mm_kernel(a_ref, b_ref, o_ref, acc_ref):
    @pl.when(pl.program_id(2) == 0)
    def _():
        acc_ref[...] = jnp.zeros_like(acc_ref)

    acc_ref[...] += jnp.dot(a_ref[...].astype(jnp.bfloat16), b_ref[...].astype(jnp.bfloat16),
                            preferred_element_type=jnp.float32)

    @pl.when(pl.program_id(2) == pl.num_programs(2) - 1)
    def _():
        o_ref[...] = acc_ref[...].astype(o_ref.dtype)


def _mm_fullk_kernel(a_ref, b_ref, o_ref):
    o_ref[...] = jnp.dot(a_ref[...].astype(jnp.bfloat16), b_ref[...].astype(jnp.bfloat16),
                         preferred_element_type=jnp.float32).astype(o_ref.dtype)


def mm(a, b, out_dtype=jnp.float32, tm=ROW_BLOCK, tn=1024, tk=2048, layer=None):
    m, k = a.shape
    n = b.shape[-1]
    assert (layer is None) == (b.ndim == 2)
    tm = min(tm, _round_up(m, 16))
    tn = min(tn, _round_up(n, LANES))
    tk = min(tk, k)
    mp, np_ = _round_up(m, tm), _round_up(n, tn)
    if mp != m:
        a = jnp.pad(a, ((0, mp - m), (0, 0)))
    if np_ != n:
        b = jnp.pad(b, ((0, 0),) * (b.ndim - 1) + ((0, np_ - n),))
    b_block = lambda rows, index: (pl.BlockSpec((rows, tn), index) if layer is None else
                                   pl.BlockSpec((None, rows, tn), lambda *g: (layer,) + index(*g)))
    if tk == k:
        out = pl.pallas_call(
            _mm_fullk_kernel,
            grid=(mp // tm, np_ // tn),
            in_specs=[pl.BlockSpec((tm, k), lambda i, j: (i, 0)),
                      b_block(k, lambda i, j: (0, j))],
            out_specs=pl.BlockSpec((tm, tn), lambda i, j: (i, j)),
            out_shape=jax.ShapeDtypeStruct((mp, np_), out_dtype),
            compiler_params=_params("parallel", "parallel"),
            name="mm_fullk",
        )(a, b)
    else:
        out = pl.pallas_call(
            _mm_kernel,
            grid=(mp // tm, np_ // tn, k // tk),
            in_specs=[pl.BlockSpec((tm, tk), lambda i, j, kk: (i, kk)),
                      b_block(tk, lambda i, j, kk: (kk, j))],
            out_specs=pl.BlockSpec((tm, tn), lambda i, j, kk: (i, j)),
            out_shape=jax.ShapeDtypeStruct((mp, np_), out_dtype),
            scratch_shapes=[pltpu.VMEM((tm, tn), jnp.float32)],
            compiler_params=_params("parallel", "parallel", "arbitrary"),
            name="mm",
        )(a, b)
    if mp != m or np_ != n:
        out = out[:m, :n]
    return out


def _mm_resid_kernel(a_ref, b_ref, x_ref, g_ref, o_ref, acc_ref):
    row_block = pl.program_id(0)

    @pl.when(pl.program_id(2) == 0)
    def _():
        acc_ref[...] = jnp.zeros_like(acc_ref)

    acc_ref[...] += jnp.dot(a_ref[...], b_ref[...].astype(jnp.bfloat16), preferred_element_type=jnp.float32)

    @pl.when(pl.program_id(2) == pl.num_programs(2) - 1)
    def _():
        tm = o_ref.shape[0]
        row = row_block * tm + lax.broadcasted_iota(jnp.int32, (tm, 1), 0)
        gate = g_ref[0, 0]
        for seg in range(1, BATCH + 1):
            gate = jnp.where(row >= seg * SEQ, g_ref[seg, 0], gate)
        o_ref[...] = x_ref[...] + gate * acc_ref[...]


def mm_resid(a, b, x, modtab, which, layer=None, tn=1024):
    m, k = a.shape
    n = b.shape[-1]
    assert (layer is None) == (b.ndim == 2)
    tm = _row_tile(m)
    tk = 2048 * 2 // b.dtype.itemsize
    tn, tk = min(tn, n), min(tk, k)
    assert n % tn == 0 and k % tk == 0
    b_index = lambda i, j, kk: (kk, j)
    b_spec = (pl.BlockSpec((tk, tn), b_index) if layer is None else
              pl.BlockSpec((None, tk, tn), lambda i, j, kk: (layer, kk, j)))
    return pl.pallas_call(
        _mm_resid_kernel,
        grid=(m // tm, n // tn, k // tk),
        in_specs=[pl.BlockSpec((tm, tk), lambda i, j, kk: (i, kk)),
                  b_spec,
                  pl.BlockSpec((tm, tn), lambda i, j, kk: (i, j)),
                  pl.BlockSpec((BATCH + 1, 1, 1, tn), lambda i, j, kk: (0, which, 0, j))],
        out_specs=pl.BlockSpec((tm, tn), lambda i, j, kk: (i, j)),
        out_shape=jax.ShapeDtypeStruct((m, n), jnp.float32),
        scratch_shapes=[pltpu.VMEM((tm, tn), jnp.float32)],
        compiler_params=_params("parallel", "parallel", "arbitrary"),
        name="mm_resid",
    )(a, b, x, modtab)


def _adaln_kernel(ct_ref, w_ref, o_ref):
    @pl.when(pl.program_id(2) == 0)
    def _():
        o_ref[...] = jnp.zeros_like(o_ref)

    w = w_ref[...]
    for r in range(BATCH + 1):
        o_ref[r:r + 1, :] += jnp.sum(w * ct_ref[:, r:r + 1], axis=0, keepdims=True)


def adaln_tables(cond, w_mod, b_mod, tk=512, tn=2048):
    depth, k, n = w_mod.shape
    n_rows = cond.shape[0]
    cond_t = jnp.pad(cond.T, ((0, 0), (0, LANES - n_rows)))
    out = pl.pallas_call(
        _adaln_kernel,
        grid=(depth, n // tn, k // tk),
        in_specs=[pl.BlockSpec((tk, LANES), lambda l, j, kk: (kk, 0)),
                  pl.BlockSpec((None, tk, tn), lambda l, j, kk: (l, kk, j))],
        out_specs=pl.BlockSpec((None, SUBLANES, tn), lambda l, j, kk: (l, 0, j)),
        out_shape=jax.ShapeDtypeStruct((depth, SUBLANES, n), jnp.float32),
        compiler_params=_params("parallel", "parallel", "arbitrary"),
        name="adaln",
    )(cond_t, w_mod)
    return (out[:, :n_rows] + b_mod[:, None, :]).reshape(depth, n_rows, 6, 1, k)


def _norm_mod_kernel(x_ref, g_ref, sh_ref, sc_ref, o_ref):
    x = x_ref[...]
    y = x * lax.rsqrt(jnp.mean(x * x, axis=-1, keepdims=True) + EPS)
    o_ref[...] = ((y * g_ref[...]) * (1.0 + sc_ref[0, 0]) + sh_ref[0, 0]).astype(o_ref.dtype)


def norm_mod(x, g, modtab, which_shift, which_scale, tm=SEQ_BLOCK):
    m, d = x.shape
    mod_spec = lambda which: pl.BlockSpec((1, 1, 1, d), lambda i: (_segment(i, tm), which, 0, 0))
    return pl.pallas_call(
        _norm_mod_kernel,
        grid=(m // tm,),
        in_specs=[pl.BlockSpec((tm, d), lambda i: (i, 0)),
                  pl.BlockSpec((1, d), lambda i: (0, 0)),
                  mod_spec(which_shift), mod_spec(which_scale)],
        out_specs=pl.BlockSpec((tm, d), lambda i: (i, 0)),
        out_shape=jax.ShapeDtypeStruct((m, d), jnp.bfloat16),
        compiler_params=_params("parallel"),
        name="norm_mod",
    )(x, g.reshape(1, d), modtab, modtab)


def _rmsnorm_kernel(x_ref, g_ref, o_ref):
    x = x_ref[...]
    o_ref[...] = x * lax.rsqrt(jnp.mean(x * x, axis=-1, keepdims=True) + EPS) * g_ref[...]


def rmsnorm_rows(x, g, tm=SEQ_BLOCK):
    m, d = x.shape
    return pl.pallas_call(
        _rmsnorm_kernel,
        grid=(m // tm,),
        in_specs=[pl.BlockSpec((tm, d), lambda i: (i, 0)), pl.BlockSpec((1, d), lambda i: (0, 0))],
        out_specs=pl.BlockSpec((tm, d), lambda i: (i, 0)),
        out_shape=jax.ShapeDtypeStruct((m, d), jnp.float32),
        compiler_params=_params("parallel"),
        name="rmsnorm",
    )(x, g.reshape(1, d))


def _masked_conv(u, w_ref, pad_l, is_ctx):
    rows = u.shape[0]
    t = lax.broadcasted_iota(jnp.int32, (rows, 1), 0)
    seg = jnp.where(is_ctx, rows, GRID_W)
    pos = t & (seg - 1)
    y = None
    for j in range(w_ref.shape[0]):
        k = j - pad_l
        if k == 0:
            sh = u
        else:
            sh = pltpu.roll(u, (-k) % rows, axis=0)
            sh = jnp.where((pos + k >= 0) & (pos + k < seg), sh, 0.0)
        term = w_ref[j:j + 1, :] * sh
        y = term if y is None else y + term
    return y


def _rglru_kernel(*refs, reverse):
    if reverse:
        (p6_ref, cw_ref, cb_ref, wa_ref, ba_ref, wx_ref, bx_ref, lam_ref, hf_ref, p7_ref,
         o_ref, a_sc, b_sc, h_sc) = refs
    else:
        p6_ref, cw_ref, cb_ref, wa_ref, ba_ref, wx_ref, bx_ref, lam_ref, o_ref, a_sc, b_sc, h_sc = refs
    s = pl.program_id(1)

    @pl.when(s == 0)
    def _():
        h_sc[...] = jnp.zeros_like(h_sc)

    u = _masked_conv(p6_ref[...], cw_ref, RG_CONV_LEFT, s == 0) + cb_ref[...]
    ub = u.astype(jnp.bfloat16)
    for g in range(RG_BLOCKS):
        cols = slice(g * RG_BW, (g + 1) * RG_BW)
        ug = ub[:, cols]
        r = jax.nn.sigmoid(jnp.dot(ug, wa_ref[0, g].astype(jnp.bfloat16), preferred_element_type=jnp.float32)
                           + ba_ref[:, cols])
        i = jax.nn.sigmoid(jnp.dot(ug, wx_ref[0, g].astype(jnp.bfloat16), preferred_element_type=jnp.float32)
                           + bx_ref[:, cols])
        neg_lam = -lam_ref[:, cols]
        softplus = jnp.log1p(jnp.exp(-jnp.abs(neg_lam))) + jnp.maximum(neg_lam, 0.0)
        log_a = (-RG_C * softplus) * r
        a_sc[:, cols] = jnp.exp(log_a)
        b_sc[:, cols] = jnp.sqrt(1.0 - jnp.exp(2.0 * log_a)) * (i * u[:, cols])

    n_groups = a_sc.shape[0] // SUBLANES
    row = lax.broadcasted_iota(jnp.int32, (SUBLANES, a_sc.shape[1]), 0)

    def body(it, h_prev):
        grp = (n_groups - 1 - it) if reverse else it
        off = pl.multiple_of(grp * SUBLANES, SUBLANES)
        a = a_sc[pl.ds(off, SUBLANES), :]
        b = b_sc[pl.ds(off, SUBLANES), :]
        for k in (1, 2, 4):
            shift = (SUBLANES - k) if reverse else k
            inside = (row < SUBLANES - k) if reverse else (row >= k)
            a_s = jnp.where(inside, pltpu.roll(a, shift, axis=0), 1.0)
            b_s = jnp.where(inside, pltpu.roll(b, shift, axis=0), 0.0)
            b = a * b_s + b
            a = a * a_s
        h = b + a * h_prev
        b_sc[pl.ds(off, SUBLANES), :] = h
        return h[0:1] if reverse else h[SUBLANES - 1:SUBLANES]

    h_sc[...] = lax.fori_loop(0, n_groups, body, h_sc[...])
    if reverse:
        o_ref[...] = (jax.nn.gelu(p7_ref[...]) * (hf_ref[...] + b_sc[...])).astype(o_ref.dtype)
    else:
        o_ref[...] = b_sc[...]


def _seq_block(b, s, reverse):
    lat = (LAT_BLOCKS - s) if reverse else (s - 1)
    return jnp.where(s == 0, BATCH * LAT_BLOCKS + b, b * LAT_BLOCKS + lat)


def rglru_mixer(proj, col_in, col_gate, conv_w, conv_b, wa, ba, wx, bx, lam):
    w = BRANCH_W
    row2 = lambda a: a.reshape(1, w)
    outs = None
    for reverse in (False, True):
        d = int(reverse)
        blk = lambda col: (lambda b, s: (_seq_block(b, s, reverse), col))
        const2 = lambda b, s: (0, 0)
        in_specs = [pl.BlockSpec((SEQ_BLOCK, w), blk(col_in)),
                    pl.BlockSpec(conv_w.shape, const2),
                    pl.BlockSpec((1, w), const2),
                    pl.BlockSpec((1,) + wa.shape[1:], lambda b, s: (d, 0, 0, 0)),
                    pl.BlockSpec((1, w), const2),
                    pl.BlockSpec((1,) + wx.shape[1:], lambda b, s: (d, 0, 0, 0)),
                    pl.BlockSpec((1, w), const2),
                    pl.BlockSpec((1, w), const2)]
        args = [proj, conv_w, row2(conv_b), wa, row2(ba[d]), wx, row2(bx[d]), row2(lam[d])]
        if reverse:
            in_specs += [pl.BlockSpec((SEQ_BLOCK, w), blk(0)), pl.BlockSpec((SEQ_BLOCK, w), blk(col_gate))]
            args += [outs, proj]
        outs = pl.pallas_call(
            functools.partial(_rglru_kernel, reverse=reverse),
            grid=(BATCH, LAT_BLOCKS + 1),
            in_specs=in_specs,
            out_specs=pl.BlockSpec((SEQ_BLOCK, w), blk(0)),
            out_shape=jax.ShapeDtypeStruct((ROWS, w), jnp.bfloat16 if reverse else jnp.float32),
            scratch_shapes=[pltpu.VMEM((SEQ_BLOCK, w), jnp.float32), pltpu.VMEM((SEQ_BLOCK, w), jnp.float32),
                            pltpu.VMEM((1, w), jnp.float32)],
            compiler_params=_params("parallel", "arbitrary"),
            name="rglru_bwd" if reverse else "rglru_fwd",
        )(*args)
    return outs


def _sconv_kernel(pb_ref, pc_ref, px_ref, w_ref, o_ref):
    is_ctx = pl.program_id(0) >= BATCH * LAT_BLOCKS
    conv = _masked_conv(pc_ref[...] * px_ref[...], w_ref, SC_CONV_LEFT, is_ctx)
    o_ref[...] = (pb_ref[...] * conv).astype(o_ref.dtype)


def sconv_mixer(proj, col_b, col_c, col_x, conv_w):
    w = BRANCH_W
    spec = lambda col: pl.BlockSpec((SEQ_BLOCK, w), lambda i: (i, col))
    return pl.pallas_call(
        _sconv_kernel,
        grid=(ROWS // SEQ_BLOCK,),
        in_specs=[spec(col_b), spec(col_c), spec(col_x), pl.BlockSpec(conv_w.shape, lambda i: (0, 0))],
        out_specs=pl.BlockSpec((SEQ_BLOCK, w), lambda i: (i, 0)),
        out_shape=jax.ShapeDtypeStruct((ROWS, w), jnp.bfloat16),
        compiler_params=_params("parallel"),
        name="sconv",
    )(proj, proj, proj, conv_w)


def _merge_kernel(h_ref, wg_ref, *rest):
    y_refs, (wb_ref, o_ref, acc_ref) = rest[:N_BRANCHES], rest[N_BRANCHES:]
    b = pl.program_id(2)

    @pl.when(b == 0)
    def _():
        acc_ref[...] = jnp.zeros_like(acc_ref)

    gate = jax.nn.sigmoid(jnp.dot(h_ref[...], wg_ref[...], preferred_element_type=jnp.float32))
    y = y_refs[0][...]
    for branch in range(1, N_BRANCHES):
        y = jnp.where(b == branch, y_refs[branch][...], y)
    acc_ref[...] += gate * jnp.dot(y, wb_ref[0], preferred_element_type=jnp.float32)

    @pl.when(b == N_BRANCHES - 1)
    def _():
        o_ref[...] = acc_ref[...].astype(o_ref.dtype)


MAX_ROW_TILE = 1088


def _row_tile(m):
    return max(t for t in range(BF16_SUBLANES, MAX_ROW_TILE + 1, BF16_SUBLANES) if m % t == 0)


def merge_branches(h, w_gate, ys, w_branch, m):
    d = h.shape[1]
    bw = ys[0].shape[1]
    tm = 2 * ROW_BLOCK if m % (2 * ROW_BLOCK) == 0 else ROW_BLOCK
    tn = 1024 * ROW_BLOCK // tm
    n_col = d // tn
    y_spec = pl.BlockSpec((tm, bw), lambda i, j, b: (i, 0))
    return pl.pallas_call(
        _merge_kernel,
        grid=(m // tm, n_col, N_BRANCHES),
        in_specs=[pl.BlockSpec((tm, d), lambda i, j, b: (i, 0)),
                  pl.BlockSpec((d, tn), lambda i, j, b: (0, b * n_col + j))]
                 + [y_spec] * N_BRANCHES
                 + [pl.BlockSpec((1, bw, tn), lambda i, j, b: (b, 0, j))],
        out_specs=pl.BlockSpec((tm, tn), lambda i, j, b: (i, j)),
        out_shape=jax.ShapeDtypeStruct((m, d), jnp.bfloat16),
        scratch_shapes=[pltpu.VMEM((tm, tn), jnp.float32)],
        compiler_params=_params("parallel", "parallel", "arbitrary"),
        name="merge_branches",
    )(h, w_gate, *ys, w_branch)


def _extract_topk(s, n_top, val_ref, idx_ref, slot, rid=None):
    if rid is None:
        rid = lax.broadcasted_iota(jnp.int32, s.shape, 0).astype(jnp.float32)
    for r in range(n_top):
        m = jnp.max(s, axis=0, keepdims=True)
        am = jnp.min(jnp.where(s == m, rid, jnp.inf), axis=0, keepdims=True)
        val_ref[slot, r:r + 1, :] = m
        idx_ref[slot, r:r + 1, :] = am
        s = jnp.where(rid == am, -jnp.inf, s)


CAND_COUNTS = tuple(PEER_TOPK // (j1 + 1) for j1 in range(PEER_TOPK))
N_CAND = sum(CAND_COUNTS)
N_CAND_ROWS = _round_up(N_CAND, SUBLANES)


def _cand_flat_ids(tt):
    ids = [j1 * PEER_TOPK + j2 for j1, n2 in enumerate(CAND_COUNTS) for j2 in range(n2)]
    ids += [PEER_TOPK * PEER_TOPK + p for p in range(N_CAND_ROWS - N_CAND)]
    return jnp.broadcast_to(jnp.asarray(ids, jnp.float32)[:, None], (N_CAND_ROWS, tt))


def _lookup_rows(table, sel):
    out = jnp.zeros(sel.shape, table.dtype)
    for r in range(table.shape[0]):
        out = jnp.where(sel == r, table[r:r + 1, :], out)
    return out


def _peer_topk_kernel(q_ref, keys_ref, flat_ref, i1_ref, i2_ref, w_ref, val_sc, idx_sc, cand_sc, top_sc, pos_sc,
                      ent_sc):
    n_half = keys_ref.shape[0]
    for hp in range(n_half):
        q = q_ref[:, hp * PEER_DKH:(hp + 1) * PEER_DKH]
        s = lax.dot_general(keys_ref[hp], q, (((1,), (1,)), ((), ())), precision=lax.Precision.HIGHEST,
                            preferred_element_type=jnp.float32)
        _extract_topk(s, PEER_TOPK, val_sc, idx_sc, hp)
    cand_sc[N_CAND_ROWS - SUBLANES:N_CAND_ROWS, :] = jnp.full((SUBLANES, cand_sc.shape[1]), -jnp.inf, jnp.float32)
    for h in range(PEER_HEADS):
        v1, v2 = val_sc[2 * h], val_sc[2 * h + 1]
        row0 = 0
        for j1, n2 in enumerate(CAND_COUNTS):
            cand_sc[row0:row0 + n2, :] = v1[j1:j1 + 1, :] + v2[0:n2, :]
            row0 += n2
        _extract_topk(cand_sc[...], PEER_TOPK, top_sc, pos_sc, 0, rid=flat_ref[...])
        top, pos = top_sc[0], pos_sc[0].astype(jnp.int32)
        e = jnp.exp(top - top[0:1, :])
        rows = slice(h * PEER_TOPK, (h + 1) * PEER_TOPK)
        ent_sc[0, rows, :] = _lookup_rows(idx_sc[2 * h], pos >> TOPK_SHIFT)
        ent_sc[1, rows, :] = _lookup_rows(idx_sc[2 * h + 1], pos & (PEER_TOPK - 1))
        ent_sc[2, rows, :] = e / jnp.sum(e, axis=0, keepdims=True)
    i1_ref[...] = ent_sc[0].T.astype(jnp.int32)
    i2_ref[...] = ent_sc[1].T.astype(jnp.int32)
    w_ref[...] = ent_sc[2].T


def peer_topk(q, keys, tt=LANES):
    n_tok = q.shape[0]
    n_ent = PEER_HEADS * PEER_TOPK
    assert n_ent == tt
    ent_spec = pl.BlockSpec((tt, n_ent), lambda i: (i, 0))
    f32, i32 = jnp.float32, jnp.int32
    return pl.pallas_call(
        _peer_topk_kernel,
        grid=(n_tok // tt,),
        in_specs=[pl.BlockSpec((tt, q.shape[1]), lambda i: (i, 0)),
                  pl.BlockSpec(keys.shape, lambda i: (0, 0, 0)),
                  pl.BlockSpec((N_CAND_ROWS, tt), lambda i: (0, 0))],
        out_specs=[ent_spec, ent_spec, ent_spec],
        out_shape=[jax.ShapeDtypeStruct((n_tok, n_ent), i32), jax.ShapeDtypeStruct((n_tok, n_ent), i32),
                   jax.ShapeDtypeStruct((n_tok, n_ent), f32)],
        scratch_shapes=[pltpu.VMEM((2 * PEER_HEADS, PEER_TOPK, tt), f32), pltpu.VMEM((2 * PEER_HEADS, PEER_TOPK, tt), f32),
                        pltpu.VMEM((N_CAND_ROWS, tt), f32),
                        pltpu.VMEM((1, PEER_TOPK, tt), f32), pltpu.VMEM((1, PEER_TOPK, tt), f32),
                        pltpu.VMEM((3, n_ent, tt), f32)],
        compiler_params=_params("parallel"),
        name="peer_topk",
    )(q, keys, _cand_flat_ids(tt))


def _peer_score_kernel(h_ref, u_ref, i1_ref, i2_ref, o_ref):
    j = pl.program_id(1)

    @pl.when(j == 0)
    def _():
        o_ref[...] = jnp.zeros_like(o_ref)

    s = lax.dot_general(h_ref[...], u_ref[...].astype(jnp.bfloat16), (((1,), (1,)), ((), ())),
                        preferred_element_type=jnp.float32)
    i1, i2 = i1_ref[...], i2_ref[...]
    acc = o_ref[...]
    n_chunks = s.shape[1] // PEER_NKEYS
    for c in range(n_chunks):
        picked = jnp.take_along_axis(s[:, c * PEER_NKEYS:(c + 1) * PEER_NKEYS], i2, axis=1)
        acc = jnp.where(i1 == j * n_chunks + c, picked, acc)
    o_ref[...] = acc


def peer_scores(h, u_tab, layer, i1, i2):
    n_tok, d = h.shape
    n_exp = u_tab.shape[1]
    n_ent = i1.shape[1]
    tm = _row_tile(n_tok)
    tn = 1024 * 2 // u_tab.dtype.itemsize
    assert n_exp % tn == 0 and n_ent == PEER_NKEYS
    ent_spec = pl.BlockSpec((tm, n_ent), lambda i, j: (i, 0))
    return pl.pallas_call(
        _peer_score_kernel,
        grid=(n_tok // tm, n_exp // tn),
        in_specs=[pl.BlockSpec((tm, d), lambda i, j: (i, 0)),
                  pl.BlockSpec((None, tn, d), lambda i, j: (layer, j, 0)),
                  ent_spec, ent_spec],
        out_specs=ent_spec,
        out_shape=jax.ShapeDtypeStruct((n_tok, n_ent), jnp.float32),
        compiler_params=_params("parallel", "arbitrary"),
        name="peer_scores",
    )(h, u_tab, i1, i2)


def _peer_coef_kernel(sc_ref, w_ref, i1_ref, i2_ref, o_ref, wa_sc, ct_sc):
    wa_sc[...] = w_ref[...] * jax.nn.gelu(sc_ref[...])
    n_keys = PEER_NKEYS
    n_ent = sc_ref.shape[1]
    group = ct_sc.shape[0]
    key = lax.broadcasted_iota(jnp.int32, (n_keys, n_ent), 0)

    def body(gi, carry):
        t0 = pl.multiple_of(gi * group, group)
        for u in range(group):
            row = lambda ref: jnp.broadcast_to(ref[pl.ds(t0 + u, 1), :], (n_keys, n_ent))
            at = jnp.where(key == row(i1_ref), row(wa_sc), 0.0).astype(jnp.bfloat16)
            bt = jnp.where(key == row(i2_ref), 1.0, 0.0).astype(jnp.bfloat16)
            ct_sc[u] = lax.dot_general(at, bt, (((1,), (1,)), ((), ())), preferred_element_type=jnp.float32)
        by_key = jnp.swapaxes(ct_sc[...], 0, 1)
        for a in range(n_keys):
            o_ref[pl.ds(t0, group), a * n_keys:(a + 1) * n_keys] = by_key[a].astype(o_ref.dtype)
        return carry

    lax.fori_loop(0, sc_ref.shape[0] // group, body, 0)


def peer_coef(sc, wts, i1, i2, tb=128):
    n_tok, n_ent = sc.shape
    tb = min(tb, n_tok)
    assert n_tok % tb == 0 and tb % BF16_SUBLANES == 0
    ent_spec = pl.BlockSpec((tb, n_ent), lambda i: (i, 0))
    n_exp = PEER_NKEYS * PEER_NKEYS
    return pl.pallas_call(
        _peer_coef_kernel,
        grid=(n_tok // tb,),
        in_specs=[ent_spec] * 4,
        out_specs=pl.BlockSpec((tb, n_exp), lambda i: (i, 0)),
        out_shape=jax.ShapeDtypeStruct((n_tok, n_exp), jnp.bfloat16),
        scratch_shapes=[pltpu.VMEM((tb, n_ent), jnp.float32),
                        pltpu.VMEM((BF16_SUBLANES, PEER_NKEYS, PEER_NKEYS), jnp.float32)],
        compiler_params=_params("parallel"),
        name="peer_coef",
    )(sc, wts, i1, i2)


def peer_ffn(h, x, modtab, layer, w_q, keys, u_tab, v_tab):
    q = mm(h, w_q, tm=_row_tile(h.shape[0]), tn=512, tk=D_MODEL, layer=layer)
    i1, i2, wts = peer_topk(q, keys.reshape(2 * PEER_HEADS, PEER_NKEYS, PEER_DKH))
    sc = peer_scores(h, u_tab, layer, i1, i2)
    return mm_resid(peer_coef(sc, wts, i1, i2), v_tab, x, modtab, MOD_GATE2, layer=layer)


def _dft_parts(n):
    m = int(round(n ** 0.5))
    assert m * m == n
    part = jnp.arange(m, dtype=jnp.int32)[:, None]
    col = jnp.arange(n, dtype=jnp.int32)[None, :]
    ang_a = ((part * col) % m).astype(jnp.float32) * (2.0 * jnp.pi / m)
    ang_b = ((part * col) % n).astype(jnp.float32) * (2.0 * jnp.pi / n)
    ca, sa = jnp.cos(ang_a)[:, None, :], jnp.sin(ang_a)[:, None, :]
    cb, sb = jnp.cos(ang_b)[None, :, :], jnp.sin(ang_b)[None, :, :]
    scale = n ** -0.5
    return ((ca * cb - sa * sb) * scale).reshape(n, n), ((sa * cb + ca * sb) * scale).reshape(n, n)


def dft_tables():
    cc, sc = _dft_parts(FOURIER_GW)
    ct_l, st_l = _dft_parts(SEQ)
    ct_c, st_c = _dft_parts(CTX_LEN)
    bf16 = jnp.bfloat16
    return (jnp.concatenate([cc, sc], axis=1).astype(bf16),
            jnp.concatenate([ct_l, -st_l], axis=1).astype(bf16),
            jnp.concatenate([ct_c, -st_c], axis=1).astype(bf16))


def _fourier_kernel(p_ref, chan_ref, pos_ref, *rest):
    o_ref, gcs_sc = rest[-2:]
    seq = p_ref.shape[0]
    n_r = seq // o_ref.shape[0]
    r = pl.program_id(2)

    @pl.when(r == 0)
    def _():
        gc = jnp.dot(p_ref[...].astype(jnp.bfloat16), chan_ref[...], preferred_element_type=jnp.float32)
        gcs_sc[0:seq, :] = gc[:, :FOURIER_GW].astype(gcs_sc.dtype)
        gcs_sc[seq:2 * seq, :] = gc[:, FOURIER_GW:].astype(gcs_sc.dtype)

    @pl.when(r < n_r)
    def _():
        o_ref[...] = jnp.dot(pos_ref[...], gcs_sc[...], preferred_element_type=jnp.float32).astype(o_ref.dtype)

    if len(rest) == 3:
        @pl.when((r == n_r) & (pl.program_id(0) == 0))
        def _():
            o_ref[...] = rest[0][...]


def fourier_mixer(proj, tables, tr=ROW_BLOCK):
    chan, pos_lat, pos_ctx = tables
    gw = FOURIER_GW
    ctx_rows = BATCH * CTX_LEN
    assert ctx_rows == tr
    n_r = SEQ // tr
    chan_spec = pl.BlockSpec(chan.shape, lambda b, g, r: (0, 0))
    y_ctx = pl.pallas_call(
        _fourier_kernel,
        grid=(BATCH, FOURIER_GROUPS, 1),
        in_specs=[pl.BlockSpec((CTX_LEN, gw), lambda b, g, r: (LAT_ROWS // CTX_LEN + b, g)), chan_spec,
                  pl.BlockSpec(pos_ctx.shape, lambda b, g, r: (0, 0))],
        out_specs=pl.BlockSpec((CTX_LEN, gw), lambda b, g, r: (b, g)),
        out_shape=jax.ShapeDtypeStruct((ctx_rows, BRANCH_W), jnp.bfloat16),
        scratch_shapes=[pltpu.VMEM((2 * CTX_LEN, gw), jnp.bfloat16)],
        compiler_params=_params("parallel", "parallel", "arbitrary"),
        name="fourier_ctx",
    )(proj, chan, pos_ctx)
    return pl.pallas_call(
        _fourier_kernel,
        grid=(BATCH, FOURIER_GROUPS, n_r + 1),
        in_specs=[pl.BlockSpec((SEQ, gw), lambda b, g, r: (b, g)), chan_spec,
                  pl.BlockSpec((tr, 2 * SEQ), lambda b, g, r: (jnp.minimum(r, n_r - 1), 0)),
                  pl.BlockSpec((ctx_rows, gw), lambda b, g, r: (0, g))],
        out_specs=pl.BlockSpec((tr, gw), lambda b, g, r: (jnp.where((r == n_r) & (b == 0), BATCH * n_r,
                                                                    b * n_r + jnp.minimum(r, n_r - 1)), g)),
        out_shape=jax.ShapeDtypeStruct((ROWS, BRANCH_W), jnp.bfloat16),
        scratch_shapes=[pltpu.VMEM((2 * SEQ, gw), jnp.bfloat16)],
        compiler_params=_params("arbitrary", "arbitrary", "arbitrary"),
        name="fourier_lat",
    )(proj, chan, pos_lat, y_ctx)


def _mlstm_kernel(q_ref, k_ref, v_ref, gt_ref, o_ref, ct_sc, n_sc, m_sc):
    d = pl.program_id(0)
    f32, bf16 = jnp.float32, jnp.bfloat16
    n_t = q_ref.shape[0]

    @pl.when(pl.program_id(2) == 0)
    def _():
        ct_sc[...] = jnp.zeros_like(ct_sc)
        n_sc[...] = jnp.zeros_like(n_sc)
        m_sc[...] = jnp.zeros_like(m_sc)

    r = lax.broadcasted_iota(jnp.int32, (n_t, n_t), 0)
    c = lax.broadcasted_iota(jnp.int32, (n_t, n_t), 1)
    upto = (c - r) * (1 - 2 * d) <= 0
    eye = r == c
    for head in range(MLSTM_HEADS):
        qk_cols = slice(head * MLSTM_DQK, (head + 1) * MLSTM_DQK)
        v_cols = slice(head * MLSTM_DV, (head + 1) * MLSTM_DV)
        q = q_ref[:, qk_cols] * (MLSTM_DQK ** -0.5)
        k = k_ref[:, qk_cols]
        v = v_ref[:, v_cols]
        gate0 = d * (2 * MLSTM_HEADS) + head
        li = gt_ref[pl.ds(gate0, 1), :]
        gf = gt_ref[pl.ds(gate0 + MLSTM_HEADS, 1), :]
        lf = -(jnp.log1p(jnp.exp(-jnp.abs(gf))) + jnp.maximum(-gf, 0.0))

        b_col = jnp.sum(jnp.where(upto, lf, 0.0), axis=1, keepdims=True)
        b_row = jnp.sum(jnp.where(eye, b_col, 0.0), axis=0, keepdims=True)
        m_prev = m_sc[head]
        logw = jnp.where(upto, b_col - b_row + li, -jnp.inf)
        g_col = b_col + m_prev
        mt = jnp.maximum(g_col, jnp.max(logw, axis=1, keepdims=True))
        qb = q.astype(bf16)
        s = (lax.dot_general(qb, k.astype(bf16), (((1,), (1,)), ((), ())), preferred_element_type=f32)
             * jnp.exp(logw - mt))
        w_inter = jnp.exp(g_col - mt)
        num = (jnp.dot(s.astype(bf16), v.astype(bf16), preferred_element_type=f32)
               + w_inter * jnp.dot(qb, ct_sc[head].astype(bf16), preferred_element_type=f32))
        den = jnp.sum(s, axis=1, keepdims=True) + w_inter * jnp.sum(q * n_sc[head], axis=1, keepdims=True)
        o_ref[0, :, v_cols] = num / jnp.maximum(jnp.abs(den), jnp.exp(-mt))

        total = jnp.sum(lf, axis=1, keepdims=True)
        logu = total - b_row + li
        m_new = jnp.maximum(total + m_prev, jnp.max(logu, axis=1, keepdims=True))
        ws_row = jnp.exp(logu - m_new)
        wc = jnp.exp(total + m_prev - m_new)
        ws_col = jnp.sum(jnp.where(eye, ws_row, 0.0), axis=1, keepdims=True)
        kv = lax.dot_general(k.astype(bf16), (ws_col * v).astype(bf16), (((0,), (0,)), ((), ())),
                             preferred_element_type=f32)
        ct_sc[head] = wc * ct_sc[head] + kv
        n_sc[head] = wc * n_sc[head] + jnp.sum(ws_col * k, axis=0, keepdims=True)
        m_sc[head] = m_new


def _mlstm_chunk(d, b, s):
    n_ctx, n_lat = CTX_LEN // MLSTM_CHUNK, SEQ // MLSTM_CHUNK
    ctx_j = jnp.where(d == 0, s, n_ctx - 1 - s)
    lat_j = jnp.where(d == 0, s - n_ctx, n_ctx + n_lat - 1 - s)
    return jnp.where(s < n_ctx, BATCH * n_lat + b * n_ctx + ctx_j, b * n_lat + lat_j)


def mlstm_mixer(proj, gates_t):
    t = MLSTM_CHUNK
    qk_w = MLSTM_HEADS * MLSTM_DQK
    q_col, k_col, v_col = BRANCH_W // qk_w, BRANCH_W // qk_w + 1, 2
    n_steps = (CTX_LEN + SEQ) // t
    f32 = jnp.float32
    return pl.pallas_call(
        _mlstm_kernel,
        grid=(2, BATCH, n_steps),
        in_specs=[pl.BlockSpec((t, qk_w), lambda d, b, s: (_mlstm_chunk(d, b, s), q_col)),
                  pl.BlockSpec((t, qk_w), lambda d, b, s: (_mlstm_chunk(d, b, s), k_col)),
                  pl.BlockSpec((t, BRANCH_W), lambda d, b, s: (_mlstm_chunk(d, b, s), v_col)),
                  pl.BlockSpec((N_MLSTM_GATES, t), lambda d, b, s: (0, _mlstm_chunk(d, b, s)))],
        out_specs=pl.BlockSpec((1, t, BRANCH_W), lambda d, b, s: (d, _mlstm_chunk(d, b, s), 0)),
        out_shape=jax.ShapeDtypeStruct((2, ROWS, BRANCH_W), f32),
        scratch_shapes=[pltpu.VMEM((MLSTM_HEADS, MLSTM_DQK, MLSTM_DV), f32), pltpu.VMEM((MLSTM_HEADS, 1, MLSTM_DQK), f32),
                        pltpu.VMEM((MLSTM_HEADS, 1, 1), f32)],
        compiler_params=_params("parallel", "parallel", "arbitrary"),
        name="mlstm",
    )(proj, proj, proj, gates_t)


def _mlstm_out_kernel(hs_ref, og_ref, g_ref, o_ref):
    h = hs_ref[0] + hs_ref[1]
    for head in range(MLSTM_HEADS):
        cols = slice(head * MLSTM_DV, (head + 1) * MLSTM_DV)
        hh = h[:, cols]
        hn = hh * lax.rsqrt(jnp.mean(hh * hh, axis=-1, keepdims=True) + EPS)
        o_ref[:, cols] = (hn * g_ref[:, cols] * jax.nn.sigmoid(og_ref[:, cols])).astype(o_ref.dtype)


def mlstm_out(hs, proj, col_gate, norm_g, tm=SEQ_BLOCK):
    w = BRANCH_W
    return pl.pallas_call(
        _mlstm_out_kernel,
        grid=(ROWS // tm,),
        in_specs=[pl.BlockSpec((2, tm, w), lambda i: (0, i, 0)),
                  pl.BlockSpec((tm, w), lambda i: (i, col_gate)),
                  pl.BlockSpec((1, w), lambda i: (0, 0))],
        out_specs=pl.BlockSpec((tm, w), lambda i: (i, 0)),
        out_shape=jax.ShapeDtypeStruct((ROWS, w), jnp.bfloat16),
        compiler_params=_params("parallel"),
        name="mlstm_out",
    )(hs, proj, norm_g.reshape(1, w))


def kernel(x, c, ctx, c_ctx, w_mod, b_mod, g_norm1, g_norm2, w_in, w_branch, w_out, mlstm_gate_b, mlstm_norm_g,
           rg_conv_w, rg_conv_b, rg_wa, rg_ba, rg_wx, rg_bx, rg_lam, sc_conv_w, peer_wq, peer_keys, peer_u,
           peer_v, g_final):
    bf16 = jnp.bfloat16
    xs = jnp.concatenate([x.reshape(LAT_ROWS, D_MODEL), ctx.reshape(BATCH * CTX_LEN, D_MODEL)], axis=0)
    cond = jax.nn.silu(jnp.concatenate([c, c_ctx[None, :]], axis=0))
    tables = dft_tables()
    modtabs = adaln_tables(cond, w_mod, b_mod)
    w_out_b, peer_u_b, peer_v_b = w_out.astype(bf16), peer_u.astype(bf16), peer_v.astype(bf16)
    for l in range(DEPTH):
        last = l == DEPTH - 1
        modtab = modtabs[l]
        w_main = jnp.concatenate([w_in[l][:, :COL_MAIN], w_in[l][:, COL_GATES:COL_MERGE]], axis=1).astype(bf16)
        w_gates = jnp.pad(w_in[l][:, COL_MAIN:COL_GATES], ((0, 0), (0, LANES - N_MLSTM_GATES))).astype(bf16)
        w_merge = w_in[l][:, COL_MERGE:].astype(bf16)

        h = norm_mod(xs, g_norm1[l], modtab, MOD_SHIFT1, MOD_SCALE1)
        proj = mm(h, w_main, tm=_row_tile(ROWS), tn=512, tk=D_MODEL)
        gates = mm(h, w_gates, tm=_row_tile(ROWS), tk=D_MODEL)
        y_four = fourier_mixer(proj, tables)
        gates_t = gates[:, :N_MLSTM_GATES].T + mlstm_gate_b[l].reshape(N_MLSTM_GATES, 1)
        y_ml = mlstm_out(mlstm_mixer(proj, gates_t), proj, 3, mlstm_norm_g[l])
        y_rg = rglru_mixer(proj, 4, 5, rg_conv_w[l], rg_conv_b[l], rg_wa[l], rg_ba[l], rg_wx[l], rg_bx[l], rg_lam[l])
        y_sc = sconv_mixer(proj, 6, 7, 8, sc_conv_w[l])
        m = LAT_ROWS if last else ROWS
        merged = merge_branches(h, w_merge, (y_four, y_ml, y_rg, y_sc), w_branch[l].astype(bf16), m)
        xs = mm_resid(merged, w_out_b, xs, modtab, MOD_GATE1, layer=l)

        h2 = norm_mod(xs, g_norm2[l], modtab, MOD_SHIFT2, MOD_SCALE2)
        xs = peer_ffn(h2, xs, modtab, l, peer_wq, peer_keys[l], peer_u_b, peer_v_b)
    return rmsnorm_rows(xs, g_final).reshape(BATCH, SEQ, D_MODEL)
```
